```python
import jax, jax.numpy as jnp
from jax import lax
import numpy as np

D_MODEL = 2048
BATCH = 4
SEQ = 2048
DEPTH = 1
DEC_BATCH = 128
DEC_SEQ = 8
PAST_LEN = 16384
PAGE_SIZE = 128

D_MIX = 2 * D_MODEL
GATE_WIDTH = D_MIX // 2
GATE_HEADS = 16
GATE_HEAD_DIM = GATE_WIDTH // GATE_HEADS
CHUNK = 128
SSM_WIDTH = D_MIX - GATE_WIDTH
SSM_HEAD_DIM = 64
SSM_HEADS = SSM_WIDTH // SSM_HEAD_DIM
SSM_GROUPS = 4
SSM_STATE = 128
SSM_CONV = 4
SSM_CHUNK = 128
SSM_CONV_DIM = SSM_WIDTH + 2 * SSM_GROUPS * SSM_STATE
IN_DIM = 2 * GATE_WIDTH + SSM_WIDTH + SSM_CONV_DIM + SSM_HEADS
D_FF = 11 * D_MODEL // 4
FFN_CONV = 3
EPS = 1e-6

kernel_name = "hybrid_chunkmlp_ssd_convffn_step"


def rms_norm(x, w):
    xf = x.astype(jnp.float32)
    y = xf * lax.rsqrt(jnp.mean(xf * xf, axis=-1, keepdims=True) + EPS)
    return (y * w.astype(jnp.float32)).astype(x.dtype)


def layer_norm(x, g, b):
    xf = x.astype(jnp.float32)
    mu = jnp.mean(xf, axis=-1, keepdims=True)
    xc = xf - mu
    y = xc * lax.rsqrt(jnp.mean(xc * xc, axis=-1, keepdims=True) + EPS)
    return (y * g.astype(jnp.float32) + b.astype(jnp.float32)).astype(x.dtype)


def pad_seq(a, pad):
    return jnp.pad(a, [(0, 0), (0, pad)] + [(0, 0)] * (a.ndim - 2))


def causal_depthwise_conv(x, prev, w, b):
    k = w.shape[0]
    xp = jnp.concatenate([prev.astype(x.dtype), x], axis=1)
    out = lax.conv_general_dilated(xp, w[:, None, :].astype(x.dtype), (1,), "VALID",
                                   dimension_numbers=("NWC", "WIO", "NWC"),
                                   feature_group_count=x.shape[-1])
    return out + b.astype(x.dtype), xp[:, xp.shape[1] - (k - 1):]


def chunk_spatial_gate(a_u, a_v, ln_g, ln_b, w_s, b_s):
    bsz, L, _ = a_u.shape
    u = jax.nn.gelu(a_u, approximate=False)
    v = layer_norm(jax.nn.gelu(a_v, approximate=False), ln_g, ln_b)
    v = v.reshape(bsz, L, GATE_HEADS, GATE_HEAD_DIM)
    q = min(CHUNK, L)
    n = -(-L // q)
    vp = pad_seq(v, n * q - L).reshape(bsz, n, q, GATE_HEADS, GATE_HEAD_DIM)
    mask = np.tril(np.ones((q, q), dtype=bool))
    w = jnp.where(mask[None], w_s[:, :q, :q], 0).astype(v.dtype)
    s = jnp.einsum("hts,bnshd->bnthd", w, vp) + b_s[:, :q].T[None, None, :, :, None].astype(v.dtype)
    s = s.reshape(bsz, n * q, GATE_WIDTH)[:, :L]
    start = ((L - 1) // CHUNK) * CHUNK
    return u * s, v[:, start:]


def ssd_scan(x, dt, a, bm, cm, init_state):
    bsz, L, H, P = x.shape
    G, N = bm.shape[2], bm.shape[3]
    R = H // G
    q = min(SSM_CHUNK, L)
    nc = -(-L // q)
    pad = nc * q - L
    x, dt, bm, cm = pad_seq(x, pad), pad_seq(dt, pad), pad_seq(bm, pad), pad_seq(cm, pad)
    xdt = (x * dt[..., None]).reshape(bsz, nc, q, G, R, P)
    da_cs = jnp.cumsum((dt * a).reshape(bsz, nc, q, G, R), axis=2)
    bc = bm.reshape(bsz, nc, q, G, N)
    cc = cm.reshape(bsz, nc, q, G, N)
    seg = da_cs[:, :, :, None] - da_cs[:, :, None, :]
    mask = np.tril(np.ones((q, q), dtype=bool))[None, None, :, :, None, None]
    decay = jnp.exp(jnp.where(mask, seg, -jnp.inf))
    cb = jnp.einsum("bctgn,bcsgn->bctsg", cc, bc)
    y_diag = jnp.einsum("bctsgr,bcsgrp->bctgrp", cb[..., None] * decay, xdt)
    decay_states = jnp.exp(da_cs[:, :, -1:] - da_cs)
    states = jnp.einsum("bcsgn,bcsgr,bcsgrp->bcgrpn", bc, decay_states, xdt)
    init = init_state.reshape(bsz, 1, G, R, P, N)
    states = jnp.concatenate([init, states], axis=1)
    tot_cs = jnp.cumsum(jnp.pad(da_cs[:, :, -1], ((0, 0), (1, 0), (0, 0), (0, 0))), axis=1)
    seg_c = tot_cs[:, :, None] - tot_cs[:, None, :]
    mask_c = np.tril(np.ones((nc + 1, nc + 1), dtype=bool))[None, :, :, None, None]
    chunk_decay = jnp.exp(jnp.where(mask_c, seg_c, -jnp.inf))
    new_states = jnp.einsum("bzcgr,bcgrpn->bzgrpn", chunk_decay, states)
    states_in, final = new_states[:, :-1], new_states[:, -1]
    y_off = jnp.einsum("bctgn,bcgrpn,bctgr->bctgrp", cc, states_in, jnp.exp(da_cs))
    y = (y_diag + y_off).reshape(bsz, nc * q, H, P)[:, :L]
    return y, final.reshape(bsz, H, P, N)


def ssd_mixer(z, xbc, dt_raw, ssm_state, conv_state, conv_w, conv_b, dt_bias, a_log, d_skip, norm_w):
    bsz, L, _ = z.shape
    f32 = jnp.float32
    xbc, new_conv = causal_depthwise_conv(xbc, conv_state, conv_w, conv_b)
    xbc = jax.nn.silu(xbc)
    gn = SSM_GROUPS * SSM_STATE
    xs = xbc[..., :SSM_WIDTH].reshape(bsz, L, SSM_HEADS, SSM_HEAD_DIM).astype(f32)
    bm = xbc[..., SSM_WIDTH:SSM_WIDTH + gn].reshape(bsz, L, SSM_GROUPS, SSM_STATE).astype(f32)
    cm = xbc[..., SSM_WIDTH + gn:].reshape(bsz, L, SSM_GROUPS, SSM_STATE).astype(f32)
    dt = jax.nn.softplus(dt_raw.astype(f32) + dt_bias.astype(f32))
    a = -jnp.exp(a_log.astype(f32))
    y, new_state = ssd_scan(xs, dt, a, bm, cm, ssm_state.astype(f32))
    y = y + d_skip.astype(f32)[:, None] * xs
    y = y.reshape(bsz, L, SSM_WIDTH) * jax.nn.silu(z.astype(f32))
    yg = y.reshape(bsz, L, SSM_GROUPS, SSM_WIDTH // SSM_GROUPS)
    yg = yg * lax.rsqrt(jnp.mean(yg * yg, axis=-1, keepdims=True) + EPS)
    y = yg.reshape(bsz, L, SSM_WIDTH) * norm_w.astype(f32)
    return y.astype(z.dtype), new_state.astype(ssm_state.dtype), new_conv


def conv_ffn(h, conv_state, w_up, conv_w, conv_b, w_down):
    up = h @ w_up
    up, new_conv = causal_depthwise_conv(up, conv_state, conv_w, conv_b)
    g, u = jnp.split(up, 2, axis=-1)
    return (jax.nn.gelu(g, approximate=True) * u) @ w_down, new_conv


def decoder_layer(x, ssm_state, ssm_conv_state, ffn_conv_state, p):
    h = rms_norm(x, p["norm_mix_pre"])
    proj = h @ p["w_in"]
    o1 = GATE_WIDTH
    o2 = 2 * GATE_WIDTH
    o3 = o2 + SSM_WIDTH
    o4 = o3 + SSM_CONV_DIM
    a_u, a_v, z, xbc, dt_raw = proj[..., :o1], proj[..., o1:o2], proj[..., o2:o3], proj[..., o3:o4], proj[..., o4:]
    gate_out, chunk_v = chunk_spatial_gate(a_u, a_v, p["gate_ln_g"], p["gate_ln_b"], p["gate_w_s"], p["gate_b_s"])
    ssm_out, new_ssm, new_ssm_conv = ssd_mixer(z, xbc, dt_raw, ssm_state, ssm_conv_state,
                                              p["ssm_conv_w"], p["ssm_conv_b"], p["ssm_dt_bias"],
                                              p["ssm_a_log"], p["ssm_d"], p["ssm_norm_w"])
    mix = jnp.concatenate([gate_out, ssm_out], axis=-1) @ p["w_out"]
    x = x + rms_norm(mix, p["norm_mix_post"])
    f, new_ffn_conv = conv_ffn(rms_norm(x, p["norm_ffn_pre"]), ffn_conv_state,
                               p["ffn_w_up"], p["ffn_conv_w"], p["ffn_conv_b"], p["ffn_w_down"])
    x = x + rms_norm(f, p["norm_ffn_post"])
    return x, new_ssm, new_ssm_conv, new_ffn_conv, chunk_v


def setup_inputs(seed: int = 0) -> dict:
    key = jax.random.key(seed)
    ks = jax.random.split(key, 32)
    nrm = jax.random.normal
    f32 = jnp.float32
    dt0 = jnp.exp(jax.random.uniform(ks[12], (DEPTH, SSM_HEADS), f32, np.log(1e-3), np.log(1e-1)))
    return {
        "x_prompt": nrm(ks[0], (BATCH, SEQ, D_MODEL), f32),
        "x_sample": nrm(ks[1], (DEC_BATCH, DEC_SEQ, D_MODEL), f32),
        "state_ssm": 0.1 * nrm(ks[2], (DEPTH, DEC_BATCH, SSM_HEADS, SSM_HEAD_DIM, SSM_STATE), f32),
        "state_ssm_conv": nrm(ks[3], (DEPTH, DEC_BATCH, SSM_CONV - 1, SSM_CONV_DIM), f32),
        "state_ffn_conv": nrm(ks[4], (DEPTH, DEC_BATCH, FFN_CONV - 1, 2 * D_FF), f32),
        "norm_mix_pre": 1.0 + 0.1 * nrm(ks[5], (DEPTH, D_MODEL), f32),
        "w_in": nrm(ks[6], (DEPTH, D_MODEL, IN_DIM), f32) * D_MODEL ** -0.5,
        "gate_ln_g": 1.0 + 0.1 * nrm(ks[7], (DEPTH, GATE_WIDTH), f32),
        "gate_ln_b": 0.01 * nrm(ks[8], (DEPTH, GATE_WIDTH), f32),
        "gate_w_s": nrm(ks[9], (DEPTH, GATE_HEADS, CHUNK, CHUNK), f32) * CHUNK ** -0.5,
        "gate_b_s": 1.0 + 0.1 * nrm(ks[10], (DEPTH, GATE_HEADS, CHUNK), f32),
        "ssm_conv_w": nrm(ks[11], (DEPTH, SSM_CONV, SSM_CONV_DIM), f32) * SSM_CONV ** -0.5,
        "ssm_conv_b": 0.01 * nrm(ks[13], (DEPTH, SSM_CONV_DIM), f32),
        "ssm_dt_bias": dt0 + jnp.log(-jnp.expm1(-dt0)),
        "ssm_a_log": jnp.log(jax.random.uniform(ks[14], (DEPTH, SSM_HEADS), f32, 1.0, 16.0)),
        "ssm_d": 1.0 + 0.1 * nrm(ks[15], (DEPTH, SSM_HEADS), f32),
        "ssm_norm_w": 1.0 + 0.1 * nrm(ks[16], (DEPTH, SSM_WIDTH), f32),
        "w_out": nrm(ks[17], (DEPTH, D_MIX, D_MODEL), f32) * D_MIX ** -0.5,
        "norm_mix_post": 1.0 + 0.1 * nrm(ks[18], (DEPTH, D_MODEL), f32),
        "norm_ffn_pre": 1.0 + 0.1 * nrm(ks[19], (DEPTH, D_MODEL), f32),
        "ffn_w_up": nrm(ks[20], (DEPTH, D_MODEL, 2 * D_FF), f32) * D_MODEL ** -0.5,
        "ffn_conv_w": nrm(ks[21], (DEPTH, FFN_CONV, 2 * D_FF), f32) * FFN_CONV ** -0.5,
        "ffn_conv_b": 0.01 * nrm(ks[22], (DEPTH, 2 * D_FF), f32),
        "ffn_w_down": nrm(ks[23], (DEPTH, D_FF, D_MODEL), f32) * D_FF ** -0.5,
        "norm_ffn_post": 1.0 + 0.1 * nrm(ks[24], (DEPTH, D_MODEL), f32),
    }


def reference(x_prompt, x_sample, state_ssm, state_ssm_conv, state_ffn_conv,
              norm_mix_pre, w_in, gate_ln_g, gate_ln_b, gate_w_s, gate_b_s,
              ssm_conv_w, ssm_conv_b, ssm_dt_bias, ssm_a_log, ssm_d, ssm_norm_w,
              w_out, norm_mix_post, norm_ffn_pre, ffn_w_up, ffn_conv_w, ffn_conv_b,
              ffn_w_down, norm_ffn_post):
    bp = x_prompt.shape[0]
    yp, ys = x_prompt, x_sample
    sp_ssm, sp_sconv, sp_fconv, sp_v = [], [], [], []
    ss_ssm, ss_sconv, ss_fconv, ss_v = [], [], [], []
    for i in range(DEPTH):
        p = dict(norm_mix_pre=norm_mix_pre[i], w_in=w_in[i], gate_ln_g=gate_ln_g[i], gate_ln_b=gate_ln_b[i],
                 gate_w_s=gate_w_s[i], gate_b_s=gate_b_s[i], ssm_conv_w=ssm_conv_w[i], ssm_conv_b=ssm_conv_b[i],
                 ssm_dt_bias=ssm_dt_bias[i], ssm_a_log=ssm_a_log[i], ssm_d=ssm_d[i], ssm_norm_w=ssm_norm_w[i],
                 w_out=w_out[i], norm_mix_post=norm_mix_post[i], norm_ffn_pre=norm_ffn_pre[i],
                 ffn_w_up=ffn_w_up[i], ffn_conv_w=ffn_conv_w[i], ffn_conv_b=ffn_conv_b[i],
                 ffn_w_down=ffn_w_down[i], norm_ffn_post=norm_ffn_post[i])
        z_ssm = jnp.zeros((bp, SSM_HEADS, SSM_HEAD_DIM, SSM_STATE), state_ssm.dtype)
        z_sconv = jnp.zeros((bp, SSM_CONV - 1, SSM_CONV_DIM), x_prompt.dtype)
        z_fconv = jnp.zeros((bp, FFN_CONV - 1, 2 * D_FF), x_prompt.dtype)
        yp, a1, a2, a3, a4 = decoder_layer(yp, z_ssm, z_sconv, z_fconv, p)
        sp_ssm.append(a1); sp_sconv.append(a2); sp_fconv.append(a3); sp_v.append(a4)
        ys, b1, b2, b3, b4 = decoder_layer(ys, state_ssm[i], state_ssm_conv[i], state_ffn_conv[i], p)
        ss_ssm.append(b1); ss_sconv.append(b2); ss_fconv.append(b3); ss_v.append(b4)
    return (yp, ys,
            jnp.stack(sp_ssm), jnp.stack(sp_sconv), jnp.stack(sp_fconv), jnp.stack(sp_v),
            jnp.stack(ss_ssm), jnp.stack(ss_sconv), jnp.stack(ss_fconv), jnp.stack(ss_v))
```

```python
import functools

import jax
import jax.numpy as jnp
import numpy as np
from jax import lax
from jax.experimental import pallas as pl
from jax.experimental.pallas import tpu as pltpu

F32 = jnp.float32
BF16 = jnp.bfloat16

D_MODEL = 2048
GATE_WIDTH = 2048
GATE_HEADS = 16
GATE_HEAD_DIM = 128
CHUNK = 128
SSM_WIDTH = 2048
SSM_HEAD_DIM = 64
SSM_HEADS = 32
SSM_GROUPS = 4
SSM_STATE = 128
SSM_CONV = 4
SSM_CONV_DIM = SSM_WIDTH + 2 * SSM_GROUPS * SSM_STATE
PROJ_MAIN = 2 * GATE_WIDTH + SSM_WIDTH + SSM_CONV_DIM
D_FF = 5632
FFN_CONV = 3
EPS = 1e-6
HEADS_PER_GROUP = SSM_HEADS // SSM_GROUPS
GROUP_WIDTH = SSM_WIDTH // SSM_GROUPS

LANES = 128
SUBLANES = 8
VMEM_LIMIT = 56 * 1024 * 1024


def _cparams(sem):
    return pltpu.CompilerParams(dimension_semantics=sem, vmem_limit_bytes=VMEM_LIMIT)


def _rms(x, w):
    return x * lax.rsqrt(jnp.mean(x * x, axis=-1, keepdims=True) + EPS) * w


def _gelu_erf(x):
    return 0.5 * x * (1.0 + lax.erf(x * np.float32(0.7071067811865476)))


def _split_bf16(x, n):
    parts = []
    r = x
    for k in range(n):
        p = r.astype(BF16)
        parts.append(p)
        if k + 1 < n:
            r = r - p.astype(F32)
    return parts


def _dot(a, b):
    return jnp.dot(a, b, preferred_element_type=F32)


def _mm_split(m_bf, x, n):
    acc = None
    for p in _split_bf16(x, n):
        t = _dot(m_bf, p)
        acc = t if acc is None else acc + t
    return acc


def _expand_heads(x, e_bf):
    acc = None
    for p in _split_bf16(x, 2):
        t = _dot(p, e_bf)
        acc = t if acc is None else acc + t
    return acc


def _seg_masks(rows, seg):
    r = lax.broadcasted_iota(jnp.int32, (rows, rows), 0)
    c = lax.broadcasted_iota(jnp.int32, (rows, rows), 1)
    same = (r // seg) == (c // seg)
    return same & (c <= r), same


def _in_proj_body(x_ref, nw_ref, w_ref, wdt_ref, o_ref, dt_ref, h_s, *, n_gelu):
    j = pl.program_id(1)

    @pl.when(j == 0)
    def _():
        h = _rms(x_ref[...], nw_ref[...]).astype(BF16)
        h_s[...] = h
        dt_ref[...] = _dot(h, wdt_ref[...])

    acc = _dot(h_s[...], w_ref[...])

    @pl.when(j < n_gelu)
    def _():
        o_ref[...] = _gelu_erf(acc).astype(o_ref.dtype)

    @pl.when(j >= n_gelu)
    def _():
        o_ref[...] = acc.astype(o_ref.dtype)


def _in_proj(x2d, nw, w_main, w_dt, *, tm, tn):
    t = x2d.shape[0]
    assert t % tm == 0 and PROJ_MAIN % tn == 0 and (2 * GATE_WIDTH) % tn == 0
    return pl.pallas_call(
        functools.partial(_in_proj_body, n_gelu=2 * GATE_WIDTH // tn),
        grid=(t // tm, PROJ_MAIN // tn),
        in_specs=[
            pl.BlockSpec((tm, D_MODEL), lambda i, j: (i, 0)),
            pl.BlockSpec((1, D_MODEL), lambda i, j: (0, 0)),
            pl.BlockSpec((D_MODEL, tn), lambda i, j: (0, j)),
            pl.BlockSpec((D_MODEL, LANES), lambda i, j: (0, 0)),
        ],
        out_specs=[
            pl.BlockSpec((tm, tn), lambda i, j: (i, j)),
            pl.BlockSpec((tm, LANES), lambda i, j: (i, 0)),
        ],
        out_shape=[
            jax.ShapeDtypeStruct((t, PROJ_MAIN), BF16),
            jax.ShapeDtypeStruct((t, LANES), F32),
        ],
        scratch_shapes=[pltpu.VMEM((tm, D_MODEL), BF16)],
        compiler_params=_cparams(("arbitrary", "arbitrary")),
        name="in_proj",
    )(x2d, nw, w_main, w_dt)


def _gate_tile(gu_ref, gv_ref, lng_ref, lnb_ref, wm_s, btT_ref, mix_ref):
    g = gv_ref[...].astype(F32)
    mu = jnp.mean(g, axis=-1, keepdims=True)
    xc = g - mu
    v = xc * lax.rsqrt(jnp.mean(xc * xc, axis=-1, keepdims=True) + EPS) * lng_ref[...] + lnb_ref[...]
    vb = v.astype(BF16)
    rows = v.shape[0]
    for h in range(GATE_HEADS):
        sl = slice(h * GATE_HEAD_DIM, (h + 1) * GATE_HEAD_DIM)
        s = _dot(wm_s[h], vb[:, sl]) + jnp.broadcast_to(btT_ref[:, h:h + 1], (rows, GATE_HEAD_DIM))
        mix_ref[:, sl] = (gu_ref[:, sl].astype(F32) * s).astype(mix_ref.dtype)
    return v


def _ssd_token_level(act, dt_raw, dtb_ref, alog_ref, maskf, segf):
    xs = act[:, :SSM_WIDTH]
    bm = act[:, SSM_WIDTH:SSM_WIDTH + GROUP_WIDTH]
    cm = act[:, SSM_WIDTH + GROUP_WIDTH:]
    dt = jax.nn.softplus(dt_raw + dtb_ref[...])
    a = -jnp.exp(alog_ref[...])
    da = dt * a
    cs = _mm_split(maskf, da, 3)
    cl = _mm_split(segf, da, 3)
    return xs, bm, cm, dt, cs, cl


def _ssd_diag_pair(cb, cs, cs_t, dt_t, mask, h):
    seg = cs[:, h:h + 1] - cs_t[h:h + 1, :]
    decay = jnp.exp(jnp.where(mask, seg, -jnp.inf))
    return cb * decay * dt_t[h:h + 1, :]


def _ssd_finish(y, xs, z_ref, dexp_ref, nwm_ref, mix_ref):
    y = y + dexp_ref[...] * xs
    y = y * jax.nn.silu(z_ref[...].astype(F32))
    for g in range(SSM_GROUPS):
        sl = slice(g * GROUP_WIDTH, (g + 1) * GROUP_WIDTH)
        yg = y[:, sl]
        yg = yg * lax.rsqrt(jnp.mean(yg * yg, axis=-1, keepdims=True) + EPS) * nwm_ref[:, sl]
        mix_ref[:, GATE_WIDTH + g * GROUP_WIDTH:GATE_WIDTH + (g + 1) * GROUP_WIDTH] = yg.astype(mix_ref.dtype)


def _mixer_prompt_body(gu_ref, gv_ref, z_ref, xbc_ref, dt_ref, lng_ref, lnb_ref, wt_ref, btT_ref, cw_ref, cb_ref,
                       dtb_ref, alog_ref, dexp_ref, nwm_ref, e_ref,
                       mix_ref, cv_ref, st_ref, sc_ref,
                       wm_s, xb_s, st_s, y_s):
    b = pl.program_id(0)
    c = pl.program_id(1)
    last = c == pl.num_programs(1) - 1
    rows = CHUNK
    hist = SUBLANES
    mask, same = _seg_masks(rows, rows)

    @pl.when((b == 0) & (c == 0))
    def _():
        for h in range(GATE_HEADS):
            wm_s[h] = jnp.where(mask, wt_ref[h], 0.0).astype(BF16)

    @pl.when(c == 0)
    def _():
        xb_s[0:hist, :] = jnp.zeros((hist, SSM_CONV_DIM), F32)
        st_s[...] = jnp.zeros(st_s.shape, F32)

    v = _gate_tile(gu_ref, gv_ref, lng_ref, lnb_ref, wm_s, btT_ref, mix_ref)

    @pl.when(last)
    def _():
        cv_ref[0] = v

    xb_s[hist:hist + rows, :] = xbc_ref[...].astype(F32)
    conv = cb_ref[...]
    for k in range(SSM_CONV):
        off = hist - (SSM_CONV - 1 - k)
        conv = conv + cw_ref[k:k + 1, :] * xb_s[off:off + rows, :]
    act = jax.nn.silu(conv)

    @pl.when(last)
    def _():
        sc_ref[0] = xb_s[hist + rows - (SSM_CONV - 1):hist + rows, :]

    xb_s[0:hist, :] = xb_s[rows:rows + hist, :]

    maskf = mask.astype(BF16)
    segf = same.astype(BF16)
    xs, bm, cm, dt, cs, cl = _ssd_token_level(act, dt_ref[...], dtb_ref, alog_ref, maskf, segf)
    cs_t = cs.T
    dt_t = dt.T
    ecs = jnp.exp(cs)
    e_bf = e_ref[...]
    coef_x = _expand_heads(dt * jnp.exp(cl - cs), e_bf)
    dlast_x = _expand_heads(jnp.exp(cl[0:SUBLANES, :]), e_bf)[0:1, :]
    lane = lax.broadcasted_iota(jnp.int32, (rows, LANES), 1)
    xs_b = xs.astype(BF16)
    for g in range(SSM_GROUPS):
        cg = cm[:, g * SSM_STATE:(g + 1) * SSM_STATE]
        bg = bm[:, g * SSM_STATE:(g + 1) * SSM_STATE]
        cb = lax.dot_general(cg.astype(BF16), bg.astype(BF16), (((1,), (1,)), ((), ())), preferred_element_type=F32)
        for p in range(HEADS_PER_GROUP // 2):
            h0 = g * HEADS_PER_GROUP + 2 * p
            sl = slice((h0 // 2) * LANES, (h0 // 2 + 1) * LANES)
            rhs = jnp.concatenate([xs_b[:, sl], st_s[:, sl].astype(BF16)], axis=0)
            ys = []
            for h in (h0, h0 + 1):
                m_h = _ssd_diag_pair(cb, cs, cs_t, dt_t, mask, h)
                c_h = cg * jnp.broadcast_to(ecs[:, h:h + 1], (rows, SSM_STATE))
                lhs = jnp.concatenate([m_h.astype(BF16), c_h.astype(BF16)], axis=1)
                ys.append(_dot(lhs, rhs))
            y_s[:, sl] = jnp.where(lane < SSM_HEAD_DIM, ys[0], ys[1])
    wc = (xs * coef_x).astype(BF16)
    for g in range(SSM_GROUPS):
        sl = slice(g * GROUP_WIDTH, (g + 1) * GROUP_WIDTH)
        bg = bm[:, g * SSM_STATE:(g + 1) * SSM_STATE].astype(BF16)
        upd = lax.dot_general(bg, wc[:, sl], (((0,), (0,)), ((), ())), preferred_element_type=F32)
        st_s[:, sl] = st_s[:, sl] * dlast_x[:, sl] + upd

    @pl.when(last)
    def _():
        st_ref[0] = st_s[...].T

    _ssd_finish(y_s[...], xs, z_ref, dexp_ref, nwm_ref, mix_ref)


def _mixer_prompt(proj, dt, prm, *, nb, nc):
    rows = CHUNK
    t = nb * nc * rows
    row = lambda b, c: b * nc + c
    full = lambda shape: pl.BlockSpec(shape, lambda b, c: (0,) * len(shape))
    return pl.pallas_call(
        _mixer_prompt_body,
        grid=(nb, nc),
        in_specs=[
            pl.BlockSpec((rows, GATE_WIDTH), lambda b, c: (row(b, c), 0)),
            pl.BlockSpec((rows, GATE_WIDTH), lambda b, c: (row(b, c), 1)),
            pl.BlockSpec((rows, SSM_WIDTH), lambda b, c: (row(b, c), 2)),
            pl.BlockSpec((rows, SSM_CONV_DIM), lambda b, c: (row(b, c), 2)),
            pl.BlockSpec((rows, LANES), lambda b, c: (row(b, c), 0)),
            full((1, GATE_WIDTH)), full((1, GATE_WIDTH)),
            full((GATE_HEADS, rows, rows)), full((rows, GATE_HEADS)),
            full((SSM_CONV, SSM_CONV_DIM)), full((1, SSM_CONV_DIM)),
            full((1, LANES)), full((1, LANES)), full((1, SSM_WIDTH)), full((1, SSM_WIDTH)),
            full((LANES, SSM_WIDTH)),
        ],
        out_specs=[
            pl.BlockSpec((rows, 2 * GATE_WIDTH), lambda b, c: (row(b, c), 0)),
            pl.BlockSpec((1, rows, GATE_WIDTH), lambda b, c: (b, 0, 0)),
            pl.BlockSpec((1, SSM_WIDTH, SSM_STATE), lambda b, c: (b, 0, 0)),
            pl.BlockSpec((1, SSM_CONV - 1, SSM_CONV_DIM), lambda b, c: (b, 0, 0)),
        ],
        out_shape=[
            jax.ShapeDtypeStruct((t, 2 * GATE_WIDTH), BF16),
            jax.ShapeDtypeStruct((nb, rows, GATE_WIDTH), F32),
            jax.ShapeDtypeStruct((nb, SSM_WIDTH, SSM_STATE), F32),
            jax.ShapeDtypeStruct((nb, SSM_CONV - 1, SSM_CONV_DIM), F32),
        ],
        scratch_shapes=[
            pltpu.VMEM((GATE_HEADS, rows, rows), BF16),
            pltpu.VMEM((rows + SUBLANES, SSM_CONV_DIM), F32),
            pltpu.VMEM((SSM_STATE, SSM_WIDTH), F32),
            pltpu.VMEM((rows, SSM_WIDTH), F32),
        ],
        compiler_params=_cparams(("arbitrary", "arbitrary")),
        name="mixer_prompt",
    )(proj, proj, proj, proj, dt, prm["ln_g"], prm["ln_b"], prm["wt_p"], prm["btT_p"], prm["conv_w"], prm["conv_b"],
      prm["dtb"], prm["alog"], prm["dexp"], prm["nwm"], prm["e"])


SEQ_TILE = 8


def _state_decay_body(dt_ref, dtb_ref, alog_ref, o_ref, *, seg):
    nseq = o_ref.shape[0]
    a = -jnp.exp(alog_ref[...])
    tot = jnp.zeros(o_ref.shape, F32)
    for t in range(seg):
        d = jax.nn.softplus(dt_ref[pl.ds(t, nseq, stride=seg), :] + dtb_ref[...])
        tot = tot + d * a
    o_ref[...] = jnp.exp(tot)


def _state_decay(dt, prm, *, seg):
    nseq = dt.shape[0] // seg
    return pl.pallas_call(
        functools.partial(_state_decay_body, seg=seg),
        out_shape=jax.ShapeDtypeStruct((nseq, LANES), F32),
        name="state_decay",
    )(dt, prm["dtb"], prm["alog"])


def _mixer_sample_body(dec_ref, gu_ref, gv_ref, z_ref, xbc_ref, dt_ref, prev_ref, sin_ref,
                       lng_ref, lnb_ref, wt_ref, btT_ref, cw_ref, cb_ref,
                       dtb_ref, alog_ref, dexp_ref, nwm_ref, e_ref,
                       mix_ref, cv_ref, sout_ref, sc_ref,
                       wm_s, y_s, cm_s, bm_s, wc_s, ex_s, *, seg):
    i = pl.program_id(0)
    rows = SEQ_TILE * seg
    mask, same = _seg_masks(rows, seg)

    @pl.when(i == 0)
    def _():
        for h in range(GATE_HEADS):
            wm_s[h] = jnp.where(mask, wt_ref[h], 0.0).astype(BF16)

    cv_ref[...] = _gate_tile(gu_ref, gv_ref, lng_ref, lnb_ref, wm_s, btT_ref, mix_ref)

    x3 = xbc_ref[...].astype(F32).reshape(SEQ_TILE, seg, SSM_CONV_DIM)
    p3 = prev_ref[...].reshape(SEQ_TILE, seg, SSM_CONV_DIM)
    tpos = lax.broadcasted_iota(jnp.int32, x3.shape, 1)
    conv = cb_ref[...] + cw_ref[SSM_CONV - 1:SSM_CONV, :] * x3
    for d in range(1, SSM_CONV):
        shifted = jnp.where(tpos >= d, pltpu.roll(x3, d, axis=1), pltpu.roll(p3, d, axis=1))
        conv = conv + cw_ref[SSM_CONV - 1 - d:SSM_CONV - d, :] * shifted
    act = jax.nn.silu(conv).reshape(rows, SSM_CONV_DIM)
    sc_ref[...] = x3[:, seg - (SSM_CONV - 1):, :]

    maskf = mask.astype(BF16)
    segf = same.astype(BF16)
    xs, bm, cm, dt, cs, cl = _ssd_token_level(act, dt_ref[...], dtb_ref, alog_ref, maskf, segf)
    cs_t = cs.T
    dt_t = dt.T
    e_bf = e_ref[...]
    coef_x = _expand_heads(dt * jnp.exp(cl - cs), e_bf)
    ecs_x = _expand_heads(jnp.exp(cs), e_bf)
    lane = lax.broadcasted_iota(jnp.int32, (rows, LANES), 1)
    xs_b = xs.astype(BF16)
    for g in range(SSM_GROUPS):
        cg = cm[:, g * SSM_STATE:(g + 1) * SSM_STATE]
        bg = bm[:, g * SSM_STATE:(g + 1) * SSM_STATE]
        cb = lax.dot_general(cg.astype(BF16), bg.astype(BF16), (((1,), (1,)), ((), ())), preferred_element_type=F32)
        for p in range(HEADS_PER_GROUP // 2):
            h0 = g * HEADS_PER_GROUP + 2 * p
            sl = slice((h0 // 2) * LANES, (h0 // 2 + 1) * LANES)
            ys = [_dot(_ssd_diag_pair(cb, cs, cs_t, dt_t, mask, h).astype(BF16), xs_b[:, sl]) for h in (h0, h0 + 1)]
            y_s[:, sl] = jnp.where(lane < SSM_HEAD_DIM, ys[0], ys[1])
    cm_s[...] = cm
    bm_s[...] = bm
    wc_s[...] = xs * coef_x
    ex_s[...] = ecs_x

    rowid = lax.broadcasted_iota(jnp.int32, (rows, SSM_STATE), 0)

    def seq_step(s, carry):
        r8 = pl.ds(pl.multiple_of(s * seg, seg), seg)
        for g in range(SSM_GROUPS):
            gsl = slice(g * GROUP_WIDTH, (g + 1) * GROUP_WIDTH)
            nsl = slice(g * SSM_STATE, (g + 1) * SSM_STATE)
            st = sin_ref[s, gsl, :]
            c8 = cm_s[r8, nsl].astype(BF16)
            yo = lax.dot_general(c8, st.astype(BF16), (((1,), (1,)), ((), ())), preferred_element_type=F32)
            y_s[r8, gsl] = y_s[r8, gsl] + yo * ex_s[r8, gsl]
            bmask = jnp.where(rowid // seg == s, bm_s[:, nsl], 0.0).astype(BF16)
            upd = lax.dot_general(wc_s[:, gsl].astype(BF16), bmask, (((0,), (0,)), ((), ())), preferred_element_type=F32)
            for r in range(HEADS_PER_GROUP):
                d = dec_ref[(i * SEQ_TILE + s) * SSM_HEADS + g * HEADS_PER_GROUP + r]
                hsl = slice(r * SSM_HEAD_DIM, (r + 1) * SSM_HEAD_DIM)
                osl = slice(g * GROUP_WIDTH + r * SSM_HEAD_DIM, g * GROUP_WIDTH + (r + 1) * SSM_HEAD_DIM)
                sout_ref[s, osl, :] = st[hsl, :] * d + upd[hsl, :]
        return carry

    lax.fori_loop(0, SEQ_TILE, seq_step, 0)

    _ssd_finish(y_s[...], xs, z_ref, dexp_ref, nwm_ref, mix_ref)


def _mixer_sample(dec, proj, dt, prev, state, prm, *, seg):
    rows = SEQ_TILE * seg
    t = proj.shape[0]
    nseq = t // seg
    assert t % rows == 0
    full = lambda shape: pl.BlockSpec(shape, lambda i: (0,) * len(shape))
    return pl.pallas_call(
        functools.partial(_mixer_sample_body, seg=seg),
        grid=(t // rows,),
        in_specs=[
            pl.BlockSpec(memory_space=pltpu.SMEM),
            pl.BlockSpec((rows, GATE_WIDTH), lambda i: (i, 0)),
            pl.BlockSpec((rows, GATE_WIDTH), lambda i: (i, 1)),
            pl.BlockSpec((rows, SSM_WIDTH), lambda i: (i, 2)),
            pl.BlockSpec((rows, SSM_CONV_DIM), lambda i: (i, 2)),
            pl.BlockSpec((rows, LANES), lambda i: (i, 0)),
            pl.BlockSpec((rows, SSM_CONV_DIM), lambda i: (i, 0)),
            pl.BlockSpec((SEQ_TILE, SSM_WIDTH, SSM_STATE), lambda i: (i, 0, 0)),
            full((1, GATE_WIDTH)), full((1, GATE_WIDTH)),
            full((GATE_HEADS, rows, rows)), full((rows, GATE_HEADS)),
            full((SSM_CONV, SSM_CONV_DIM)), full((1, SSM_CONV_DIM)),
            full((1, LANES)), full((1, LANES)), full((1, SSM_WIDTH)), full((1, SSM_WIDTH)),
            full((LANES, SSM_WIDTH)),
        ],
        out_specs=[
            pl.BlockSpec((rows, 2 * GATE_WIDTH), lambda i: (i, 0)),
            pl.BlockSpec((rows, GATE_WIDTH), lambda i: (i, 0)),
            pl.BlockSpec((SEQ_TILE, SSM_WIDTH, SSM_STATE), lambda i: (i, 0, 0)),
            pl.BlockSpec((SEQ_TILE, SSM_CONV - 1, SSM_CONV_DIM), lambda i: (i, 0, 0)),
        ],
        out_shape=[
            jax.ShapeDtypeStruct((t, 2 * GATE_WIDTH), BF16),
            jax.ShapeDtypeStruct((t, GATE_WIDTH), F32),
            jax.ShapeDtypeStruct((nseq, SSM_WIDTH, SSM_STATE), F32),
            jax.ShapeDtypeStruct((nseq, SSM_CONV - 1, SSM_CONV_DIM), F32),
        ],
        scratch_shapes=[
            pltpu.VMEM((GATE_HEADS, rows, rows), BF16),
            pltpu.VMEM((rows, SSM_WIDTH), F32),
            pltpu.VMEM((rows, GROUP_WIDTH), F32),
            pltpu.VMEM((rows, GROUP_WIDTH), F32),
            pltpu.VMEM((rows, SSM_WIDTH), F32),
            pltpu.VMEM((rows, SSM_WIDTH), F32),
        ],
        compiler_params=_cparams(("arbitrary",)),
        name="mixer_sample",
    )(dec, proj, proj, proj, proj, dt, prev, state, prm["ln_g"], prm["ln_b"], prm["wt_s"], prm["btT_s"],
      prm["conv_w"], prm["conv_b"], prm["dtb"], prm["alog"], prm["dexp"], prm["nwm"], prm["e"])


def _out_proj_body(m_ref, x_ref, w_ref, npost_ref, npre_ref, x1_ref, h2_ref):
    mix = _dot(m_ref[...], w_ref[...])
    x1 = x_ref[...] + _rms(mix, npost_ref[...])
    x1_ref[...] = x1
    h2_ref[...] = _rms(x1, npre_ref[...]).astype(h2_ref.dtype)


def _out_proj(mixin, x2d, w_out, npost, npre, *, tm):
    t, k = mixin.shape
    assert t % tm == 0
    return pl.pallas_call(
        _out_proj_body,
        grid=(t // tm,),
        in_specs=[
            pl.BlockSpec((tm, k), lambda i: (i, 0)),
            pl.BlockSpec((tm, D_MODEL), lambda i: (i, 0)),
            pl.BlockSpec((k, D_MODEL), lambda i: (0, 0), pipeline_mode=pl.Buffered(1)),
            pl.BlockSpec((1, D_MODEL), lambda i: (0, 0)),
            pl.BlockSpec((1, D_MODEL), lambda i: (0, 0)),
        ],
        out_specs=[
            pl.BlockSpec((tm, D_MODEL), lambda i: (i, 0)),
            pl.BlockSpec((tm, D_MODEL), lambda i: (i, 0)),
        ],
        out_shape=[
            jax.ShapeDtypeStruct((t, D_MODEL), F32),
            jax.ShapeDtypeStruct((t, D_MODEL), BF16),
        ],
        compiler_params=_cparams(("arbitrary",)),
        name="out_proj",
    )(mixin, x2d, w_out, npost, npre)


def _ffn_conv_taps(cur, shifted, cw_ref, cb_ref):
    out = cb_ref[...] + cw_ref[FFN_CONV - 1:FFN_CONV, :] * cur
    for d in range(1, FFN_CONV):
        out = out + cw_ref[FFN_CONV - 1 - d:FFN_CONV - d, :] * shifted[d - 1]
    return out


def _ffn_up_prompt_body(h_ref, wg_ref, wu_ref, cwg_ref, cwu_ref, cbg_ref, cbu_ref, a_ref, st_ref,
                        xg_s, xu_s, *, tiles_per_seq):
    i = pl.program_id(1)
    tm = h_ref.shape[0]
    hist = SUBLANES
    first = (i % tiles_per_seq) == 0
    h = h_ref[...]
    convs = []
    for half, (w_ref, x_s, cw_ref, cb_ref) in enumerate(
            ((wg_ref, xg_s, cwg_ref, cbg_ref), (wu_ref, xu_s, cwu_ref, cbu_ref))):
        @pl.when(first)
        def _():
            x_s[0:hist, :] = jnp.zeros((hist, x_s.shape[1]), F32)

        x_s[hist:hist + tm, :] = _dot(h, w_ref[...])
        shifted = [x_s[hist - d:hist - d + tm, :] for d in range(1, FFN_CONV)]
        convs.append(_ffn_conv_taps(x_s[hist:hist + tm, :], shifted, cw_ref, cb_ref))
        st_ref[0, :, half, :] = x_s[hist + tm - (FFN_CONV - 1):hist + tm, :]
        x_s[0:hist, :] = x_s[tm:tm + hist, :]
    a_ref[...] = (jax.nn.gelu(convs[0], approximate=True) * convs[1]).astype(a_ref.dtype)


def _ffn_up_sample_body(h_ref, wg_ref, wu_ref, cwg_ref, cwu_ref, cbg_ref, cbu_ref, pg_ref, pu_ref, a_ref, st_ref, *, seg):
    tm = h_ref.shape[0]
    tn = wg_ref.shape[1]
    nseq = tm // seg
    h = h_ref[...]
    tpos = lax.broadcasted_iota(jnp.int32, (nseq, seg, tn), 1)
    convs = []
    for half, (w_ref, p_ref, cw_ref, cb_ref) in enumerate(
            ((wg_ref, pg_ref, cwg_ref, cbg_ref), (wu_ref, pu_ref, cwu_ref, cbu_ref))):
        x3 = _dot(h, w_ref[...]).reshape(nseq, seg, tn)
        p3 = p_ref[...].reshape(nseq, seg, tn)
        shifted = [jnp.where(tpos >= d, pltpu.roll(x3, d, axis=1), pltpu.roll(p3, d, axis=1)) for d in range(1, FFN_CONV)]
        convs.append(_ffn_conv_taps(x3, shifted, cw_ref, cb_ref))
        st_ref[:, :, half, :] = x3[:, seg - (FFN_CONV - 1):, :]
    act = jax.nn.gelu(convs[0], approximate=True) * convs[1]
    a_ref[...] = act.reshape(tm, tn).astype(a_ref.dtype)


def _ffn_up(h2, w_up, cw, cb, prev, *, nseq, seg, tm, tn):
    t = h2.shape[0]
    nj = D_FF // tn
    assert t % tm == 0 and D_FF % tn == 0 and t == nseq * seg
    common_in = [
        pl.BlockSpec((tm, D_MODEL), lambda j, i: (i, 0)),
        pl.BlockSpec((D_MODEL, tn), lambda j, i: (0, j)),
        pl.BlockSpec((D_MODEL, tn), lambda j, i: (0, j + nj)),
        pl.BlockSpec((FFN_CONV, tn), lambda j, i: (0, j)),
        pl.BlockSpec((FFN_CONV, tn), lambda j, i: (0, j + nj)),
        pl.BlockSpec((1, tn), lambda j, i: (0, j)),
        pl.BlockSpec((1, tn), lambda j, i: (0, j + nj)),
    ]
    a_spec = pl.BlockSpec((tm, tn), lambda j, i: (i, j))
    out_shape = [
        jax.ShapeDtypeStruct((t, D_FF), BF16),
        jax.ShapeDtypeStruct((nseq, FFN_CONV - 1, 2, D_FF), F32),
    ]
    if prev is None:
        assert seg % tm == 0
        tps = seg // tm
        return pl.pallas_call(
            functools.partial(_ffn_up_prompt_body, tiles_per_seq=tps),
            grid=(nj, t // tm),
            in_specs=common_in,
            out_specs=[a_spec, pl.BlockSpec((1, FFN_CONV - 1, 2, tn), lambda j, i: (i // tps, 0, 0, j))],
            out_shape=out_shape,
            scratch_shapes=[pltpu.VMEM((tm + SUBLANES, tn), F32), pltpu.VMEM((tm + SUBLANES, tn), F32)],
            compiler_params=_cparams(("arbitrary", "arbitrary")),
            name="ffn_up_prompt",
        )(h2, w_up, w_up, cw, cw, cb, cb)
    assert tm % seg == 0 and seg == SUBLANES
    return pl.pallas_call(
        functools.partial(_ffn_up_sample_body, seg=seg),
        grid=(nj, t // tm),
        in_specs=common_in + [
            pl.BlockSpec((tm, tn), lambda j, i: (i, j)),
            pl.BlockSpec((tm, tn), lambda j, i: (i, j + nj)),
        ],
        out_specs=[a_spec, pl.BlockSpec((tm // seg, FFN_CONV - 1, 2, tn), lambda j, i: (i, 0, 0, j))],
        out_shape=out_shape,
        compiler_params=_cparams(("arbitrary", "arbitrary")),
        name="ffn_up_sample",
    )(h2, w_up, w_up, cw, cw, cb, cb, prev, prev)


def _ffn_down_body(a_ref, x_ref, w_ref, nw_ref, y_ref):
    f = _dot(a_ref[...], w_ref[...])
    y_ref[...] = x_ref[...] + _rms(f, nw_ref[...])


def _ffn_down(act, x1, w_down, nw, *, tm):
    t, k = act.shape
    assert t % tm == 0
    return pl.pallas_call(
        _ffn_down_body,
        grid=(t // tm,),
        in_specs=[
            pl.BlockSpec((tm, k), lambda i: (i, 0)),
            pl.BlockSpec((tm, D_MODEL), lambda i: (i, 0)),
            pl.BlockSpec((k, D_MODEL), lambda i: (0, 0), pipeline_mode=pl.Buffered(1)),
            pl.BlockSpec((1, D_MODEL), lambda i: (0, 0)),
        ],
        out_specs=pl.BlockSpec((tm, D_MODEL), lambda i: (i, 0)),
        out_shape=jax.ShapeDtypeStruct((t, D_MODEL), F32),
        compiler_params=_cparams(("arbitrary",)),
        name="ffn_down",
    )(act, x1, w_down, nw)


def _head_expander():
    e = np.zeros((LANES, SSM_WIDTH), np.float32)
    for h in range(SSM_HEADS):
        e[h, h * SSM_HEAD_DIM:(h + 1) * SSM_HEAD_DIM] = 1.0
    return jnp.asarray(e, BF16)


def _pad_lanes(v):
    return jnp.pad(v, (0, LANES - v.shape[0]))[None, :]


def _prep_params(norm_mix_pre, w_in, gate_ln_g, gate_ln_b, gate_w_s, gate_b_s, ssm_conv_w, ssm_conv_b, ssm_dt_bias,
                 ssm_a_log, ssm_d, ssm_norm_w, w_out, norm_mix_post, norm_ffn_pre, ffn_w_up, ffn_conv_w, ffn_conv_b,
                 ffn_w_down, norm_ffn_post, seg_sample):
    rep = CHUNK // seg_sample
    ws_small = gate_w_s[:, :seg_sample, :seg_sample]
    return dict(
        nw_pre=norm_mix_pre[None, :],
        w_main=w_in[:, :PROJ_MAIN].astype(BF16),
        w_dt=jnp.pad(w_in[:, PROJ_MAIN:], ((0, 0), (0, LANES - SSM_HEADS))).astype(BF16),
        ln_g=gate_ln_g[None, :], ln_b=gate_ln_b[None, :],
        wt_p=gate_w_s, btT_p=gate_b_s.T,
        wt_s=jnp.tile(ws_small, (1, SEQ_TILE, SEQ_TILE)), btT_s=jnp.tile(gate_b_s[:, :seg_sample], (1, SEQ_TILE)).T,
        conv_w=ssm_conv_w, conv_b=ssm_conv_b[None, :],
        dtb=_pad_lanes(ssm_dt_bias), alog=_pad_lanes(ssm_a_log),
        dexp=jnp.repeat(ssm_d, SSM_HEAD_DIM)[None, :], nwm=ssm_norm_w[None, :],
        e=_head_expander(),
        w_out=w_out.astype(BF16), n_post=norm_mix_post[None, :], n_pre2=norm_ffn_pre[None, :],
        w_up=ffn_w_up.astype(BF16), fcw=ffn_conv_w, fcb=ffn_conv_b[None, :],
        w_down=ffn_w_down.astype(BF16), n_post2=norm_ffn_post[None, :],
    )


def _pad_history(state, seg):
    nseq, k1, c = state.shape
    return jnp.pad(state, ((0, 0), (seg - k1, 0), (0, 0))).reshape(nseq * seg, c)


def _dense_tiles(t):
    tm = 512 if t % 512 == 0 else (256 if t % 256 == 0 else 64)
    return tm


def _layer_prompt(x, prm):
    nb, seq, _ = x.shape
    assert seq % CHUNK == 0
    nc = seq // CHUNK
    x2d = x.reshape(nb * seq, D_MODEL)
    tm = min(_dense_tiles(nb * seq), seq)
    proj, dt = _in_proj(x2d, prm["nw_pre"], prm["w_main"], prm["w_dt"], tm=tm, tn=512)
    mixin, cv, st, sc = _mixer_prompt(proj, dt, prm, nb=nb, nc=nc)
    x1, h2 = _out_proj(mixin, x2d, prm["w_out"], prm["n_post"], prm["n_pre2"], tm=min(tm, 256))
    act, fst = _ffn_up(h2, prm["w_up"], prm["fcw"], prm["fcb"], None, nseq=nb, seg=seq, tm=tm, tn=512)
    y = _ffn_down(act, x1, prm["w_down"], prm["n_post2"], tm=min(tm, 256))
    return (y.reshape(nb, seq, D_MODEL),
            st.reshape(nb, SSM_HEADS, SSM_HEAD_DIM, SSM_STATE),
            sc,
            fst.reshape(nb, FFN_CONV - 1, 2 * D_FF),
            cv.reshape(nb, CHUNK, GATE_HEADS, GATE_HEAD_DIM))


def _layer_sample(x, state_ssm, state_sconv, state_fconv, prm):
    nb, seg, _ = x.shape
    assert seg == SUBLANES and nb % SEQ_TILE == 0
    t = nb * seg
    x2d = x.reshape(t, D_MODEL)
    tm = _dense_tiles(t)
    proj, dt = _in_proj(x2d, prm["nw_pre"], prm["w_main"], prm["w_dt"], tm=tm, tn=512)
    dec = _state_decay(dt, prm, seg=seg)[:, :SSM_HEADS].reshape(nb * SSM_HEADS)
    mixin, cv, st, sc = _mixer_sample(dec, proj, dt, _pad_history(state_sconv, seg),
                                      state_ssm.reshape(nb, SSM_WIDTH, SSM_STATE), prm, seg=seg)
    x1, h2 = _out_proj(mixin, x2d, prm["w_out"], prm["n_post"], prm["n_pre2"], tm=min(tm, 256))
    act, fst = _ffn_up(h2, prm["w_up"], prm["fcw"], prm["fcb"], _pad_history(state_fconv, seg),
                       nseq=nb, seg=seg, tm=tm, tn=512)
    y = _ffn_down(act, x1, prm["w_down"], prm["n_post2"], tm=min(tm, 256))
    return (y.reshape(nb, seg, D_MODEL),
            st.reshape(nb, SSM_HEADS, SSM_HEAD_DIM, SSM_STATE),
            sc,
            fst.reshape(nb, FFN_CONV - 1, 2 * D_FF),
            cv.reshape(nb, seg, GATE_HEADS, GATE_HEAD_DIM))


def kernel(x_prompt, x_sample, state_ssm, state_ssm_conv, state_ffn_conv, norm_mix_pre, w_in, gate_ln_g, gate_ln_b,
           gate_w_s, gate_b_s, ssm_conv_w, ssm_conv_b, ssm_dt_bias, ssm_a_log, ssm_d, ssm_norm_w, w_out,
           norm_mix_post, norm_ffn_pre, ffn_w_up, ffn_conv_w, ffn_conv_b, ffn_w_down, norm_ffn_post):
    depth = w_in.shape[0]
    yp, ys = x_prompt, x_sample
    outs_p, outs_s = [], []
    for l in range(depth):
        prm = _prep_params(norm_mix_pre[l], w_in[l], gate_ln_g[l], gate_ln_b[l], gate_w_s[l], gate_b_s[l],
                           ssm_conv_w[l], ssm_conv_b[l], ssm_dt_bias[l], ssm_a_log[l], ssm_d[l], ssm_norm_w[l],
                           w_out[l], norm_mix_post[l], norm_ffn_pre[l], ffn_w_up[l], ffn_conv_w[l], ffn_conv_b[l],
                           ffn_w_down[l], norm_ffn_post[l], x_sample.shape[1])
        yp, *rest_p = _layer_prompt(yp, prm)
        ys, *rest_s = _layer_sample(ys, state_ssm[l], state_ssm_conv[l], state_ffn_conv[l], prm)
        outs_p.append(rest_p)
        outs_s.append(rest_s)
    stack = lambda outs, k: jnp.stack([o[k] for o in outs])
    return (yp, ys,
            stack(outs_p, 0), stack(outs_p, 1), stack(outs_p, 2), stack(outs_p, 3),
            stack(outs_s, 0), stack(outs_s, 1), stack(outs_s, 2), stack(outs_s, 3))
```

```python
import functools

import jax
import jax.numpy as jnp
import numpy as np
from jax import lax
from jax.experimental import pallas as pl
from jax.experimental.pallas import tpu as pltpu

F32 = jnp.float32
BF16 = jnp.bfloat16

D_MODEL = 2048
GATE_WIDTH = 2048
GATE_HEADS = 16
GATE_HEAD_DIM = 128
CHUNK = 128
SSM_WIDTH = 2048
SSM_HEAD_DIM = 64
SSM_HEADS = 32
SSM_GROUPS = 4
SSM_STATE = 128
SSM_CONV = 4
SSM_CONV_DIM = SSM_WIDTH + 2 * SSM_GROUPS * SSM_STATE
PROJ_MAIN = 2 * GATE_WIDTH + SSM_WIDTH + SSM_CONV_DIM
D_FF = 5632
FFN_CONV = 3
EPS = 1e-6
HEADS_PER_GROUP = SSM_HEADS // SSM_GROUPS
GROUP_WIDTH = SSM_WIDTH // SSM_GROUPS

LANES = 128
SUBLANES = 8
VMEM_LIMIT = 56 * 1024 * 1024


def _cparams(sem):
    return pltpu.CompilerParams(dimension_semantics=sem, vmem_limit_bytes=VMEM_LIMIT)


def _rms(x, w):
    return x * lax.rsqrt(jnp.mean(x * x, axis=-1, keepdims=True) + EPS) * w


def _gelu_erf(x):
    return 0.5 * x * (1.0 + lax.erf(x * np.float32(0.7071067811865476)))


def _split_bf16(x, n):
    parts = []
    r = x
    for k in range(n):
        p = r.astype(BF16)
        parts.append(p)
        if k + 1 < n:
            r = r - p.astype(F32)
    return parts


def _dot(a, b):
    return jnp.dot(a, b, preferred_element_type=F32)


def _mm_split(m_bf, x, n):
    acc = None
    for p in _split_bf16(x, n):
        t = _dot(m_bf, p)
        acc = t if acc is None else acc + t
    return acc


def _expand_heads(x, e_bf):
    acc = None
    for p in _split_bf16(x, 2):
        t = _dot(p, e_bf)
        acc = t if acc is None else acc + t
    return acc


def _seg_masks(rows, seg):
    r = lax.broadcasted_iota(jnp.int32, (rows, rows), 0)
    c = lax.broadcasted_iota(jnp.int32, (rows, rows), 1)
    same = (r // seg) == (c // seg)
    return same & (c <= r), same


def _pre_norm_body(x_ref, nw_ref, wdt_ref, h_ref, dt_ref):
    h = _rms(x_ref[...], nw_ref[...]).astype(BF16)
    h_ref[...] = h
    dt_ref[...] = _dot(h, wdt_ref[...])


def _pre_norm(x2d, nw, w_dt, *, tm):
    t = x2d.shape[0]
    assert t % tm == 0
    return pl.pallas_call(
        _pre_norm_body,
        grid=(t // tm,),
        in_specs=[
            pl.BlockSpec((tm, D_MODEL), lambda i: (i, 0)),
            pl.BlockSpec((1, D_MODEL), lambda i: (0, 0)),
            pl.BlockSpec((D_MODEL, LANES), lambda i: (0, 0)),
        ],
        out_specs=[
            pl.BlockSpec((tm, D_MODEL), lambda i: (i, 0)),
            pl.BlockSpec((tm, LANES), lambda i: (i, 0)),
        ],
        out_shape=[
            jax.ShapeDtypeStruct((t, D_MODEL), BF16),
            jax.ShapeDtypeStruct((t, LANES), F32),
        ],
        compiler_params=_cparams(("arbitrary",)),
        name="pre_norm",
    )(x2d, nw, w_dt)


def _in_proj_body(h_ref, w_ref, o_ref, wb_s, *, n_gelu, sub):
    j = pl.program_id(0)
    i = pl.program_id(1)
    tm = h_ref.shape[0]

    @pl.when(i == 0)
    def _():
        wb_s[...] = w_ref[...].astype(BF16)

    def run(epilogue):
        for rs in _row_subs(tm, sub):
            o_ref[rs, :] = epilogue(_dot(h_ref[rs, :], wb_s[...])).astype(o_ref.dtype)

    @pl.when(j < n_gelu)
    def _():
        run(_gelu_erf)

    @pl.when(j >= n_gelu)
    def _():
        run(lambda a: a)


def _in_proj(h, w_in, *, tm, tn, sub):
    t = h.shape[0]
    assert t % tm == 0 and PROJ_MAIN % tn == 0 and (2 * GATE_WIDTH) % tn == 0 and tm % sub == 0
    return pl.pallas_call(
        functools.partial(_in_proj_body, n_gelu=2 * GATE_WIDTH // tn, sub=sub),
        grid=(PROJ_MAIN // tn, t // tm),
        in_specs=[
            pl.BlockSpec((tm, D_MODEL), lambda j, i: (i, 0)),
            pl.BlockSpec((D_MODEL, tn), lambda j, i: (0, j)),
        ],
        out_specs=pl.BlockSpec((tm, tn), lambda j, i: (i, j)),
        out_shape=jax.ShapeDtypeStruct((t, PROJ_MAIN), BF16),
        scratch_shapes=[pltpu.VMEM((D_MODEL, tn), BF16)],
        compiler_params=_cparams(("arbitrary", "arbitrary")),
        name="in_proj",
    )(h, w_in)


def _gate_tile(gu_ref, gv_ref, lng_ref, lnb_ref, wm_s, btT_ref, mix_ref):
    g = gv_ref[...].astype(F32)
    mu = jnp.mean(g, axis=-1, keepdims=True)
    xc = g - mu
    v = xc * lax.rsqrt(jnp.mean(xc * xc, axis=-1, keepdims=True) + EPS) * lng_ref[...] + lnb_ref[...]
    vb = v.astype(BF16)
    rows = v.shape[0]
    for h in range(GATE_HEADS):
        sl = slice(h * GATE_HEAD_DIM, (h + 1) * GATE_HEAD_DIM)
        s = _dot(wm_s[h], vb[:, sl]) + jnp.broadcast_to(btT_ref[:, h:h + 1], (rows, GATE_HEAD_DIM))
        mix_ref[:, sl] = (gu_ref[:, sl].astype(F32) * s).astype(mix_ref.dtype)
    return v


def _ssd_token_level(act, dt_raw, dtb_ref, alog_ref, maskf, segf):
    xs = act[:, :SSM_WIDTH]
    bm = act[:, SSM_WIDTH:SSM_WIDTH + GROUP_WIDTH]
    cm = act[:, SSM_WIDTH + GROUP_WIDTH:]
    dt = jax.nn.softplus(dt_raw + dtb_ref[...])
    a = -jnp.exp(alog_ref[...])
    da = dt * a
    cs = _mm_split(maskf, da, 3)
    cl = _mm_split(segf, da, 3)
    return xs, bm, cm, dt, cs, cl


def _ssd_diag_pair(cb, cs, cs_t, dt_t, mask, h):
    seg = cs[:, h:h + 1] - cs_t[h:h + 1, :]
    decay = jnp.exp(jnp.where(mask, seg, -jnp.inf))
    return cb * decay * dt_t[h:h + 1, :]


def _ssd_finish(y, xs, z_ref, dexp_ref, nwm_ref, mix_ref):
    y = y + dexp_ref[...] * xs
    y = y * jax.nn.silu(z_ref[...].astype(F32))
    for g in range(SSM_GROUPS):
        sl = slice(g * GROUP_WIDTH, (g + 1) * GROUP_WIDTH)
        yg = y[:, sl]
        yg = yg * lax.rsqrt(jnp.mean(yg * yg, axis=-1, keepdims=True) + EPS) * nwm_ref[:, sl]
        mix_ref[:, GATE_WIDTH + g * GROUP_WIDTH:GATE_WIDTH + (g + 1) * GROUP_WIDTH] = yg.astype(mix_ref.dtype)


def _mixer_prompt_body(gu_ref, gv_ref, z_ref, xbc_ref, dt_ref, lng_ref, lnb_ref, wt_ref, btT_ref, cw_ref, cb_ref,
                       dtb_ref, alog_ref, dexp_ref, nwm_ref, e_ref,
                       mix_ref, cv_ref, st_ref, sc_ref,
                       wm_s, shift_s, xx_s, st_s, y_s):
    b = pl.program_id(0)
    c = pl.program_id(1)
    last = c == pl.num_programs(1) - 1
    rows = CHUNK
    keep = 2 * SUBLANES
    mask, same = _seg_masks(rows, rows)

    @pl.when((b == 0) & (c == 0))
    def _():
        for h in range(GATE_HEADS):
            wm_s[h] = jnp.where(mask, wt_ref[h], 0.0).astype(BF16)
        r = lax.broadcasted_iota(jnp.int32, shift_s.shape, 0)
        col = lax.broadcasted_iota(jnp.int32, shift_s.shape, 1)
        shift_s[...] = jnp.where(col == rows + r % rows - (r // rows + 1), 1.0, 0.0).astype(BF16)
        xx_s[0:rows, :] = jnp.zeros((rows, SSM_CONV_DIM), BF16)

    @pl.when(c == 0)
    def _():
        xx_s[rows - keep:rows, :] = jnp.zeros((keep, SSM_CONV_DIM), BF16)
        st_s[...] = jnp.zeros(st_s.shape, F32)

    v = _gate_tile(gu_ref, gv_ref, lng_ref, lnb_ref, wm_s, btT_ref, mix_ref)

    @pl.when(last)
    def _():
        cv_ref[0] = v

    xx_s[rows:2 * rows, :] = xbc_ref[...]
    sh = _dot(shift_s[...], xx_s[...])
    conv = cb_ref[...] + cw_ref[SSM_CONV - 1:SSM_CONV, :] * xbc_ref[...].astype(F32)
    for d in range(1, SSM_CONV):
        conv = conv + cw_ref[SSM_CONV - 1 - d:SSM_CONV - d, :] * sh[(d - 1) * rows:d * rows, :]
    act = jax.nn.silu(conv)

    @pl.when(last)
    def _():
        sc_ref[0] = xbc_ref[rows - (SSM_CONV - 1):rows, :].astype(F32)

    xx_s[rows - keep:rows, :] = xx_s[2 * rows - keep:2 * rows, :]

    maskf = mask.astype(BF16)
    segf = same.astype(BF16)
    xs, bm, cm, dt, cs, cl = _ssd_token_level(act, dt_ref[...], dtb_ref, alog_ref, maskf, segf)
    cs_t = cs.T
    dt_t = dt.T
    ecs = jnp.exp(cs)
    e_bf = e_ref[...]
    coef_x = _expand_heads(dt * jnp.exp(cl - cs), e_bf)
    dlast_x = _expand_heads(jnp.exp(cl[0:SUBLANES, :]), e_bf)[0:1, :]
    lane = lax.broadcasted_iota(jnp.int32, (rows, LANES), 1)
    xs_b = xs.astype(BF16)
    for g in range(SSM_GROUPS):
        cg = cm[:, g * SSM_STATE:(g + 1) * SSM_STATE]
        bg = bm[:, g * SSM_STATE:(g + 1) * SSM_STATE]
        cb = lax.dot_general(cg.astype(BF16), bg.astype(BF16), (((1,), (1,)), ((), ())), preferred_element_type=F32)
        for p in range(HEADS_PER_GROUP // 2):
            h0 = g * HEADS_PER_GROUP + 2 * p
            sl = slice((h0 // 2) * LANES, (h0 // 2 + 1) * LANES)
            rhs = jnp.concatenate([xs_b[:, sl], st_s[:, sl].astype(BF16)], axis=0)
            ys = []
            for h in (h0, h0 + 1):
                m_h = _ssd_diag_pair(cb, cs, cs_t, dt_t, mask, h)
                c_h = cg * jnp.broadcast_to(ecs[:, h:h + 1], (rows, SSM_STATE))
                lhs = jnp.concatenate([m_h.astype(BF16), c_h.astype(BF16)], axis=1)
                ys.append(_dot(lhs, rhs))
            y_s[:, sl] = jnp.where(lane < SSM_HEAD_DIM, ys[0], ys[1])
    wc = (xs * coef_x).astype(BF16)
    for g in range(SSM_GROUPS):
        sl = slice(g * GROUP_WIDTH, (g + 1) * GROUP_WIDTH)
        bg = bm[:, g * SSM_STATE:(g + 1) * SSM_STATE].astype(BF16)
        upd = lax.dot_general(bg, wc[:, sl], (((0,), (0,)), ((), ())), preferred_element_type=F32)
        st_s[:, sl] = st_s[:, sl] * dlast_x[:, sl] + upd

    @pl.when(last)
    def _():
        st_ref[0] = st_s[...].T

    _ssd_finish(y_s[...], xs, z_ref, dexp_ref, nwm_ref, mix_ref)


def _mixer_prompt(proj, dt, prm, *, nb, nc):
    rows = CHUNK
    t = nb * nc * rows
    row = lambda b, c: b * nc + c
    full = lambda shape: pl.BlockSpec(shape, lambda b, c: (0,) * len(shape))
    return pl.pallas_call(
        _mixer_prompt_body,
        grid=(nb, nc),
        in_specs=[
            pl.BlockSpec((rows, GATE_WIDTH), lambda b, c: (row(b, c), 0)),
            pl.BlockSpec((rows, GATE_WIDTH), lambda b, c: (row(b, c), 1)),
            pl.BlockSpec((rows, SSM_WIDTH), lambda b, c: (row(b, c), 2)),
            pl.BlockSpec((rows, SSM_CONV_DIM), lambda b, c: (row(b, c), 2)),
            pl.BlockSpec((rows, LANES), lambda b, c: (row(b, c), 0)),
            full((1, GATE_WIDTH)), full((1, GATE_WIDTH)),
            full((GATE_HEADS, rows, rows)), full((rows, GATE_HEADS)),
            full((SSM_CONV, SSM_CONV_DIM)), full((1, SSM_CONV_DIM)),
            full((1, LANES)), full((1, LANES)), full((1, SSM_WIDTH)), full((1, SSM_WIDTH)),
            full((LANES, SSM_WIDTH)),
        ],
        out_specs=[
            pl.BlockSpec((rows, 2 * GATE_WIDTH), lambda b, c: (row(b, c), 0)),
            pl.BlockSpec((1, rows, GATE_WIDTH), lambda b, c: (b, 0, 0)),
            pl.BlockSpec((1, SSM_WIDTH, SSM_STATE), lambda b, c: (b, 0, 0)),
            pl.BlockSpec((1, SSM_CONV - 1, SSM_CONV_DIM), lambda b, c: (b, 0, 0)),
        ],
        out_shape=[
            jax.ShapeDtypeStruct((t, 2 * GATE_WIDTH), BF16),
            jax.ShapeDtypeStruct((nb, rows, GATE_WIDTH), F32),
            jax.ShapeDtypeStruct((nb, SSM_WIDTH, SSM_STATE), F32),
            jax.ShapeDtypeStruct((nb, SSM_CONV - 1, SSM_CONV_DIM), F32),
        ],
        scratch_shapes=[
            pltpu.VMEM((GATE_HEADS, rows, rows), BF16),
            pltpu.VMEM(((SSM_CONV - 1) * rows, 2 * rows), BF16),
            pltpu.VMEM((2 * rows, SSM_CONV_DIM), BF16),
            pltpu.VMEM((SSM_STATE, SSM_WIDTH), F32),
            pltpu.VMEM((rows, SSM_WIDTH), F32),
        ],
        compiler_params=_cparams(("arbitrary", "arbitrary")),
        name="mixer_prompt",
    )(proj, proj, proj, proj, dt, prm["ln_g"], prm["ln_b"], prm["wt_p"], prm["btT_p"], prm["conv_w"], prm["conv_b"],
      prm["dtb"], prm["alog"], prm["dexp"], prm["nwm"], prm["e"])


SEQ_TILE = 8


def _state_decay_body(dt_ref, dtb_ref, alog_ref, o_ref, *, seg):
    nseq = o_ref.shape[0]
    a = -jnp.exp(alog_ref[...])
    tot = jnp.zeros(o_ref.shape, F32)
    for t in range(seg):
        d = jax.nn.softplus(dt_ref[pl.ds(t, nseq, stride=seg), :] + dtb_ref[...])
        tot = tot + d * a
    o_ref[...] = jnp.exp(tot)


def _state_decay(dt, prm, *, seg):
    nseq = dt.shape[0] // seg
    return pl.pallas_call(
        functools.partial(_state_decay_body, seg=seg),
        out_shape=jax.ShapeDtypeStruct((nseq, LANES), F32),
        name="state_decay",
    )(dt, prm["dtb"], prm["alog"])


def _mixer_sample_body(dec_ref, gu_ref, gv_ref, z_ref, xbc_ref, dt_ref, prev_ref, sin_ref,
                       lng_ref, lnb_ref, wt_ref, btT_ref, cw_ref, cb_ref,
                       dtb_ref, alog_ref, dexp_ref, nwm_ref, e_ref,
                       mix_ref, cv_ref, sout_ref, sc_ref,
                       wm_s, y_s, cm_s, bm_s, wc_s, ex_s, *, seg):
    i = pl.program_id(0)
    rows = SEQ_TILE * seg
    mask, same = _seg_masks(rows, seg)

    @pl.when(i == 0)
    def _():
        for h in range(GATE_HEADS):
            wm_s[h] = jnp.where(mask, wt_ref[h], 0.0).astype(BF16)

    cv_ref[...] = _gate_tile(gu_ref, gv_ref, lng_ref, lnb_ref, wm_s, btT_ref, mix_ref)

    x3 = xbc_ref[...].astype(F32).reshape(SEQ_TILE, seg, SSM_CONV_DIM)
    p3 = prev_ref[...].reshape(SEQ_TILE, seg, SSM_CONV_DIM)
    tpos = lax.broadcasted_iota(jnp.int32, x3.shape, 1)
    conv = cb_ref[...] + cw_ref[SSM_CONV - 1:SSM_CONV, :] * x3
    for d in range(1, SSM_CONV):
        shifted = jnp.where(tpos >= d, pltpu.roll(x3, d, axis=1), pltpu.roll(p3, d, axis=1))
        conv = conv + cw_ref[SSM_CONV - 1 - d:SSM_CONV - d, :] * shifted
    act = jax.nn.silu(conv).reshape(rows, SSM_CONV_DIM)
    sc_ref[...] = x3[:, seg - (SSM_CONV - 1):, :]

    maskf = mask.astype(BF16)
    segf = same.astype(BF16)
    xs, bm, cm, dt, cs, cl = _ssd_token_level(act, dt_ref[...], dtb_ref, alog_ref, maskf, segf)
    cs_t = cs.T
    dt_t = dt.T
    e_bf = e_ref[...]
    coef_x = _expand_heads(dt * jnp.exp(cl - cs), e_bf)
    ecs_x = _expand_heads(jnp.exp(cs), e_bf)
    lane = lax.broadcasted_iota(jnp.int32, (rows, LANES), 1)
    xs_b = xs.astype(BF16)
    for g in range(SSM_GROUPS):
        cg = cm[:, g * SSM_STATE:(g + 1) * SSM_STATE]
        bg = bm[:, g * SSM_STATE:(g + 1) * SSM_STATE]
        cb = lax.dot_general(cg.astype(BF16), bg.astype(BF16), (((1,), (1,)), ((), ())), preferred_element_type=F32)
        for p in range(HEADS_PER_GROUP // 2):
            h0 = g * HEADS_PER_GROUP + 2 * p
            sl = slice((h0 // 2) * LANES, (h0 // 2 + 1) * LANES)
            ys = [_dot(_ssd_diag_pair(cb, cs, cs_t, dt_t, mask, h).astype(BF16), xs_b[:, sl]) for h in (h0, h0 + 1)]
            y_s[:, sl] = jnp.where(lane < SSM_HEAD_DIM, ys[0], ys[1])
    cm_s[...] = cm
    bm_s[...] = bm
    wc_s[...] = xs * coef_x
    ex_s[...] = ecs_x

    rowid = lax.broadcasted_iota(jnp.int32, (rows, SSM_STATE), 0)

    def seq_step(s, carry):
        r8 = pl.ds(pl.multiple_of(s * seg, seg), seg)
        for g in range(SSM_GROUPS):
            gsl = slice(g * GROUP_WIDTH, (g + 1) * GROUP_WIDTH)
            nsl = slice(g * SSM_STATE, (g + 1) * SSM_STATE)
            st = sin_ref[s, gsl, :]
            c8 = cm_s[r8, nsl].astype(BF16)
            yo = lax.dot_general(c8, st.astype(BF16), (((1,), (1,)), ((), ())), preferred_element_type=F32)
            y_s[r8, gsl] = y_s[r8, gsl] + yo * ex_s[r8, gsl]
            bmask = jnp.where(rowid // seg == s, bm_s[:, nsl], 0.0).astype(BF16)
            upd = lax.dot_general(wc_s[:, gsl].astype(BF16), bmask, (((0,), (0,)), ((), ())), preferred_element_type=F32)
            for r in range(HEADS_PER_GROUP):
                d = dec_ref[(i * SEQ_TILE + s) * SSM_HEADS + g * HEADS_PER_GROUP + r]
                hsl = slice(r * SSM_HEAD_DIM, (r + 1) * SSM_HEAD_DIM)
                osl = slice(g * GROUP_WIDTH + r * SSM_HEAD_DIM, g * GROUP_WIDTH + (r + 1) * SSM_HEAD_DIM)
                sout_ref[s, osl, :] = st[hsl, :] * d + upd[hsl, :]
        return carry

    lax.fori_loop(0, SEQ_TILE, seq_step, 0)

    _ssd_finish(y_s[...], xs, z_ref, dexp_ref, nwm_ref, mix_ref)


def _mixer_sample(dec, proj, dt, prev, state, prm, *, seg):
    rows = SEQ_TILE * seg
    t = proj.shape[0]
    nseq = t // seg
    assert t % rows == 0
    full = lambda shape: pl.BlockSpec(shape, lambda i: (0,) * len(shape))
    return pl.pallas_call(
        functools.partial(_mixer_sample_body, seg=seg),
        grid=(t // rows,),
        in_specs=[
            pl.BlockSpec(memory_space=pltpu.SMEM),
            pl.BlockSpec((rows, GATE_WIDTH), lambda i: (i, 0)),
            pl.BlockSpec((rows, GATE_WIDTH), lambda i: (i, 1)),
            pl.BlockSpec((rows, SSM_WIDTH), lambda i: (i, 2)),
            pl.BlockSpec((rows, SSM_CONV_DIM), lambda i: (i, 2)),
            pl.BlockSpec((rows, LANES), lambda i: (i, 0)),
            pl.BlockSpec((rows, SSM_CONV_DIM), lambda i: (i, 0)),
            pl.BlockSpec((SEQ_TILE, SSM_WIDTH, SSM_STATE), lambda i: (i, 0, 0)),
            full((1, GATE_WIDTH)), full((1, GATE_WIDTH)),
            full((GATE_HEADS, rows, rows)), full((rows, GATE_HEADS)),
            full((SSM_CONV, SSM_CONV_DIM)), full((1, SSM_CONV_DIM)),
            full((1, LANES)), full((1, LANES)), full((1, SSM_WIDTH)), full((1, SSM_WIDTH)),
            full((LANES, SSM_WIDTH)),
        ],
        out_specs=[
            pl.BlockSpec((rows, 2 * GATE_WIDTH), lambda i: (i, 0)),
            pl.BlockSpec((rows, GATE_WIDTH), lambda i: (i, 0)),
            pl.BlockSpec((SEQ_TILE, SSM_WIDTH, SSM_STATE), lambda i: (i, 0, 0)),
            pl.BlockSpec((SEQ_TILE, SSM_CONV - 1, SSM_CONV_DIM), lambda i: (i, 0, 0)),
        ],
        out_shape=[
            jax.ShapeDtypeStruct((t, 2 * GATE_WIDTH), BF16),
            jax.ShapeDtypeStruct((t, GATE_WIDTH), F32),
            jax.ShapeDtypeStruct((nseq, SSM_WIDTH, SSM_STATE), F32),
            jax.ShapeDtypeStruct((nseq, SSM_CONV - 1, SSM_CONV_DIM), F32),
        ],
        scratch_shapes=[
            pltpu.VMEM((GATE_HEADS, rows, rows), BF16),
            pltpu.VMEM((rows, SSM_WIDTH), F32),
            pltpu.VMEM((rows, GROUP_WIDTH), F32),
            pltpu.VMEM((rows, GROUP_WIDTH), F32),
            pltpu.VMEM((rows, SSM_WIDTH), F32),
            pltpu.VMEM((rows, SSM_WIDTH), F32),
        ],
        compiler_params=_cparams(("arbitrary",)),
        name="mixer_sample",
    )(dec, proj, proj, proj, proj, dt, prev, state, prm["ln_g"], prm["ln_b"], prm["wt_s"], prm["btT_s"],
      prm["conv_w"], prm["conv_b"], prm["dtb"], prm["alog"], prm["dexp"], prm["nwm"], prm["e"])


ROW_SUB = 128


def _row_subs(tm, sub=ROW_SUB):
    sub = min(sub, tm)
    assert tm % sub == 0
    return [slice(r * sub, (r + 1) * sub) for r in range(tm // sub)]


def _out_proj_body(m_ref, x_ref, w_ref, npost_ref, npre_ref, x1_ref, h2_ref):
    for rs in _row_subs(m_ref.shape[0]):
        mix = _dot(m_ref[rs, :], w_ref[...])
        x1 = x_ref[rs, :] + _rms(mix, npost_ref[...])
        x1_ref[rs, :] = x1
        h2_ref[rs, :] = _rms(x1, npre_ref[...]).astype(h2_ref.dtype)


def _out_proj(mixin, x2d, w_out, npost, npre, *, tm):
    t, k = mixin.shape
    assert t % tm == 0
    return pl.pallas_call(
        _out_proj_body,
        grid=(t // tm,),
        in_specs=[
            pl.BlockSpec((tm, k), lambda i: (i, 0)),
            pl.BlockSpec((tm, D_MODEL), lambda i: (i, 0)),
            pl.BlockSpec((k, D_MODEL), lambda i: (0, 0), pipeline_mode=pl.Buffered(1)),
            pl.BlockSpec((1, D_MODEL), lambda i: (0, 0)),
            pl.BlockSpec((1, D_MODEL), lambda i: (0, 0)),
        ],
        out_specs=[
            pl.BlockSpec((tm, D_MODEL), lambda i: (i, 0)),
            pl.BlockSpec((tm, D_MODEL), lambda i: (i, 0)),
        ],
        out_shape=[
            jax.ShapeDtypeStruct((t, D_MODEL), F32),
            jax.ShapeDtypeStruct((t, D_MODEL), BF16),
        ],
        compiler_params=_cparams(("arbitrary",)),
        name="out_proj",
    )(mixin, x2d, w_out, npost, npre)


def _ffn_conv_taps(cur, shifted, cw_ref, cb_ref):
    out = cb_ref[...] + cw_ref[FFN_CONV - 1:FFN_CONV, :] * cur
    for d in range(1, FFN_CONV):
        out = out + cw_ref[FFN_CONV - 1 - d:FFN_CONV - d, :] * shifted[d - 1]
    return out


def _cast_weights_once(i, pairs):
    @pl.when(i == 0)
    def _():
        for w_ref, wb_s in pairs:
            wb_s[...] = w_ref[...].astype(BF16)


def _ffn_up_prompt_body(h_ref, wg_ref, wu_ref, cwg_ref, cwu_ref, cbg_ref, cbu_ref, a_ref, st_ref,
                        wgb_s, wub_s, tg_s, tu_s, *, tiles_per_seq):
    i = pl.program_id(1)
    tm = h_ref.shape[0]
    _cast_weights_once(i, ((wg_ref, wgb_s), (wu_ref, wub_s)))

    @pl.when((i % tiles_per_seq) == 0)
    def _():
        for t_s in (tg_s, tu_s):
            t_s[...] = jnp.zeros(t_s.shape, F32)

    tails = [tg_s[...], tu_s[...]]
    row8 = lax.broadcasted_iota(jnp.int32, tg_s.shape, 0)
    for rs in _row_subs(tm, 128):
        h = h_ref[rs, :]
        convs = []
        for k, (wb_s, cw_ref, cb_ref) in enumerate(((wgb_s, cwg_ref, cbg_ref), (wub_s, cwu_ref, cbu_ref))):
            x = _dot(h, wb_s[...])
            shifted = []
            for d in range(1, FFN_CONV):
                r = pltpu.roll(x, d, axis=0)
                head = jnp.where(row8 >= d, r[0:SUBLANES], pltpu.roll(tails[k], d, axis=0))
                shifted.append(jnp.concatenate([head, r[SUBLANES:]], axis=0))
            convs.append(_ffn_conv_taps(x, shifted, cw_ref, cb_ref))
            tails[k] = x[x.shape[0] - SUBLANES:]
        a_ref[rs, :] = (jax.nn.gelu(convs[0], approximate=True) * convs[1]).astype(a_ref.dtype)
    for half, t_s in enumerate((tg_s, tu_s)):
        t_s[...] = tails[half]
        st_ref[0, :, half, :] = tails[half][SUBLANES - (FFN_CONV - 1):]


def _ffn_up_sample_body(h_ref, wg_ref, wu_ref, cwg_ref, cwu_ref, cbg_ref, cbu_ref, pg_ref, pu_ref, a_ref, st_ref,
                        wgb_s, wub_s, *, seg):
    i = pl.program_id(1)
    tm = h_ref.shape[0]
    tn = wg_ref.shape[1]
    _cast_weights_once(i, ((wg_ref, wgb_s), (wu_ref, wub_s)))
    for rs in _row_subs(tm):
        n = rs.stop - rs.start
        nseq = n // seg
        sq = slice(rs.start // seg, rs.stop // seg)
        h = h_ref[rs, :]
        tpos = lax.broadcasted_iota(jnp.int32, (nseq, seg, tn), 1)
        convs = []
        for half, (wb_s, p_ref, cw_ref, cb_ref) in enumerate(
                ((wgb_s, pg_ref, cwg_ref, cbg_ref), (wub_s, pu_ref, cwu_ref, cbu_ref))):
            x3 = _dot(h, wb_s[...]).reshape(nseq, seg, tn)
            p3 = p_ref[rs, :].reshape(nseq, seg, tn)
            shifted = [jnp.where(tpos >= d, pltpu.roll(x3, d, axis=1), pltpu.roll(p3, d, axis=1))
                       for d in range(1, FFN_CONV)]
            convs.append(_ffn_conv_taps(x3, shifted, cw_ref, cb_ref))
            st_ref[sq, :, half, :] = x3[:, seg - (FFN_CONV - 1):, :]
        act = jax.nn.gelu(convs[0], approximate=True) * convs[1]
        a_ref[rs, :] = act.reshape(n, tn).astype(a_ref.dtype)


def _ffn_up(h2, w_up, cw, cb, prev, *, nseq, seg, tm, tn):
    t = h2.shape[0]
    nj = D_FF // tn
    assert t % tm == 0 and D_FF % tn == 0 and t == nseq * seg
    common_in = [
        pl.BlockSpec((tm, D_MODEL), lambda j, i: (i, 0)),
        pl.BlockSpec((D_MODEL, tn), lambda j, i: (0, j)),
        pl.BlockSpec((D_MODEL, tn), lambda j, i: (0, j + nj)),
        pl.BlockSpec((FFN_CONV, tn), lambda j, i: (0, j)),
        pl.BlockSpec((FFN_CONV, tn), lambda j, i: (0, j + nj)),
        pl.BlockSpec((1, tn), lambda j, i: (0, j)),
        pl.BlockSpec((1, tn), lambda j, i: (0, j + nj)),
    ]
    a_spec = pl.BlockSpec((tm, tn), lambda j, i: (i, j))
    out_shape = [
        jax.ShapeDtypeStruct((t, D_FF), BF16),
        jax.ShapeDtypeStruct((nseq, FFN_CONV - 1, 2, D_FF), F32),
    ]
    wb_scratch = [pltpu.VMEM((D_MODEL, tn), BF16), pltpu.VMEM((D_MODEL, tn), BF16)]
    if prev is None:
        assert seg % tm == 0
        tps = seg // tm
        return pl.pallas_call(
            functools.partial(_ffn_up_prompt_body, tiles_per_seq=tps),
            grid=(nj, t // tm),
            in_specs=common_in,
            out_specs=[a_spec, pl.BlockSpec((1, FFN_CONV - 1, 2, tn), lambda j, i: (i // tps, 0, 0, j))],
            out_shape=out_shape,
            scratch_shapes=wb_scratch + [pltpu.VMEM((SUBLANES, tn), F32), pltpu.VMEM((SUBLANES, tn), F32)],
            compiler_params=_cparams(("arbitrary", "arbitrary")),
            name="ffn_up_prompt",
        )(h2, w_up, w_up, cw, cw, cb, cb)
    assert tm % seg == 0 and seg == SUBLANES
    return pl.pallas_call(
        functools.partial(_ffn_up_sample_body, seg=seg),
        grid=(nj, t // tm),
        in_specs=common_in + [
            pl.BlockSpec((tm, tn), lambda j, i: (i, j)),
            pl.BlockSpec((tm, tn), lambda j, i: (i, j + nj)),
        ],
        out_specs=[a_spec, pl.BlockSpec((tm // seg, FFN_CONV - 1, 2, tn), lambda j, i: (i, 0, 0, j))],
        out_shape=out_shape,
        scratch_shapes=wb_scratch,
        compiler_params=_cparams(("arbitrary", "arbitrary")),
        name="ffn_up_sample",
    )(h2, w_up, w_up, cw, cw, cb, cb, prev, prev)


def _ffn_down_body(a_ref, x_ref, w_ref, nw_ref, y_ref):
    for rs in _row_subs(a_ref.shape[0], 128):
        f = _dot(a_ref[rs, :], w_ref[...])
        y_ref[rs, :] = x_ref[rs, :] + _rms(f, nw_ref[...])


def _ffn_down(act, x1, w_down, nw, *, tm):
    t, k = act.shape
    assert t % tm == 0
    return pl.pallas_call(
        _ffn_down_body,
        grid=(t // tm,),
        in_specs=[
            pl.BlockSpec((tm, k), lambda i: (i, 0)),
            pl.BlockSpec((tm, D_MODEL), lambda i: (i, 0)),
            pl.BlockSpec((k, D_MODEL), lambda i: (0, 0), pipeline_mode=pl.Buffered(1)),
            pl.BlockSpec((1, D_MODEL), lambda i: (0, 0)),
        ],
        out_specs=pl.BlockSpec((tm, D_MODEL), lambda i: (i, 0)),
        out_shape=jax.ShapeDtypeStruct((t, D_MODEL), F32),
        compiler_params=_cparams(("arbitrary",)),
        name="ffn_down",
    )(act, x1, w_down, nw)


def _head_expander():
    e = np.zeros((LANES, SSM_WIDTH), np.float32)
    for h in range(SSM_HEADS):
        e[h, h * SSM_HEAD_DIM:(h + 1) * SSM_HEAD_DIM] = 1.0
    return jnp.asarray(e, BF16)


def _pad_lanes(v):
    return jnp.pad(v, (0, LANES - v.shape[0]))[None, :]


def _prep_params(norm_mix_pre, w_in, gate_ln_g, gate_ln_b, gate_w_s, gate_b_s, ssm_conv_w, ssm_conv_b, ssm_dt_bias,
                 ssm_a_log, ssm_d, ssm_norm_w, w_out, norm_mix_post, norm_ffn_pre, ffn_w_up, ffn_conv_w, ffn_conv_b,
                 ffn_w_down, norm_ffn_post, seg_sample):
    rep = CHUNK // seg_sample
    ws_small = gate_w_s[:, :seg_sample, :seg_sample]
    return dict(
        nw_pre=norm_mix_pre[None, :],
        w_in=w_in,
        w_dt=jnp.pad(w_in[:, PROJ_MAIN:], ((0, 0), (0, LANES - SSM_HEADS))).astype(BF16),
        ln_g=gate_ln_g[None, :], ln_b=gate_ln_b[None, :],
        wt_p=gate_w_s, btT_p=gate_b_s.T,
        wt_s=jnp.tile(ws_small, (1, SEQ_TILE, SEQ_TILE)), btT_s=jnp.tile(gate_b_s[:, :seg_sample], (1, SEQ_TILE)).T,
        conv_w=ssm_conv_w, conv_b=ssm_conv_b[None, :],
        dtb=_pad_lanes(ssm_dt_bias), alog=_pad_lanes(ssm_a_log),
        dexp=jnp.repeat(ssm_d, SSM_HEAD_DIM)[None, :], nwm=ssm_norm_w[None, :],
        e=_head_expander(),
        w_out=w_out.astype(BF16), n_post=norm_mix_post[None, :], n_pre2=norm_ffn_pre[None, :],
        w_up=ffn_w_up, fcw=ffn_conv_w, fcb=ffn_conv_b[None, :],
        w_down=ffn_w_down.astype(BF16), n_post2=norm_ffn_post[None, :],
    )


def _pad_history(state, seg):
    nseq, k1, c = state.shape
    return jnp.pad(state, ((0, 0), (seg - k1, 0), (0, 0))).reshape(nseq * seg, c)


def _row_tile(t, cap):
    tm = cap
    while t % tm:
        tm //= 2
    assert tm >= 64
    return tm


TM_STREAM = 1024
TM_OUT_PROJ = 512
TM_FFN_DOWN = 256
TM_PRE_NORM = 512


def _layer_prompt(x, prm):
    nb, seq, _ = x.shape
    assert seq % CHUNK == 0
    nc = seq // CHUNK
    x2d = x.reshape(nb * seq, D_MODEL)
    tm = _row_tile(seq, TM_STREAM)
    h, dt = _pre_norm(x2d, prm["nw_pre"], prm["w_dt"], tm=_row_tile(seq, TM_PRE_NORM))
    proj = _in_proj(h, prm["w_in"], tm=tm, tn=1024, sub=min(tm, 256))
    mixin, cv, st, sc = _mixer_prompt(proj, dt, prm, nb=nb, nc=nc)
    x1, h2 = _out_proj(mixin, x2d, prm["w_out"], prm["n_post"], prm["n_pre2"], tm=_row_tile(seq, TM_OUT_PROJ))
    act, fst = _ffn_up(h2, prm["w_up"], prm["fcw"], prm["fcb"], None, nseq=nb, seg=seq, tm=tm, tn=512)
    y = _ffn_down(act, x1, prm["w_down"], prm["n_post2"], tm=_row_tile(seq, TM_FFN_DOWN))
    return (y.reshape(nb, seq, D_MODEL),
            st.reshape(nb, SSM_HEADS, SSM_HEAD_DIM, SSM_STATE),
            sc,
            fst.reshape(nb, FFN_CONV - 1, 2 * D_FF),
            cv.reshape(nb, CHUNK, GATE_HEADS, GATE_HEAD_DIM))


def _layer_sample(x, state_ssm, state_sconv, state_fconv, prm):
    nb, seg, _ = x.shape
    assert seg == SUBLANES and nb % SEQ_TILE == 0
    t = nb * seg
    x2d = x.reshape(t, D_MODEL)
    tm = _row_tile(t, TM_STREAM)
    h, dt = _pre_norm(x2d, prm["nw_pre"], prm["w_dt"], tm=_row_tile(t, TM_PRE_NORM))
    proj = _in_proj(h, prm["w_in"], tm=tm, tn=1024, sub=min(tm, 256))
    dec = _state_decay(dt, prm, seg=seg)[:, :SSM_HEADS].reshape(nb * SSM_HEADS)
    mixin, cv, st, sc = _mixer_sample(dec, proj, dt, _pad_history(state_sconv, seg),
                                      state_ssm.reshape(nb, SSM_WIDTH, SSM_STATE), prm, seg=seg)
    x1, h2 = _out_proj(mixin, x2d, prm["w_out"], prm["n_post"], prm["n_pre2"], tm=_row_tile(t, TM_OUT_PROJ))
    act, fst = _ffn_up(h2, prm["w_up"], prm["fcw"], prm["fcb"], _pad_history(state_fconv, seg),
                       nseq=nb, seg=seg, tm=tm, tn=512)
    y = _ffn_down(act, x1, prm["w_down"], prm["n_post2"], tm=_row_tile(t, TM_FFN_DOWN))
    return (y.reshape(nb, seg, D_MODEL),
            st.reshape(nb, SSM_HEADS, SSM_HEAD_DIM, SSM_STATE),
            sc,
            fst.reshape(nb, FFN_CONV - 1, 2 * D_FF),
            cv.reshape(nb, seg, GATE_HEADS, GATE_HEAD_DIM))


def kernel(x_prompt, x_sample, state_ssm, state_ssm_conv, state_ffn_conv, norm_mix_pre, w_in, gate_ln_g, gate_ln_b,
           gate_w_s, gate_b_s, ssm_conv_w, ssm_conv_b, ssm_dt_bias, ssm_a_log, ssm_d, ssm_norm_w, w_out,
           norm_mix_post, norm_ffn_pre, ffn_w_up, ffn_conv_w, ffn_conv_b, ffn_w_down, norm_ffn_post):
    depth = w_in.shape[0]
    yp, ys = x_prompt, x_sample
    outs_p, outs_s = [], []
    for l in range(depth):
        prm = _prep_params(norm_mix_pre[l], w_in[l], gate_ln_g[l], gate_ln_b[l], gate_w_s[l], gate_b_s[l],
                           ssm_conv_w[l], ssm_conv_b[l], ssm_dt_bias[l], ssm_a_log[l], ssm_d[l], ssm_norm_w[l],
                           w_out[l], norm_mix_post[l], norm_ffn_pre[l], ffn_w_up[l], ffn_conv_w[l], ffn_conv_b[l],
                           ffn_w_down[l], norm_ffn_post[l], x_sample.shape[1])
        yp, *rest_p = _layer_prompt(yp, prm)
        ys, *rest_s = _layer_sample(ys, state_ssm[l], state_ssm_conv[l], state_ffn_conv[l], prm)
        outs_p.append(rest_p)
        outs_s.append(rest_s)
    stack = lambda outs, k: jnp.stack([o[k] for o in outs])
    return (yp, ys,
            stack(outs_p, 0), stack(outs_p, 1), stack(outs_p, 2), stack(outs_p, 3),
            stack(outs_s, 0), stack(outs_s, 1), stack(outs_s, 2), stack(outs_s, 3))
```

```python
import functools

import jax
import jax.numpy as jnp
import numpy as np
from jax import lax
from jax.experimental import pallas as pl
from jax.experimental.pallas import tpu as pltpu

F32 = jnp.float32
BF16 = jnp.bfloat16

D_MODEL = 2048
GATE_WIDTH = 2048
GATE_HEADS = 16
GATE_HEAD_DIM = 128
CHUNK = 128
SSM_WIDTH = 2048
SSM_HEAD_DIM = 64
SSM_HEADS = 32
SSM_GROUPS = 4
SSM_STATE = 128
SSM_CONV = 4
SSM_CONV_DIM = SSM_WIDTH + 2 * SSM_GROUPS * SSM_STATE
PROJ_MAIN = 2 * GATE_WIDTH + SSM_WIDTH + SSM_CONV_DIM
D_FF = 5632
FFN_CONV = 3
EPS = 1e-6
HEADS_PER_GROUP = SSM_HEADS // SSM_GROUPS
GROUP_WIDTH = SSM_WIDTH // SSM_GROUPS

LANES = 128
SUBLANES = 8
VMEM_LIMIT = 56 * 1024 * 1024


def _cparams(sem):
    return pltpu.CompilerParams(dimension_semantics=sem, vmem_limit_bytes=VMEM_LIMIT)


def _rms(x, w):
    return x * lax.rsqrt(jnp.mean(x * x, axis=-1, keepdims=True) + EPS) * w


def _gelu_erf(x):
    return 0.5 * x * (1.0 + lax.erf(x * np.float32(0.7071067811865476)))


def _split_bf16(x, n):
    parts = []
    r = x
    for k in range(n):
        p = r.astype(BF16)
        parts.append(p)
        if k + 1 < n:
            r = r - p.astype(F32)
    return parts


def _dot(a, b):
    return jnp.dot(a, b, preferred_element_type=F32)


def _dot_nt(a, b):
    return lax.dot_general(a, b, (((1,), (1,)), ((), ())), preferred_element_type=F32)


def _mm_split(m_bf, x, n):
    acc = None
    for p in _split_bf16(x, n):
        t = _dot(m_bf, p)
        acc = t if acc is None else acc + t
    return acc


def _expand_heads(x, e_bf):
    acc = None
    for p in _split_bf16(x, 2):
        t = _dot(p, e_bf)
        acc = t if acc is None else acc + t
    return acc


def _history_block(p_ref, sq, seg):
    k1, c = p_ref.shape[1], p_ref.shape[2]
    s = sq.stop - sq.start
    t = lax.broadcasted_iota(jnp.int32, (s, seg, c), 1)
    out = jnp.zeros((s, seg, c), F32)
    for k in range(k1):
        out = jnp.where(t == seg - k1 + k, jnp.broadcast_to(p_ref[sq, k:k + 1, :], (s, seg, c)), out)
    return out


def _seg_masks(rows, seg):
    r = lax.broadcasted_iota(jnp.int32, (rows, rows), 0)
    c = lax.broadcasted_iota(jnp.int32, (rows, rows), 1)
    same = (r // seg) == (c // seg)
    return same & (c <= r), same


def _pre_norm_body(x_ref, nw_ref, wdt_ref, h_ref, dt_ref):
    h = _rms(x_ref[...], nw_ref[...]).astype(BF16)
    h_ref[...] = h
    dt_ref[...] = _dot_nt(h, wdt_ref[...])


def _pre_norm(x2d, nw, w_dt, *, tm):
    t = x2d.shape[0]
    assert t % tm == 0
    return pl.pallas_call(
        _pre_norm_body,
        grid=(t // tm,),
        in_specs=[
            pl.BlockSpec((tm, D_MODEL), lambda i: (i, 0)),
            pl.BlockSpec((1, D_MODEL), lambda i: (0, 0)),
            pl.BlockSpec((LANES, D_MODEL), lambda i: (0, 0)),
        ],
        out_specs=[
            pl.BlockSpec((tm, D_MODEL), lambda i: (i, 0)),
            pl.BlockSpec((tm, LANES), lambda i: (i, 0)),
        ],
        out_shape=[
            jax.ShapeDtypeStruct((t, D_MODEL), BF16),
            jax.ShapeDtypeStruct((t, LANES), F32),
        ],
        compiler_params=_cparams(("arbitrary",)),
        name="pre_norm",
    )(x2d, nw, w_dt)


def _in_proj_body(h_ref, w_ref, o_ref, wb_s, *, n_gelu, sub):
    j = pl.program_id(0)
    i = pl.program_id(1)
    tm = h_ref.shape[0]

    @pl.when(i == 0)
    def _():
        wb_s[...] = w_ref[...].astype(BF16)

    def run(epilogue):
        for rs in _row_subs(tm, sub):
            o_ref[rs, :] = epilogue(_dot_nt(h_ref[rs, :], wb_s[...])).astype(o_ref.dtype)

    @pl.when(j < n_gelu)
    def _():
        run(_gelu_erf)

    @pl.when(j >= n_gelu)
    def _():
        run(lambda a: a)


def _in_proj(h, w_in_t, *, tm, tn, sub):
    t = h.shape[0]
    assert t % tm == 0 and PROJ_MAIN % tn == 0 and (2 * GATE_WIDTH) % tn == 0 and tm % sub == 0
    return pl.pallas_call(
        functools.partial(_in_proj_body, n_gelu=2 * GATE_WIDTH // tn, sub=sub),
        grid=(PROJ_MAIN // tn, t // tm),
        in_specs=[
            pl.BlockSpec((tm, D_MODEL), lambda j, i: (i, 0)),
            pl.BlockSpec((tn, D_MODEL), lambda j, i: (j, 0)),
        ],
        out_specs=pl.BlockSpec((tm, tn), lambda j, i: (i, j)),
        out_shape=jax.ShapeDtypeStruct((t, PROJ_MAIN), BF16),
        scratch_shapes=[pltpu.VMEM((tn, D_MODEL), BF16)],
        compiler_params=_cparams(("arbitrary", "arbitrary")),
        name="in_proj",
    )(h, w_in_t)


def _gate_tile(gu_ref, gv_ref, lng_ref, lnb_ref, wm_s, btT_ref, mix_ref):
    g = gv_ref[...].astype(F32)
    mu = jnp.mean(g, axis=-1, keepdims=True)
    xc = g - mu
    v = xc * lax.rsqrt(jnp.mean(xc * xc, axis=-1, keepdims=True) + EPS) * lng_ref[...] + lnb_ref[...]
    vb = v.astype(BF16)
    rows = v.shape[0]
    for h in range(GATE_HEADS):
        sl = slice(h * GATE_HEAD_DIM, (h + 1) * GATE_HEAD_DIM)
        s = _dot(wm_s[h], vb[:, sl]) + jnp.broadcast_to(btT_ref[:, h:h + 1], (rows, GATE_HEAD_DIM))
        mix_ref[:, sl] = (gu_ref[:, sl].astype(F32) * s).astype(mix_ref.dtype)
    return v


def _ssd_token_level(act, dt_raw, dtb_ref, alog_ref, maskf, segf):
    xs = act[:, :SSM_WIDTH]
    bm = act[:, SSM_WIDTH:SSM_WIDTH + GROUP_WIDTH]
    cm = act[:, SSM_WIDTH + GROUP_WIDTH:]
    dt = jax.nn.softplus(dt_raw + dtb_ref[...])
    a = -jnp.exp(alog_ref[...])
    da = dt * a
    cs = _mm_split(maskf, da, 3)
    cl = _mm_split(segf, da, 3)
    return xs, bm, cm, dt, cs, cl


def _ssd_diag_pair(cb, cs, cs_t, dt_t, mask, h):
    seg = cs[:, h:h + 1] - cs_t[h:h + 1, :]
    decay = jnp.exp(jnp.where(mask, seg, -jnp.inf))
    return cb * decay * dt_t[h:h + 1, :]


def _ssd_finish(y, xs, z_ref, dexp_ref, nwm_ref, mix_ref):
    y = y + dexp_ref[...] * xs
    y = y * jax.nn.silu(z_ref[...].astype(F32))
    for g in range(SSM_GROUPS):
        sl = slice(g * GROUP_WIDTH, (g + 1) * GROUP_WIDTH)
        yg = y[:, sl]
        yg = yg * lax.rsqrt(jnp.mean(yg * yg, axis=-1, keepdims=True) + EPS) * nwm_ref[:, sl]
        mix_ref[:, GATE_WIDTH + g * GROUP_WIDTH:GATE_WIDTH + (g + 1) * GROUP_WIDTH] = yg.astype(mix_ref.dtype)


def _mixer_prompt_body(gu_ref, gv_ref, z_ref, xbc_ref, dt_ref, lng_ref, lnb_ref, wt_ref, btT_ref, cw_ref, cb_ref,
                       dtb_ref, alog_ref, dexp_ref, nwm_ref, e_ref,
                       mix_ref, cv_ref, st_ref, sc_ref,
                       wm_s, shift_s, xx_s, st_s, y_s):
    b = pl.program_id(0)
    c = pl.program_id(1)
    last = c == pl.num_programs(1) - 1
    rows = CHUNK
    keep = 2 * SUBLANES
    mask, same = _seg_masks(rows, rows)

    @pl.when((b == 0) & (c == 0))
    def _():
        for h in range(GATE_HEADS):
            wm_s[h] = jnp.where(mask, wt_ref[h], 0.0).astype(BF16)
        r = lax.broadcasted_iota(jnp.int32, shift_s.shape, 0)
        col = lax.broadcasted_iota(jnp.int32, shift_s.shape, 1)
        shift_s[...] = jnp.where(col == rows + r % rows - (r // rows + 1), 1.0, 0.0).astype(BF16)
        xx_s[0:rows, :] = jnp.zeros((rows, SSM_CONV_DIM), BF16)

    @pl.when(c == 0)
    def _():
        xx_s[rows - keep:rows, :] = jnp.zeros((keep, SSM_CONV_DIM), BF16)
        st_s[...] = jnp.zeros(st_s.shape, F32)

    v = _gate_tile(gu_ref, gv_ref, lng_ref, lnb_ref, wm_s, btT_ref, mix_ref)

    @pl.when(last)
    def _():
        cv_ref[0] = v

    xx_s[rows:2 * rows, :] = xbc_ref[...]
    sh = _dot(shift_s[...], xx_s[...])
    conv = cb_ref[...] + cw_ref[SSM_CONV - 1:SSM_CONV, :] * xbc_ref[...].astype(F32)
    for d in range(1, SSM_CONV):
        conv = conv + cw_ref[SSM_CONV - 1 - d:SSM_CONV - d, :] * sh[(d - 1) * rows:d * rows, :]
    act = jax.nn.silu(conv)

    @pl.when(last)
    def _():
        sc_ref[0] = xbc_ref[rows - (SSM_CONV - 1):rows, :].astype(F32)

    xx_s[rows - keep:rows, :] = xx_s[2 * rows - keep:2 * rows, :]

    maskf = mask.astype(BF16)
    segf = same.astype(BF16)
    xs, bm, cm, dt, cs, cl = _ssd_token_level(act, dt_ref[...], dtb_ref, alog_ref, maskf, segf)
    cs_t = cs.T
    dt_t = dt.T
    ecs = jnp.exp(cs)
    e_bf = e_ref[...]
    coef_x = _expand_heads(dt * jnp.exp(cl - cs), e_bf)
    dlast_x = _expand_heads(jnp.exp(cl[0:SUBLANES, :]), e_bf)[0:1, :]
    lane = lax.broadcasted_iota(jnp.int32, (rows, LANES), 1)
    xs_b = xs.astype(BF16)
    for g in range(SSM_GROUPS):
        cg = cm[:, g * SSM_STATE:(g + 1) * SSM_STATE]
        bg = bm[:, g * SSM_STATE:(g + 1) * SSM_STATE]
        cb = lax.dot_general(cg.astype(BF16), bg.astype(BF16), (((1,), (1,)), ((), ())), preferred_element_type=F32)
        for p in range(HEADS_PER_GROUP // 2):
            h0 = g * HEADS_PER_GROUP + 2 * p
            sl = slice((h0 // 2) * LANES, (h0 // 2 + 1) * LANES)
            rhs = jnp.concatenate([xs_b[:, sl], st_s[:, sl].astype(BF16)], axis=0)
            ys = []
            for h in (h0, h0 + 1):
                m_h = _ssd_diag_pair(cb, cs, cs_t, dt_t, mask, h)
                c_h = cg * jnp.broadcast_to(ecs[:, h:h + 1], (rows, SSM_STATE))
                lhs = jnp.concatenate([m_h.astype(BF16), c_h.astype(BF16)], axis=1)
                ys.append(_dot(lhs, rhs))
            y_s[:, sl] = jnp.where(lane < SSM_HEAD_DIM, ys[0], ys[1])
    wc = (xs * coef_x).astype(BF16)
    for g in range(SSM_GROUPS):
        sl = slice(g * GROUP_WIDTH, (g + 1) * GROUP_WIDTH)
        bg = bm[:, g * SSM_STATE:(g + 1) * SSM_STATE].astype(BF16)
        upd = lax.dot_general(bg, wc[:, sl], (((0,), (0,)), ((), ())), preferred_element_type=F32)
        st_s[:, sl] = st_s[:, sl] * dlast_x[:, sl] + upd

    @pl.when(last)
    def _():
        st_ref[0] = st_s[...].T

    _ssd_finish(y_s[...], xs, z_ref, dexp_ref, nwm_ref, mix_ref)


def _mixer_prompt(proj, dt, prm, *, nb, nc):
    rows = CHUNK
    t = nb * nc * rows
    row = lambda b, c: b * nc + c
    full = lambda shape: pl.BlockSpec(shape, lambda b, c: (0,) * len(shape))
    return pl.pallas_call(
        _mixer_prompt_body,
        grid=(nb, nc),
        in_specs=[
            pl.BlockSpec((rows, GATE_WIDTH), lambda b, c: (row(b, c), 0)),
            pl.BlockSpec((rows, GATE_WIDTH), lambda b, c: (row(b, c), 1)),
            pl.BlockSpec((rows, SSM_WIDTH), lambda b, c: (row(b, c), 2)),
            pl.BlockSpec((rows, SSM_CONV_DIM), lambda b, c: (row(b, c), 2)),
            pl.BlockSpec((rows, LANES), lambda b, c: (row(b, c), 0)),
            full((1, GATE_WIDTH)), full((1, GATE_WIDTH)),
            full((GATE_HEADS, rows, rows)), full((rows, GATE_HEADS)),
            full((SSM_CONV, SSM_CONV_DIM)), full((1, SSM_CONV_DIM)),
            full((1, LANES)), full((1, LANES)), full((1, SSM_WIDTH)), full((1, SSM_WIDTH)),
            full((LANES, SSM_WIDTH)),
        ],
        out_specs=[
            pl.BlockSpec((rows, 2 * GATE_WIDTH), lambda b, c: (row(b, c), 0)),
            pl.BlockSpec((1, rows, GATE_WIDTH), lambda b, c: (b, 0, 0)),
            pl.BlockSpec((1, SSM_WIDTH, SSM_STATE), lambda b, c: (b, 0, 0)),
            pl.BlockSpec((1, SSM_CONV - 1, SSM_CONV_DIM), lambda b, c: (b, 0, 0)),
        ],
        out_shape=[
            jax.ShapeDtypeStruct((t, 2 * GATE_WIDTH), BF16),
            jax.ShapeDtypeStruct((nb, rows, GATE_WIDTH), F32),
            jax.ShapeDtypeStruct((nb, SSM_WIDTH, SSM_STATE), F32),
            jax.ShapeDtypeStruct((nb, SSM_CONV - 1, SSM_CONV_DIM), F32),
        ],
        scratch_shapes=[
            pltpu.VMEM((GATE_HEADS, rows, rows), BF16),
            pltpu.VMEM(((SSM_CONV - 1) * rows, 2 * rows), BF16),
            pltpu.VMEM((2 * rows, SSM_CONV_DIM), BF16),
            pltpu.VMEM((SSM_STATE, SSM_WIDTH), F32),
            pltpu.VMEM((rows, SSM_WIDTH), F32),
        ],
        compiler_params=_cparams(("arbitrary", "arbitrary")),
        name="mixer_prompt",
    )(proj, proj, proj, proj, dt, prm["ln_g"], prm["ln_b"], prm["wt_p"], prm["btT_p"], prm["conv_w"], prm["conv_b"],
      prm["dtb"], prm["alog"], prm["dexp"], prm["nwm"], prm["e"])


SEQ_TILE = 8


def _state_decay_body(dt_ref, dtb_ref, alog_ref, o_ref, *, seg):
    nseq = o_ref.shape[0]
    a = -jnp.exp(alog_ref[...])
    tot = jnp.zeros(o_ref.shape, F32)
    for t in range(seg):
        d = jax.nn.softplus(dt_ref[pl.ds(t, nseq, stride=seg), :] + dtb_ref[...])
        tot = tot + d * a
    o_ref[...] = jnp.exp(tot)


def _state_decay(dt, prm, *, seg):
    nseq = dt.shape[0] // seg
    return pl.pallas_call(
        functools.partial(_state_decay_body, seg=seg),
        out_shape=jax.ShapeDtypeStruct((nseq, LANES), F32),
        name="state_decay",
    )(dt, prm["dtb"], prm["alog"])


def _mixer_sample_body(dec_ref, gu_ref, gv_ref, z_ref, xbc_ref, dt_ref, prev_ref, sin_ref,
                       lng_ref, lnb_ref, wt_ref, btT_ref, cw_ref, cb_ref,
                       dtb_ref, alog_ref, dexp_ref, nwm_ref, e_ref,
                       mix_ref, cv_ref, sout_ref, sc_ref,
                       wm_s, y_s, cm_s, bm_s, wc_s, ex_s, *, seg):
    i = pl.program_id(0)
    rows = SEQ_TILE * seg
    mask, same = _seg_masks(rows, seg)

    @pl.when(i == 0)
    def _():
        for h in range(GATE_HEADS):
            wm_s[h] = jnp.where(mask, wt_ref[h], 0.0).astype(BF16)

    cv_ref[...] = _gate_tile(gu_ref, gv_ref, lng_ref, lnb_ref, wm_s, btT_ref, mix_ref)

    x3 = xbc_ref[...].astype(F32).reshape(SEQ_TILE, seg, SSM_CONV_DIM)
    p3 = _history_block(prev_ref, slice(0, SEQ_TILE), seg)
    tpos = lax.broadcasted_iota(jnp.int32, x3.shape, 1)
    conv = cb_ref[...] + cw_ref[SSM_CONV - 1:SSM_CONV, :] * x3
    for d in range(1, SSM_CONV):
        shifted = jnp.where(tpos >= d, pltpu.roll(x3, d, axis=1), pltpu.roll(p3, d, axis=1))
        conv = conv + cw_ref[SSM_CONV - 1 - d:SSM_CONV - d, :] * shifted
    act = jax.nn.silu(conv).reshape(rows, SSM_CONV_DIM)
    sc_ref[...] = x3[:, seg - (SSM_CONV - 1):, :]

    maskf = mask.astype(BF16)
    segf = same.astype(BF16)
    xs, bm, cm, dt, cs, cl = _ssd_token_level(act, dt_ref[...], dtb_ref, alog_ref, maskf, segf)
    cs_t = cs.T
    dt_t = dt.T
    e_bf = e_ref[...]
    coef_x = _expand_heads(dt * jnp.exp(cl - cs), e_bf)
    ecs_x = _expand_heads(jnp.exp(cs), e_bf)
    lane = lax.broadcasted_iota(jnp.int32, (rows, LANES), 1)
    xs_b = xs.astype(BF16)
    for g in range(SSM_GROUPS):
        cg = cm[:, g * SSM_STATE:(g + 1) * SSM_STATE]
        bg = bm[:, g * SSM_STATE:(g + 1) * SSM_STATE]
        cb = lax.dot_general(cg.astype(BF16), bg.astype(BF16), (((1,), (1,)), ((), ())), preferred_element_type=F32)
        for p in range(HEADS_PER_GROUP // 2):
            h0 = g * HEADS_PER_GROUP + 2 * p
            sl = slice((h0 // 2) * LANES, (h0 // 2 + 1) * LANES)
            ys = [_dot(_ssd_diag_pair(cb, cs, cs_t, dt_t, mask, h).astype(BF16), xs_b[:, sl]) for h in (h0, h0 + 1)]
            y_s[:, sl] = jnp.where(lane < SSM_HEAD_DIM, ys[0], ys[1])
    cm_s[...] = cm
    bm_s[...] = bm
    wc_s[...] = xs * coef_x
    ex_s[...] = ecs_x

    rowid = lax.broadcasted_iota(jnp.int32, (rows, SSM_STATE), 0)

    def seq_step(s, carry):
        r8 = pl.ds(pl.multiple_of(s * seg, seg), seg)
        for g in range(SSM_GROUPS):
            gsl = slice(g * GROUP_WIDTH, (g + 1) * GROUP_WIDTH)
            nsl = slice(g * SSM_STATE, (g + 1) * SSM_STATE)
            st = sin_ref[s, gsl, :]
            c8 = cm_s[r8, nsl].astype(BF16)
            yo = lax.dot_general(c8, st.astype(BF16), (((1,), (1,)), ((), ())), preferred_element_type=F32)
            y_s[r8, gsl] = y_s[r8, gsl] + yo * ex_s[r8, gsl]
            bmask = jnp.where(rowid // seg == s, bm_s[:, nsl], 0.0).astype(BF16)
            upd = lax.dot_general(wc_s[:, gsl].astype(BF16), bmask, (((0,), (0,)), ((), ())), preferred_element_type=F32)
            for r in range(HEADS_PER_GROUP):
                d = dec_ref[(i * SEQ_TILE + s) * SSM_HEADS + g * HEADS_PER_GROUP + r]
                hsl = slice(r * SSM_HEAD_DIM, (r + 1) * SSM_HEAD_DIM)
                osl = slice(g * GROUP_WIDTH + r * SSM_HEAD_DIM, g * GROUP_WIDTH + (r + 1) * SSM_HEAD_DIM)
                sout_ref[s, osl, :] = st[hsl, :] * d + upd[hsl, :]
        return carry

    lax.fori_loop(0, SEQ_TILE, seq_step, 0)

    _ssd_finish(y_s[...], xs, z_ref, dexp_ref, nwm_ref, mix_ref)


def _mixer_sample(dec, proj, dt, prev, state, prm, *, seg):
    rows = SEQ_TILE * seg
    t = proj.shape[0]
    nseq = t // seg
    assert t % rows == 0
    full = lambda shape: pl.BlockSpec(shape, lambda i: (0,) * len(shape))
    return pl.pallas_call(
        functools.partial(_mixer_sample_body, seg=seg),
        grid=(t // rows,),
        in_specs=[
            pl.BlockSpec(memory_space=pltpu.SMEM),
            pl.BlockSpec((rows, GATE_WIDTH), lambda i: (i, 0)),
            pl.BlockSpec((rows, GATE_WIDTH), lambda i: (i, 1)),
            pl.BlockSpec((rows, SSM_WIDTH), lambda i: (i, 2)),
            pl.BlockSpec((rows, SSM_CONV_DIM), lambda i: (i, 2)),
            pl.BlockSpec((rows, LANES), lambda i: (i, 0)),
            pl.BlockSpec((SEQ_TILE, SSM_CONV - 1, SSM_CONV_DIM), lambda i: (i, 0, 0)),
            pl.BlockSpec((SEQ_TILE, SSM_WIDTH, SSM_STATE), lambda i: (i, 0, 0)),
            full((1, GATE_WIDTH)), full((1, GATE_WIDTH)),
            full((GATE_HEADS, rows, rows)), full((rows, GATE_HEADS)),
            full((SSM_CONV, SSM_CONV_DIM)), full((1, SSM_CONV_DIM)),
            full((1, LANES)), full((1, LANES)), full((1, SSM_WIDTH)), full((1, SSM_WIDTH)),
            full((LANES, SSM_WIDTH)),
        ],
        out_specs=[
            pl.BlockSpec((rows, 2 * GATE_WIDTH), lambda i: (i, 0)),
            pl.BlockSpec((rows, GATE_WIDTH), lambda i: (i, 0)),
            pl.BlockSpec((SEQ_TILE, SSM_WIDTH, SSM_STATE), lambda i: (i, 0, 0)),
            pl.BlockSpec((SEQ_TILE, SSM_CONV - 1, SSM_CONV_DIM), lambda i: (i, 0, 0)),
        ],
        out_shape=[
            jax.ShapeDtypeStruct((t, 2 * GATE_WIDTH), BF16),
            jax.ShapeDtypeStruct((t, GATE_WIDTH), F32),
            jax.ShapeDtypeStruct((nseq, SSM_WIDTH, SSM_STATE), F32),
            jax.ShapeDtypeStruct((nseq, SSM_CONV - 1, SSM_CONV_DIM), F32),
        ],
        scratch_shapes=[
            pltpu.VMEM((GATE_HEADS, rows, rows), BF16),
            pltpu.VMEM((rows, SSM_WIDTH), F32),
            pltpu.VMEM((rows, GROUP_WIDTH), F32),
            pltpu.VMEM((rows, GROUP_WIDTH), F32),
            pltpu.VMEM((rows, SSM_WIDTH), F32),
            pltpu.VMEM((rows, SSM_WIDTH), F32),
        ],
        compiler_params=_cparams(("arbitrary",)),
        name="mixer_sample",
    )(dec, proj, proj, proj, proj, dt, prev, state, prm["ln_g"], prm["ln_b"], prm["wt_s"], prm["btT_s"],
      prm["conv_w"], prm["conv_b"], prm["dtb"], prm["alog"], prm["dexp"], prm["nwm"], prm["e"])


ROW_SUB = 128


def _row_subs(tm, sub=ROW_SUB):
    sub = min(sub, tm)
    assert tm % sub == 0
    return [slice(r * sub, (r + 1) * sub) for r in range(tm // sub)]


def _out_proj_body(m_ref, x_ref, w_ref, npost_ref, npre_ref, x1_ref, h2_ref):
    for rs in _row_subs(m_ref.shape[0]):
        mix = _dot(m_ref[rs, :], w_ref[...])
        x1 = x_ref[rs, :] + _rms(mix, npost_ref[...])
        x1_ref[rs, :] = x1
        h2_ref[rs, :] = _rms(x1, npre_ref[...]).astype(h2_ref.dtype)


def _out_proj(mixin, x2d, w_out, npost, npre, *, tm):
    t, k = mixin.shape
    assert t % tm == 0
    return pl.pallas_call(
        _out_proj_body,
        grid=(t // tm,),
        in_specs=[
            pl.BlockSpec((tm, k), lambda i: (i, 0)),
            pl.BlockSpec((tm, D_MODEL), lambda i: (i, 0)),
            pl.BlockSpec((k, D_MODEL), lambda i: (0, 0), pipeline_mode=pl.Buffered(1)),
            pl.BlockSpec((1, D_MODEL), lambda i: (0, 0)),
            pl.BlockSpec((1, D_MODEL), lambda i: (0, 0)),
        ],
        out_specs=[
            pl.BlockSpec((tm, D_MODEL), lambda i: (i, 0)),
            pl.BlockSpec((tm, D_MODEL), lambda i: (i, 0)),
        ],
        out_shape=[
            jax.ShapeDtypeStruct((t, D_MODEL), F32),
            jax.ShapeDtypeStruct((t, D_MODEL), BF16),
        ],
        compiler_params=_cparams(("arbitrary",)),
        name="out_proj",
    )(mixin, x2d, w_out, npost, npre)


def _ffn_conv_taps(cur, shifted, cw_ref, cb_ref):
    out = cb_ref[...] + cw_ref[FFN_CONV - 1:FFN_CONV, :] * cur
    for d in range(1, FFN_CONV):
        out = out + cw_ref[FFN_CONV - 1 - d:FFN_CONV - d, :] * shifted[d - 1]
    return out


def _cast_weights_once(i, pairs):
    @pl.when(i == 0)
    def _():
        for w_ref, wb_s in pairs:
            wb_s[...] = w_ref[...].astype(BF16)


def _ffn_up_prompt_body(h_ref, wg_ref, wu_ref, cwg_ref, cwu_ref, cbg_ref, cbu_ref, a_ref, stg_ref, stu_ref,
                        wgb_s, wub_s, tg_s, tu_s, *, tiles_per_seq):
    i = pl.program_id(1)
    tm = h_ref.shape[0]
    _cast_weights_once(i, ((wg_ref, wgb_s), (wu_ref, wub_s)))

    @pl.when((i % tiles_per_seq) == 0)
    def _():
        for t_s in (tg_s, tu_s):
            t_s[...] = jnp.zeros(t_s.shape, F32)

    tails = [tg_s[...], tu_s[...]]
    row8 = lax.broadcasted_iota(jnp.int32, tg_s.shape, 0)
    for rs in _row_subs(tm, 128):
        h = h_ref[rs, :]
        convs = []
        for k, (wb_s, cw_ref, cb_ref) in enumerate(((wgb_s, cwg_ref, cbg_ref), (wub_s, cwu_ref, cbu_ref))):
            x = _dot(h, wb_s[...])
            shifted = []
            for d in range(1, FFN_CONV):
                r = pltpu.roll(x, d, axis=0)
                head = jnp.where(row8 >= d, r[0:SUBLANES], pltpu.roll(tails[k], d, axis=0))
                shifted.append(jnp.concatenate([head, r[SUBLANES:]], axis=0))
            convs.append(_ffn_conv_taps(x, shifted, cw_ref, cb_ref))
            tails[k] = x[x.shape[0] - SUBLANES:]
        a_ref[rs, :] = (jax.nn.gelu(convs[0], approximate=True) * convs[1]).astype(a_ref.dtype)
    for tail, t_s, st_ref in zip(tails, (tg_s, tu_s), (stg_ref, stu_ref)):
        t_s[...] = tail
        st_ref[0] = tail[SUBLANES - (FFN_CONV - 1):]


def _ffn_up_sample_body(h_ref, wg_ref, wu_ref, cwg_ref, cwu_ref, cbg_ref, cbu_ref, pg_ref, pu_ref,
                        a_ref, stg_ref, stu_ref, wgb_s, wub_s, *, seg):
    i = pl.program_id(1)
    tm = h_ref.shape[0]
    tn = wg_ref.shape[1]
    _cast_weights_once(i, ((wg_ref, wgb_s), (wu_ref, wub_s)))
    for rs in _row_subs(tm):
        n = rs.stop - rs.start
        nseq = n // seg
        sq = slice(rs.start // seg, rs.stop // seg)
        h = h_ref[rs, :]
        tpos = lax.broadcasted_iota(jnp.int32, (nseq, seg, tn), 1)
        convs = []
        for wb_s, p_ref, cw_ref, cb_ref, st_ref in ((wgb_s, pg_ref, cwg_ref, cbg_ref, stg_ref),
                                                    (wub_s, pu_ref, cwu_ref, cbu_ref, stu_ref)):
            x3 = _dot(h, wb_s[...]).reshape(nseq, seg, tn)
            p3 = _history_block(p_ref, sq, seg)
            shifted = [jnp.where(tpos >= d, pltpu.roll(x3, d, axis=1), pltpu.roll(p3, d, axis=1))
                       for d in range(1, FFN_CONV)]
            convs.append(_ffn_conv_taps(x3, shifted, cw_ref, cb_ref))
            st_ref[sq] = x3[:, seg - (FFN_CONV - 1):, :]
        act = jax.nn.gelu(convs[0], approximate=True) * convs[1]
        a_ref[rs, :] = act.reshape(n, tn).astype(a_ref.dtype)


def _ffn_up(h2, w_up, cw, cb, prev, *, nseq, seg, tm, tn):
    t = h2.shape[0]
    nj = D_FF // tn
    assert t % tm == 0 and D_FF % tn == 0 and t == nseq * seg
    common_in = [
        pl.BlockSpec((tm, D_MODEL), lambda j, i: (i, 0)),
        pl.BlockSpec((D_MODEL, tn), lambda j, i: (0, j)),
        pl.BlockSpec((D_MODEL, tn), lambda j, i: (0, j + nj)),
        pl.BlockSpec((FFN_CONV, tn), lambda j, i: (0, j)),
        pl.BlockSpec((FFN_CONV, tn), lambda j, i: (0, j + nj)),
        pl.BlockSpec((1, tn), lambda j, i: (0, j)),
        pl.BlockSpec((1, tn), lambda j, i: (0, j + nj)),
    ]
    a_spec = pl.BlockSpec((tm, tn), lambda j, i: (i, j))
    st_shape = jax.ShapeDtypeStruct((nseq, FFN_CONV - 1, D_FF), F32)
    out_shape = [jax.ShapeDtypeStruct((t, D_FF), BF16), st_shape, st_shape]
    wb_scratch = [pltpu.VMEM((D_MODEL, tn), BF16), pltpu.VMEM((D_MODEL, tn), BF16)]
    if prev is None:
        assert seg % tm == 0
        tps = seg // tm
        return pl.pallas_call(
            functools.partial(_ffn_up_prompt_body, tiles_per_seq=tps),
            grid=(nj, t // tm),
            in_specs=common_in,
            out_specs=[a_spec] + [pl.BlockSpec((1, FFN_CONV - 1, tn), lambda j, i: (i // tps, 0, j))] * 2,
            out_shape=out_shape,
            scratch_shapes=wb_scratch + [pltpu.VMEM((SUBLANES, tn), F32), pltpu.VMEM((SUBLANES, tn), F32)],
            compiler_params=_cparams(("arbitrary", "arbitrary")),
            name="ffn_up_prompt",
        )(h2, w_up, w_up, cw, cw, cb, cb)
    assert tm % seg == 0 and seg == SUBLANES
    return pl.pallas_call(
        functools.partial(_ffn_up_sample_body, seg=seg),
        grid=(nj, t // tm),
        in_specs=common_in + [
            pl.BlockSpec((tm // seg, FFN_CONV - 1, tn), lambda j, i: (i, 0, j)),
            pl.BlockSpec((tm // seg, FFN_CONV - 1, tn), lambda j, i: (i, 0, j + nj)),
        ],
        out_specs=[a_spec] + [pl.BlockSpec((tm // seg, FFN_CONV - 1, tn), lambda j, i: (i, 0, j))] * 2,
        out_shape=out_shape,
        scratch_shapes=wb_scratch,
        compiler_params=_cparams(("arbitrary", "arbitrary")),
        name="ffn_up_sample",
    )(h2, w_up, w_up, cw, cw, cb, cb, prev, prev)


def _ffn_down_body(a_ref, x_ref, w_ref, nw_ref, y_ref):
    for rs in _row_subs(a_ref.shape[0], 128):
        f = _dot(a_ref[rs, :], w_ref[...])
        y_ref[rs, :] = x_ref[rs, :] + _rms(f, nw_ref[...])


def _ffn_down(act, x1, w_down, nw, *, tm):
    t, k = act.shape
    assert t % tm == 0
    return pl.pallas_call(
        _ffn_down_body,
        grid=(t // tm,),
        in_specs=[
            pl.BlockSpec((tm, k), lambda i: (i, 0)),
            pl.BlockSpec((tm, D_MODEL), lambda i: (i, 0)),
            pl.BlockSpec((k, D_MODEL), lambda i: (0, 0), pipeline_mode=pl.Buffered(1)),
            pl.BlockSpec((1, D_MODEL), lambda i: (0, 0)),
        ],
        out_specs=pl.BlockSpec((tm, D_MODEL), lambda i: (i, 0)),
        out_shape=jax.ShapeDtypeStruct((t, D_MODEL), F32),
        compiler_params=_cparams(("arbitrary",)),
        name="ffn_down",
    )(act, x1, w_down, nw)


def _head_expander():
    e = np.zeros((LANES, SSM_WIDTH), np.float32)
    for h in range(SSM_HEADS):
        e[h, h * SSM_HEAD_DIM:(h + 1) * SSM_HEAD_DIM] = 1.0
    return jnp.asarray(e, BF16)


def _pad_lanes(v):
    return jnp.pad(v, (0, LANES - v.shape[0]))[None, :]


def _prep_params(norm_mix_pre, w_in, gate_ln_g, gate_ln_b, gate_w_s, gate_b_s, ssm_conv_w, ssm_conv_b, ssm_dt_bias,
                 ssm_a_log, ssm_d, ssm_norm_w, w_out, norm_mix_post, norm_ffn_pre, ffn_w_up, ffn_conv_w, ffn_conv_b,
                 ffn_w_down, norm_ffn_post, seg_sample):
    w_in_t = jnp.swapaxes(w_in, 0, 1)
    ws_small = gate_w_s[:, :seg_sample, :seg_sample]
    return dict(
        nw_pre=norm_mix_pre[None, :],
        w_in_t=w_in_t,
        w_dt=jnp.pad(w_in_t[PROJ_MAIN:, :], ((0, LANES - SSM_HEADS), (0, 0))).astype(BF16),
        ln_g=gate_ln_g[None, :], ln_b=gate_ln_b[None, :],
        wt_p=gate_w_s, btT_p=gate_b_s.T,
        wt_s=jnp.tile(ws_small, (1, SEQ_TILE, SEQ_TILE)), btT_s=jnp.tile(gate_b_s[:, :seg_sample], (1, SEQ_TILE)).T,
        conv_w=ssm_conv_w, conv_b=ssm_conv_b[None, :],
        dtb=_pad_lanes(ssm_dt_bias), alog=_pad_lanes(ssm_a_log),
        dexp=jnp.repeat(ssm_d, SSM_HEAD_DIM)[None, :], nwm=ssm_norm_w[None, :],
        e=_head_expander(),
        w_out=w_out.astype(BF16), n_post=norm_mix_post[None, :], n_pre2=norm_ffn_pre[None, :],
        w_up=ffn_w_up, fcw=ffn_conv_w, fcb=ffn_conv_b[None, :],
        w_down=ffn_w_down.astype(BF16), n_post2=norm_ffn_post[None, :],
    )


def _row_tile(t, cap):
    tm = cap
    while t % tm:
        tm //= 2
    assert tm >= 64
    return tm


TM_STREAM = 1024
TM_OUT_PROJ = 512
TM_FFN_DOWN = 256
TM_PRE_NORM = 512


def _layer_prompt(x, prm):
    nb, seq, _ = x.shape
    assert seq % CHUNK == 0
    nc = seq // CHUNK
    x2d = x.reshape(nb * seq, D_MODEL)
    tm = _row_tile(seq, TM_STREAM)
    h, dt = _pre_norm(x2d, prm["nw_pre"], prm["w_dt"], tm=_row_tile(seq, TM_PRE_NORM))
    proj = _in_proj(h, prm["w_in_t"], tm=tm, tn=1024, sub=min(tm, 256))
    mixin, cv, st, sc = _mixer_prompt(proj, dt, prm, nb=nb, nc=nc)
    x1, h2 = _out_proj(mixin, x2d, prm["w_out"], prm["n_post"], prm["n_pre2"], tm=_row_tile(seq, TM_OUT_PROJ))
    act, fst_g, fst_u = _ffn_up(h2, prm["w_up"], prm["fcw"], prm["fcb"], None, nseq=nb, seg=seq, tm=tm, tn=512)
    y = _ffn_down(act, x1, prm["w_down"], prm["n_post2"], tm=_row_tile(seq, TM_FFN_DOWN))
    return (y.reshape(nb, seq, D_MODEL),
            st.reshape(nb, SSM_HEADS, SSM_HEAD_DIM, SSM_STATE),
            sc,
            jnp.concatenate([fst_g, fst_u], axis=-1),
            cv.reshape(nb, CHUNK, GATE_HEADS, GATE_HEAD_DIM))


def _layer_sample(x, state_ssm, state_sconv, state_fconv, prm):
    nb, seg, _ = x.shape
    assert seg == SUBLANES and nb % SEQ_TILE == 0
    t = nb * seg
    x2d = x.reshape(t, D_MODEL)
    tm = _row_tile(t, TM_STREAM)
    h, dt = _pre_norm(x2d, prm["nw_pre"], prm["w_dt"], tm=_row_tile(t, TM_PRE_NORM))
    proj = _in_proj(h, prm["w_in_t"], tm=tm, tn=1024, sub=min(tm, 256))
    dec = _state_decay(dt, prm, seg=seg)[:, :SSM_HEADS].reshape(nb * SSM_HEADS)
    mixin, cv, st, sc = _mixer_sample(dec, proj, dt, state_sconv,
                                      state_ssm.reshape(nb, SSM_WIDTH, SSM_STATE), prm, seg=seg)
    x1, h2 = _out_proj(mixin, x2d, prm["w_out"], prm["n_post"], prm["n_pre2"], tm=_row_tile(t, TM_OUT_PROJ))
    act, fst_g, fst_u = _ffn_up(h2, prm["w_up"], prm["fcw"], prm["fcb"], state_fconv, nseq=nb, seg=seg, tm=tm, tn=512)
    y = _ffn_down(act, x1, prm["w_down"], prm["n_post2"], tm=_row_tile(t, TM_FFN_DOWN))
    return (y.reshape(nb, seg, D_MODEL),
            st.reshape(nb, SSM_HEADS, SSM_HEAD_DIM, SSM_STATE),
            sc,
            jnp.concatenate([fst_g, fst_u], axis=-1),
            cv.reshape(nb, seg, GATE_HEADS, GATE_HEAD_DIM))


def kernel(x_prompt, x_sample, state_ssm, state_ssm_conv, state_ffn_conv, norm_mix_pre, w_in, gate_ln_g, gate_ln_b,
           gate_w_s, gate_b_s, ssm_conv_w, ssm_conv_b, ssm_dt_bias, ssm_a_log, ssm_d, ssm_norm_w, w_out,
           norm_mix_post, norm_ffn_pre, ffn_w_up, ffn_conv_w, ffn_conv_b, ffn_w_down, norm_ffn_post):
    depth = w_in.shape[0]
    yp, ys = x_prompt, x_sample
    outs_p, outs_s = [], []
    for l in range(depth):
        prm = _prep_params(norm_mix_pre[l], w_in[l], gate_ln_g[l], gate_ln_b[l], gate_w_s[l], gate_b_s[l],
                           ssm_conv_w[l], ssm_conv_b[l], ssm_dt_bias[l], ssm_a_log[l], ssm_d[l], ssm_norm_w[l],
                           w_out[l], norm_mix_post[l], norm_ffn_pre[l], ffn_w_up[l], ffn_conv_w[l], ffn_conv_b[l],
                           ffn_w_down[l], norm_ffn_post[l], x_sample.shape[1])
        yp, *rest_p = _layer_prompt(yp, prm)
        ys, *rest_s = _layer_sample(ys, state_ssm[l], state_ssm_conv[l], state_ffn_conv[l], prm)
        outs_p.append(rest_p)
        outs_s.append(rest_s)
    stack = lambda outs, k: jnp.stack([o[k] for o in outs])
    return (yp, ys,
            stack(outs_p, 0), stack(outs_p, 1), stack(outs_p, 2), stack(outs_p, 3),
            stack(outs_s, 0), stack(outs_s, 1), stack(outs_s, 2), stack(outs_s, 3))
```

```python
import functools

import jax
import jax.numpy as jnp
import numpy as np
from jax import lax
from jax.experimental import pallas as pl
from jax.experimental.pallas import tpu as pltpu

F32 = jnp.float32
BF16 = jnp.bfloat16

D_MODEL = 2048
GATE_WIDTH = 2048
GATE_HEADS = 16
GATE_HEAD_DIM = 128
CHUNK = 128
SSM_WIDTH = 2048
SSM_HEAD_DIM = 64
SSM_HEADS = 32
SSM_GROUPS = 4
SSM_STATE = 128
SSM_CONV = 4
SSM_CONV_DIM = SSM_WIDTH + 2 * SSM_GROUPS * SSM_STATE
PROJ_MAIN = 2 * GATE_WIDTH + SSM_WIDTH + SSM_CONV_DIM
D_FF = 5632
FFN_CONV = 3
EPS = 1e-6
HEADS_PER_GROUP = SSM_HEADS // SSM_GROUPS
GROUP_WIDTH = SSM_WIDTH // SSM_GROUPS

LANES = 128
SUBLANES = 8
VMEM_LIMIT = 56 * 1024 * 1024


def _cparams(sem):
    return pltpu.CompilerParams(dimension_semantics=sem, vmem_limit_bytes=VMEM_LIMIT)


def _rms(x, w):
    return x * lax.rsqrt(jnp.mean(x * x, axis=-1, keepdims=True) + EPS) * w


def _gelu_erf(x):
    return 0.5 * x * (1.0 + lax.erf(x * np.float32(0.7071067811865476)))


def _split_bf16(x, n):
    parts = []
    r = x
    for k in range(n):
        p = r.astype(BF16)
        parts.append(p)
        if k + 1 < n:
            r = r - p.astype(F32)
    return parts


def _dot(a, b):
    return jnp.dot(a, b, preferred_element_type=F32)


def _dot_nt(a, b):
    return lax.dot_general(a, b, (((1,), (1,)), ((), ())), preferred_element_type=F32)


def _mm_split(m_bf, x, n):
    acc = None
    for p in _split_bf16(x, n):
        t = _dot(m_bf, p)
        acc = t if acc is None else acc + t
    return acc


def _expand_heads(x, e_bf):
    acc = None
    for p in _split_bf16(x, 2):
        t = _dot(p, e_bf)
        acc = t if acc is None else acc + t
    return acc


def _history_block(p_ref, sq, seg):
    k1, c = p_ref.shape[1], p_ref.shape[2]
    s = sq.stop - sq.start
    t = lax.broadcasted_iota(jnp.int32, (s, seg, c), 1)
    out = jnp.zeros((s, seg, c), F32)
    for k in range(k1):
        out = jnp.where(t == seg - k1 + k, jnp.broadcast_to(p_ref[sq, k:k + 1, :], (s, seg, c)), out)
    return out


def _seg_masks(rows, seg):
    r = lax.broadcasted_iota(jnp.int32, (rows, rows), 0)
    c = lax.broadcasted_iota(jnp.int32, (rows, rows), 1)
    same = (r // seg) == (c // seg)
    return same & (c <= r), same


BF16_ROWS = 2 * SUBLANES


def _cast_rows(rows, nsteps):
    per = BF16_ROWS
    while rows % per or rows // per > nsteps:
        per += BF16_ROWS
        assert per <= rows
    return per


def _side_cast_specs(weights, nsteps, step_of):
    ins, outs, shapes = [], [], []
    for w in weights:
        rows, cols = w.shape
        per = _cast_rows(rows, nsteps)
        spec = pl.BlockSpec((per, cols), lambda *g, nb=rows // per: (jnp.minimum(step_of(*g), nb - 1), 0))
        ins.append(spec)
        outs.append(spec)
        shapes.append(jax.ShapeDtypeStruct((rows, cols), BF16))
    return ins, outs, shapes


def _side_cast(cast_in, cast_out):
    for w_ref, o_ref in zip(cast_in, cast_out):
        o_ref[...] = w_ref[...].astype(BF16)


def _pre_norm_body(x_ref, nw_ref, wdt_ref, h_ref, dt_ref):
    h = _rms(x_ref[...], nw_ref[...]).astype(BF16)
    h_ref[...] = h
    dt_ref[...] = _dot_nt(h, wdt_ref[...])


def _pre_norm(x2d, nw, w_dt, *, tm):
    t = x2d.shape[0]
    assert t % tm == 0
    return pl.pallas_call(
        _pre_norm_body,
        grid=(t // tm,),
        in_specs=[
            pl.BlockSpec((tm, D_MODEL), lambda i: (i, 0)),
            pl.BlockSpec((1, D_MODEL), lambda i: (0, 0)),
            pl.BlockSpec((LANES, D_MODEL), lambda i: (0, 0)),
        ],
        out_specs=[
            pl.BlockSpec((tm, D_MODEL), lambda i: (i, 0)),
            pl.BlockSpec((tm, LANES), lambda i: (i, 0)),
        ],
        out_shape=[
            jax.ShapeDtypeStruct((t, D_MODEL), BF16),
            jax.ShapeDtypeStruct((t, LANES), F32),
        ],
        compiler_params=_cparams(("arbitrary",)),
        name="pre_norm",
    )(x2d, nw, w_dt)


def _in_proj_body(h_ref, w_ref, *rest, n_gelu, sub, n_cast):
    cast_in, o_ref, cast_out, wb_s = rest[:n_cast], rest[n_cast], rest[n_cast + 1:2 * n_cast + 1], rest[-1]
    j = pl.program_id(0)
    i = pl.program_id(1)
    tm = h_ref.shape[0]
    _side_cast(cast_in, cast_out)

    @pl.when(i == 0)
    def _():
        wb_s[...] = w_ref[...].astype(BF16)

    def run(epilogue):
        for rs in _row_subs(tm, sub):
            o_ref[rs, :] = epilogue(_dot_nt(h_ref[rs, :], wb_s[...])).astype(o_ref.dtype)

    @pl.when(j < n_gelu)
    def _():
        run(_gelu_erf)

    @pl.when(j >= n_gelu)
    def _():
        run(lambda a: a)


def _in_proj(h, w_in_t, *, tm, tn, sub, cast=()):
    t = h.shape[0]
    assert t % tm == 0 and PROJ_MAIN % tn == 0 and (2 * GATE_WIDTH) % tn == 0 and tm % sub == 0
    nj, ni = PROJ_MAIN // tn, t // tm
    c_in, c_out, c_shapes = _side_cast_specs(cast, nj * ni, lambda j, i: j * ni + i)
    return pl.pallas_call(
        functools.partial(_in_proj_body, n_gelu=2 * GATE_WIDTH // tn, sub=sub, n_cast=len(cast)),
        grid=(nj, ni),
        in_specs=[
            pl.BlockSpec((tm, D_MODEL), lambda j, i: (i, 0)),
            pl.BlockSpec((tn, D_MODEL), lambda j, i: (j, 0)),
        ] + c_in,
        out_specs=[pl.BlockSpec((tm, tn), lambda j, i: (i, j))] + c_out,
        out_shape=[jax.ShapeDtypeStruct((t, PROJ_MAIN), BF16)] + c_shapes,
        scratch_shapes=[pltpu.VMEM((tn, D_MODEL), BF16)],
        compiler_params=_cparams(("arbitrary", "arbitrary")),
        name="in_proj",
    )(h, w_in_t, *cast)


def _gate_tile(gu_ref, gv_ref, lng_ref, lnb_ref, wm_s, btT_ref, mix_ref):
    g = gv_ref[...].astype(F32)
    mu = jnp.mean(g, axis=-1, keepdims=True)
    xc = g - mu
    v = xc * lax.rsqrt(jnp.mean(xc * xc, axis=-1, keepdims=True) + EPS) * lng_ref[...] + lnb_ref[...]
    vb = v.astype(BF16)
    rows = v.shape[0]
    for h in range(GATE_HEADS):
        sl = slice(h * GATE_HEAD_DIM, (h + 1) * GATE_HEAD_DIM)
        s = _dot(wm_s[h], vb[:, sl]) + jnp.broadcast_to(btT_ref[:, h:h + 1], (rows, GATE_HEAD_DIM))
        mix_ref[:, sl] = (gu_ref[:, sl].astype(F32) * s).astype(mix_ref.dtype)
    return v


def _ssd_token_level(act, dt_raw, dtb_ref, alog_ref, maskf, segf):
    xs = act[:, :SSM_WIDTH]
    bm = act[:, SSM_WIDTH:SSM_WIDTH + GROUP_WIDTH]
    cm = act[:, SSM_WIDTH + GROUP_WIDTH:]
    dt = jax.nn.softplus(dt_raw + dtb_ref[...])
    a = -jnp.exp(alog_ref[...])
    da = dt * a
    cs = _mm_split(maskf, da, 3)
    cl = _mm_split(segf, da, 3)
    return xs, bm, cm, dt, cs, cl


def _ssd_diag_pair(cb, cs, cs_t, dt_t, mask, h):
    seg = cs[:, h:h + 1] - cs_t[h:h + 1, :]
    decay = jnp.exp(jnp.where(mask, seg, -jnp.inf))
    return cb * decay * dt_t[h:h + 1, :]


def _ssd_finish(y, xs, z_ref, dexp_ref, nwm_ref, mix_ref):
    y = y + dexp_ref[...] * xs
    y = y * jax.nn.silu(z_ref[...].astype(F32))
    for g in range(SSM_GROUPS):
        sl = slice(g * GROUP_WIDTH, (g + 1) * GROUP_WIDTH)
        yg = y[:, sl]
        yg = yg * lax.rsqrt(jnp.mean(yg * yg, axis=-1, keepdims=True) + EPS) * nwm_ref[:, sl]
        mix_ref[:, GATE_WIDTH + g * GROUP_WIDTH:GATE_WIDTH + (g + 1) * GROUP_WIDTH] = yg.astype(mix_ref.dtype)


def _mixer_prompt_body(gu_ref, gv_ref, z_ref, xbc_ref, dt_ref, lng_ref, lnb_ref, wt_ref, btT_ref, cw_ref, cb_ref,
                       dtb_ref, alog_ref, dexp_ref, nwm_ref, e_ref,
                       mix_ref, cv_ref, st_ref, sc_ref,
                       wm_s, shift_s, xx_s, st_s, y_s):
    b = pl.program_id(0)
    c = pl.program_id(1)
    last = c == pl.num_programs(1) - 1
    rows = CHUNK
    keep = 2 * SUBLANES
    mask, same = _seg_masks(rows, rows)

    @pl.when((b == 0) & (c == 0))
    def _():
        for h in range(GATE_HEADS):
            wm_s[h] = jnp.where(mask, wt_ref[h], 0.0).astype(BF16)
        r = lax.broadcasted_iota(jnp.int32, shift_s.shape, 0)
        col = lax.broadcasted_iota(jnp.int32, shift_s.shape, 1)
        shift_s[...] = jnp.where(col == rows + r % rows - (r // rows + 1), 1.0, 0.0).astype(BF16)
        xx_s[0:rows, :] = jnp.zeros((rows, SSM_CONV_DIM), BF16)

    @pl.when(c == 0)
    def _():
        xx_s[rows - keep:rows, :] = jnp.zeros((keep, SSM_CONV_DIM), BF16)
        st_s[...] = jnp.zeros(st_s.shape, F32)

    v = _gate_tile(gu_ref, gv_ref, lng_ref, lnb_ref, wm_s, btT_ref, mix_ref)

    @pl.when(last)
    def _():
        cv_ref[0] = v

    xx_s[rows:2 * rows, :] = xbc_ref[...]
    sh = _dot(shift_s[...], xx_s[...])
    conv = cb_ref[...] + cw_ref[SSM_CONV - 1:SSM_CONV, :] * xbc_ref[...].astype(F32)
    for d in range(1, SSM_CONV):
        conv = conv + cw_ref[SSM_CONV - 1 - d:SSM_CONV - d, :] * sh[(d - 1) * rows:d * rows, :]
    act = jax.nn.silu(conv)

    @pl.when(last)
    def _():
        sc_ref[0] = xbc_ref[rows - (SSM_CONV - 1):rows, :].astype(F32)

    xx_s[rows - keep:rows, :] = xx_s[2 * rows - keep:2 * rows, :]

    maskf = mask.astype(BF16)
    segf = same.astype(BF16)
    xs, bm, cm, dt, cs, cl = _ssd_token_level(act, dt_ref[...], dtb_ref, alog_ref, maskf, segf)
    cs_t = cs.T
    dt_t = dt.T
    ecs = jnp.exp(cs)
    e_bf = e_ref[...]
    coef_x = _expand_heads(dt * jnp.exp(cl - cs), e_bf)
    dlast_x = _expand_heads(jnp.exp(cl[0:SUBLANES, :]), e_bf)[0:1, :]
    lane = lax.broadcasted_iota(jnp.int32, (rows, LANES), 1)
    xs_b = xs.astype(BF16)
    for g in range(SSM_GROUPS):
        cg = cm[:, g * SSM_STATE:(g + 1) * SSM_STATE]
        bg = bm[:, g * SSM_STATE:(g + 1) * SSM_STATE]
        cb = lax.dot_general(cg.astype(BF16), bg.astype(BF16), (((1,), (1,)), ((), ())), preferred_element_type=F32)
        for p in range(HEADS_PER_GROUP // 2):
            h0 = g * HEADS_PER_GROUP + 2 * p
            sl = slice((h0 // 2) * LANES, (h0 // 2 + 1) * LANES)
            rhs = jnp.concatenate([xs_b[:, sl], st_s[:, sl].astype(BF16)], axis=0)
            ys = []
            for h in (h0, h0 + 1):
                m_h = _ssd_diag_pair(cb, cs, cs_t, dt_t, mask, h)
                c_h = cg * jnp.broadcast_to(ecs[:, h:h + 1], (rows, SSM_STATE))
                lhs = jnp.concatenate([m_h.astype(BF16), c_h.astype(BF16)], axis=1)
                ys.append(_dot(lhs, rhs))
            y_s[:, sl] = jnp.where(lane < SSM_HEAD_DIM, ys[0], ys[1])
    wc = (xs * coef_x).astype(BF16)
    for g in range(SSM_GROUPS):
        sl = slice(g * GROUP_WIDTH, (g + 1) * GROUP_WIDTH)
        bg = bm[:, g * SSM_STATE:(g + 1) * SSM_STATE].astype(BF16)
        upd = lax.dot_general(bg, wc[:, sl], (((0,), (0,)), ((), ())), preferred_element_type=F32)
        st_s[:, sl] = st_s[:, sl] * dlast_x[:, sl] + upd

    @pl.when(last)
    def _():
        st_ref[0] = st_s[...].T

    _ssd_finish(y_s[...], xs, z_ref, dexp_ref, nwm_ref, mix_ref)


def _mixer_prompt(proj, dt, prm, *, nb, nc):
    rows = CHUNK
    t = nb * nc * rows
    row = lambda b, c: b * nc + c
    full = lambda shape: pl.BlockSpec(shape, lambda b, c: (0,) * len(shape))
    return pl.pallas_call(
        _mixer_prompt_body,
        grid=(nb, nc),
        in_specs=[
            pl.BlockSpec((rows, GATE_WIDTH), lambda b, c: (row(b, c), 0)),
            pl.BlockSpec((rows, GATE_WIDTH), lambda b, c: (row(b, c), 1)),
            pl.BlockSpec((rows, SSM_WIDTH), lambda b, c: (row(b, c), 2)),
            pl.BlockSpec((rows, SSM_CONV_DIM), lambda b, c: (row(b, c), 2)),
            pl.BlockSpec((rows, LANES), lambda b, c: (row(b, c), 0)),
            full((1, GATE_WIDTH)), full((1, GATE_WIDTH)),
            full((GATE_HEADS, rows, rows)), full((rows, GATE_HEADS)),
            full((SSM_CONV, SSM_CONV_DIM)), full((1, SSM_CONV_DIM)),
            full((1, LANES)), full((1, LANES)), full((1, SSM_WIDTH)), full((1, SSM_WIDTH)),
            full((LANES, SSM_WIDTH)),
        ],
        out_specs=[
            pl.BlockSpec((rows, 2 * GATE_WIDTH), lambda b, c: (row(b, c), 0)),
            pl.BlockSpec((1, rows, GATE_WIDTH), lambda b, c: (b, 0, 0)),
            pl.BlockSpec((1, SSM_WIDTH, SSM_STATE), lambda b, c: (b, 0, 0)),
            pl.BlockSpec((1, SSM_CONV - 1, SSM_CONV_DIM), lambda b, c: (b, 0, 0)),
        ],
        out_shape=[
            jax.ShapeDtypeStruct((t, 2 * GATE_WIDTH), BF16),
            jax.ShapeDtypeStruct((nb, rows, GATE_WIDTH), F32),
            jax.ShapeDtypeStruct((nb, SSM_WIDTH, SSM_STATE), F32),
            jax.ShapeDtypeStruct((nb, SSM_CONV - 1, SSM_CONV_DIM), F32),
        ],
        scratch_shapes=[
            pltpu.VMEM((GATE_HEADS, rows, rows), BF16),
            pltpu.VMEM(((SSM_CONV - 1) * rows, 2 * rows), BF16),
            pltpu.VMEM((2 * rows, SSM_CONV_DIM), BF16),
            pltpu.VMEM((SSM_STATE, SSM_WIDTH), F32),
            pltpu.VMEM((rows, SSM_WIDTH), F32),
        ],
        compiler_params=_cparams(("arbitrary", "arbitrary")),
        name="mixer_prompt",
    )(proj, proj, proj, proj, dt, prm["ln_g"], prm["ln_b"], prm["wt_p"], prm["btT_p"], prm["conv_w"], prm["conv_b"],
      prm["dtb"], prm["alog"], prm["dexp"], prm["nwm"], prm["e"])


SEQ_TILE = 8


def _state_decay_body(dt_ref, dtb_ref, alog_ref, o_ref, *, seg):
    nseq = o_ref.shape[0]
    a = -jnp.exp(alog_ref[...])
    tot = jnp.zeros(o_ref.shape, F32)
    for t in range(seg):
        d = jax.nn.softplus(dt_ref[pl.ds(t, nseq, stride=seg), :] + dtb_ref[...])
        tot = tot + d * a
    o_ref[...] = jnp.exp(tot)


def _state_decay(dt, prm, *, seg):
    nseq = dt.shape[0] // seg
    return pl.pallas_call(
        functools.partial(_state_decay_body, seg=seg),
        out_shape=jax.ShapeDtypeStruct((nseq, LANES), F32),
        name="state_decay",
    )(dt, prm["dtb"], prm["alog"])


def _mixer_sample_body(dec_ref, gu_ref, gv_ref, z_ref, xbc_ref, dt_ref, prev_ref, sin_ref,
                       lng_ref, lnb_ref, wt_ref, btT_ref, cw_ref, cb_ref,
                       dtb_ref, alog_ref, dexp_ref, nwm_ref, e_ref,
                       mix_ref, cv_ref, sout_ref, sc_ref,
                       wm_s, y_s, cm_s, bm_s, wct_s, ex_s, *, seg):
    i = pl.program_id(0)
    rows = SEQ_TILE * seg
    mask, same = _seg_masks(rows, seg)

    @pl.when(i == 0)
    def _():
        for h in range(GATE_HEADS):
            wm_s[h] = jnp.where(mask, wt_ref[h], 0.0).astype(BF16)

    cv_ref[...] = _gate_tile(gu_ref, gv_ref, lng_ref, lnb_ref, wm_s, btT_ref, mix_ref)

    x3 = xbc_ref[...].astype(F32).reshape(SEQ_TILE, seg, SSM_CONV_DIM)
    p3 = _history_block(prev_ref, slice(0, SEQ_TILE), seg)
    tpos = lax.broadcasted_iota(jnp.int32, x3.shape, 1)
    conv = cb_ref[...] + cw_ref[SSM_CONV - 1:SSM_CONV, :] * x3
    for d in range(1, SSM_CONV):
        shifted = jnp.where(tpos >= d, pltpu.roll(x3, d, axis=1), pltpu.roll(p3, d, axis=1))
        conv = conv + cw_ref[SSM_CONV - 1 - d:SSM_CONV - d, :] * shifted
    act = jax.nn.silu(conv).reshape(rows, SSM_CONV_DIM)
    sc_ref[...] = x3[:, seg - (SSM_CONV - 1):, :]

    maskf = mask.astype(BF16)
    segf = same.astype(BF16)
    xs, bm, cm, dt, cs, cl = _ssd_token_level(act, dt_ref[...], dtb_ref, alog_ref, maskf, segf)
    cs_t = cs.T
    dt_t = dt.T
    e_bf = e_ref[...]
    coef_x = _expand_heads(dt * jnp.exp(cl - cs), e_bf)
    ecs_x = _expand_heads(jnp.exp(cs), e_bf)
    lane = lax.broadcasted_iota(jnp.int32, (rows, LANES), 1)
    xs_b = xs.astype(BF16)
    for g in range(SSM_GROUPS):
        cg = cm[:, g * SSM_STATE:(g + 1) * SSM_STATE]
        bg = bm[:, g * SSM_STATE:(g + 1) * SSM_STATE]
        cb = lax.dot_general(cg.astype(BF16), bg.astype(BF16), (((1,), (1,)), ((), ())), preferred_element_type=F32)
        for p in range(HEADS_PER_GROUP // 2):
            h0 = g * HEADS_PER_GROUP + 2 * p
            sl = slice((h0 // 2) * LANES, (h0 // 2 + 1) * LANES)
            ys = [_dot(_ssd_diag_pair(cb, cs, cs_t, dt_t, mask, h).astype(BF16), xs_b[:, sl]) for h in (h0, h0 + 1)]
            y_s[:, sl] = jnp.where(lane < SSM_HEAD_DIM, ys[0], ys[1])
    cm_s[...] = cm
    bm_s[...] = bm
    wct_s[...] = (xs * coef_x).T
    ex_s[...] = ecs_x

    rowid = lax.broadcasted_iota(jnp.int32, (rows, SSM_STATE), 0)

    for s in range(SEQ_TILE):
        r8 = slice(s * seg, (s + 1) * seg)
        for g in range(SSM_GROUPS):
            gsl = slice(g * GROUP_WIDTH, (g + 1) * GROUP_WIDTH)
            nsl = slice(g * SSM_STATE, (g + 1) * SSM_STATE)
            st = sin_ref[s, gsl, :]
            c8 = cm_s[r8, nsl].astype(BF16)
            yo = lax.dot_general(c8, st.astype(BF16), (((1,), (1,)), ((), ())), preferred_element_type=F32)
            y_s[r8, gsl] = y_s[r8, gsl] + yo * ex_s[r8, gsl]
            bmask = jnp.where(rowid // seg == s, bm_s[:, nsl], 0.0).astype(BF16)
            upd = _dot(wct_s[gsl, :].astype(BF16), bmask)
            for r in range(HEADS_PER_GROUP):
                d = dec_ref[(i * SEQ_TILE + s) * SSM_HEADS + g * HEADS_PER_GROUP + r]
                hsl = slice(r * SSM_HEAD_DIM, (r + 1) * SSM_HEAD_DIM)
                osl = slice(g * GROUP_WIDTH + r * SSM_HEAD_DIM, g * GROUP_WIDTH + (r + 1) * SSM_HEAD_DIM)
                sout_ref[s, osl, :] = st[hsl, :] * d + upd[hsl, :]

    _ssd_finish(y_s[...], xs, z_ref, dexp_ref, nwm_ref, mix_ref)


def _mixer_sample(dec, proj, dt, prev, state, prm, *, seg):
    rows = SEQ_TILE * seg
    t = proj.shape[0]
    nseq = t // seg
    assert t % rows == 0
    full = lambda shape: pl.BlockSpec(shape, lambda i: (0,) * len(shape))
    return pl.pallas_call(
        functools.partial(_mixer_sample_body, seg=seg),
        grid=(t // rows,),
        in_specs=[
            pl.BlockSpec(memory_space=pltpu.SMEM),
            pl.BlockSpec((rows, GATE_WIDTH), lambda i: (i, 0)),
            pl.BlockSpec((rows, GATE_WIDTH), lambda i: (i, 1)),
            pl.BlockSpec((rows, SSM_WIDTH), lambda i: (i, 2)),
            pl.BlockSpec((rows, SSM_CONV_DIM), lambda i: (i, 2)),
            pl.BlockSpec((rows, LANES), lambda i: (i, 0)),
            pl.BlockSpec((SEQ_TILE, SSM_CONV - 1, SSM_CONV_DIM), lambda i: (i, 0, 0)),
            pl.BlockSpec((SEQ_TILE, SSM_WIDTH, SSM_STATE), lambda i: (i, 0, 0)),
            full((1, GATE_WIDTH)), full((1, GATE_WIDTH)),
            full((GATE_HEADS, rows, rows)), full((rows, GATE_HEADS)),
            full((SSM_CONV, SSM_CONV_DIM)), full((1, SSM_CONV_DIM)),
            full((1, LANES)), full((1, LANES)), full((1, SSM_WIDTH)), full((1, SSM_WIDTH)),
            full((LANES, SSM_WIDTH)),
        ],
        out_specs=[
            pl.BlockSpec((rows, 2 * GATE_WIDTH), lambda i: (i, 0)),
            pl.BlockSpec((rows, GATE_WIDTH), lambda i: (i, 0)),
            pl.BlockSpec((SEQ_TILE, SSM_WIDTH, SSM_STATE), lambda i: (i, 0, 0)),
            pl.BlockSpec((SEQ_TILE, SSM_CONV - 1, SSM_CONV_DIM), lambda i: (i, 0, 0)),
        ],
        out_shape=[
            jax.ShapeDtypeStruct((t, 2 * GATE_WIDTH), BF16),
            jax.ShapeDtypeStruct((t, GATE_WIDTH), F32),
            jax.ShapeDtypeStruct((nseq, SSM_WIDTH, SSM_STATE), F32),
            jax.ShapeDtypeStruct((nseq, SSM_CONV - 1, SSM_CONV_DIM), F32),
        ],
        scratch_shapes=[
            pltpu.VMEM((GATE_HEADS, rows, rows), BF16),
            pltpu.VMEM((rows, SSM_WIDTH), F32),
            pltpu.VMEM((rows, GROUP_WIDTH), F32),
            pltpu.VMEM((rows, GROUP_WIDTH), F32),
            pltpu.VMEM((SSM_WIDTH, rows), F32),
            pltpu.VMEM((rows, SSM_WIDTH), F32),
        ],
        compiler_params=_cparams(("arbitrary",)),
        name="mixer_sample",
    )(dec, proj, proj, proj, proj, dt, prev, state, prm["ln_g"], prm["ln_b"], prm["wt_s"], prm["btT_s"],
      prm["conv_w"], prm["conv_b"], prm["dtb"], prm["alog"], prm["dexp"], prm["nwm"], prm["e"])


ROW_SUB = 128


def _row_subs(tm, sub=ROW_SUB):
    sub = min(sub, tm)
    assert tm % sub == 0
    return [slice(r * sub, (r + 1) * sub) for r in range(tm // sub)]


def _out_proj_body(m_ref, x_ref, w_ref, npost_ref, npre_ref, x1_ref, h2_ref):
    for rs in _row_subs(m_ref.shape[0]):
        mix = _dot(m_ref[rs, :], w_ref[...])
        x1 = x_ref[rs, :] + _rms(mix, npost_ref[...])
        x1_ref[rs, :] = x1
        h2_ref[rs, :] = _rms(x1, npre_ref[...]).astype(h2_ref.dtype)


def _out_proj(mixin, x2d, w_out, npost, npre, *, tm):
    t, k = mixin.shape
    assert t % tm == 0
    return pl.pallas_call(
        _out_proj_body,
        grid=(t // tm,),
        in_specs=[
            pl.BlockSpec((tm, k), lambda i: (i, 0)),
            pl.BlockSpec((tm, D_MODEL), lambda i: (i, 0)),
            pl.BlockSpec((k, D_MODEL), lambda i: (0, 0), pipeline_mode=pl.Buffered(1)),
            pl.BlockSpec((1, D_MODEL), lambda i: (0, 0)),
            pl.BlockSpec((1, D_MODEL), lambda i: (0, 0)),
        ],
        out_specs=[
            pl.BlockSpec((tm, D_MODEL), lambda i: (i, 0)),
            pl.BlockSpec((tm, D_MODEL), lambda i: (i, 0)),
        ],
        out_shape=[
            jax.ShapeDtypeStruct((t, D_MODEL), F32),
            jax.ShapeDtypeStruct((t, D_MODEL), BF16),
        ],
        compiler_params=_cparams(("arbitrary",)),
        name="out_proj",
    )(mixin, x2d, w_out, npost, npre)


def _ffn_conv_taps(cur, shifted, cw_ref, cb_ref):
    out = cb_ref[...] + cw_ref[FFN_CONV - 1:FFN_CONV, :] * cur
    for d in range(1, FFN_CONV):
        out = out + cw_ref[FFN_CONV - 1 - d:FFN_CONV - d, :] * shifted[d - 1]
    return out


def _ffn_up_prompt_body(h_ref, wg_ref, wu_ref, cwg_ref, cwu_ref, cbg_ref, cbu_ref, *rest, tiles_per_seq, n_cast):
    cast_in, (a_ref, stg_ref, stu_ref) = rest[:n_cast], rest[n_cast:n_cast + 3]
    cast_out, (tg_s, tu_s) = rest[n_cast + 3:2 * n_cast + 3], rest[2 * n_cast + 3:]
    i = pl.program_id(1)
    tm = h_ref.shape[0]
    _side_cast(cast_in, cast_out)

    @pl.when((i % tiles_per_seq) == 0)
    def _():
        for t_s in (tg_s, tu_s):
            t_s[...] = jnp.zeros(t_s.shape, F32)

    tails = [tg_s[...], tu_s[...]]
    row8 = lax.broadcasted_iota(jnp.int32, tg_s.shape, 0)
    for rs in _row_subs(tm, 128):
        h = h_ref[rs, :]
        convs = []
        for k, (w_ref, cw_ref, cb_ref) in enumerate(((wg_ref, cwg_ref, cbg_ref), (wu_ref, cwu_ref, cbu_ref))):
            x = _dot(h, w_ref[...])
            shifted = []
            for d in range(1, FFN_CONV):
                r = pltpu.roll(x, d, axis=0)
                head = jnp.where(row8 >= d, r[0:SUBLANES], pltpu.roll(tails[k], d, axis=0))
                shifted.append(jnp.concatenate([head, r[SUBLANES:]], axis=0))
            convs.append(_ffn_conv_taps(x, shifted, cw_ref, cb_ref))
            tails[k] = x[x.shape[0] - SUBLANES:]
        a_ref[rs, :] = (jax.nn.gelu(convs[0], approximate=True) * convs[1]).astype(a_ref.dtype)
    for tail, t_s, st_ref in zip(tails, (tg_s, tu_s), (stg_ref, stu_ref)):
        t_s[...] = tail
        st_ref[0] = tail[SUBLANES - (FFN_CONV - 1):]


def _ffn_up_sample_body(h_ref, wg_ref, wu_ref, cwg_ref, cwu_ref, cbg_ref, cbu_ref, pg_ref, pu_ref,
                        a_ref, stg_ref, stu_ref, *, seg):
    tm = h_ref.shape[0]
    tn = wg_ref.shape[1]
    for rs in _row_subs(tm):
        n = rs.stop - rs.start
        nseq = n // seg
        sq = slice(rs.start // seg, rs.stop // seg)
        h = h_ref[rs, :]
        tpos = lax.broadcasted_iota(jnp.int32, (nseq, seg, tn), 1)
        convs = []
        for w_ref, p_ref, cw_ref, cb_ref, st_ref in ((wg_ref, pg_ref, cwg_ref, cbg_ref, stg_ref),
                                                     (wu_ref, pu_ref, cwu_ref, cbu_ref, stu_ref)):
            x3 = _dot(h, w_ref[...]).reshape(nseq, seg, tn)
            p3 = _history_block(p_ref, sq, seg)
            shifted = [jnp.where(tpos >= d, pltpu.roll(x3, d, axis=1), pltpu.roll(p3, d, axis=1))
                       for d in range(1, FFN_CONV)]
            convs.append(_ffn_conv_taps(x3, shifted, cw_ref, cb_ref))
            st_ref[sq] = x3[:, seg - (FFN_CONV - 1):, :]
        act = jax.nn.gelu(convs[0], approximate=True) * convs[1]
        a_ref[rs, :] = act.reshape(n, tn).astype(a_ref.dtype)


def _ffn_up(h2, w_up, cw, cb, prev, *, nseq, seg, tm, tn, cast=()):
    t = h2.shape[0]
    nj, ni = D_FF // tn, t // tm
    assert t % tm == 0 and D_FF % tn == 0 and t == nseq * seg
    common_in = [
        pl.BlockSpec((tm, D_MODEL), lambda j, i: (i, 0)),
        pl.BlockSpec((D_MODEL, tn), lambda j, i: (0, j)),
        pl.BlockSpec((D_MODEL, tn), lambda j, i: (0, j + nj)),
        pl.BlockSpec((FFN_CONV, tn), lambda j, i: (0, j)),
        pl.BlockSpec((FFN_CONV, tn), lambda j, i: (0, j + nj)),
        pl.BlockSpec((1, tn), lambda j, i: (0, j)),
        pl.BlockSpec((1, tn), lambda j, i: (0, j + nj)),
    ]
    a_spec = pl.BlockSpec((tm, tn), lambda j, i: (i, j))
    st_shape = jax.ShapeDtypeStruct((nseq, FFN_CONV - 1, D_FF), F32)
    out_shape = [jax.ShapeDtypeStruct((t, D_FF), BF16), st_shape, st_shape]
    if prev is None:
        assert seg % tm == 0
        tps = seg // tm
        c_in, c_out, c_shapes = _side_cast_specs(cast, nj * ni, lambda j, i: j * ni + i)
        return pl.pallas_call(
            functools.partial(_ffn_up_prompt_body, tiles_per_seq=tps, n_cast=len(cast)),
            grid=(nj, ni),
            in_specs=common_in + c_in,
            out_specs=[a_spec] + [pl.BlockSpec((1, FFN_CONV - 1, tn), lambda j, i: (i // tps, 0, j))] * 2 + c_out,
            out_shape=out_shape + c_shapes,
            scratch_shapes=[pltpu.VMEM((SUBLANES, tn), F32), pltpu.VMEM((SUBLANES, tn), F32)],
            compiler_params=_cparams(("arbitrary", "arbitrary")),
            name="ffn_up_prompt",
        )(h2, w_up, w_up, cw, cw, cb, cb, *cast)
    assert tm % seg == 0 and seg == SUBLANES and not cast
    return pl.pallas_call(
        functools.partial(_ffn_up_sample_body, seg=seg),
        grid=(nj, ni),
        in_specs=common_in + [
            pl.BlockSpec((tm // seg, FFN_CONV - 1, tn), lambda j, i: (i, 0, j)),
            pl.BlockSpec((tm // seg, FFN_CONV - 1, tn), lambda j, i: (i, 0, j + nj)),
        ],
        out_specs=[a_spec] + [pl.BlockSpec((tm // seg, FFN_CONV - 1, tn), lambda j, i: (i, 0, j))] * 2,
        out_shape=out_shape,
        compiler_params=_cparams(("arbitrary", "arbitrary")),
        name="ffn_up_sample",
    )(h2, w_up, w_up, cw, cw, cb, cb, prev, prev)


def _ffn_down_body(a_ref, x_ref, w_ref, nw_ref, y_ref):
    for rs in _row_subs(a_ref.shape[0], 128):
        f = _dot(a_ref[rs, :], w_ref[...])
        y_ref[rs, :] = x_ref[rs, :] + _rms(f, nw_ref[...])


def _ffn_down(act, x1, w_down, nw, *, tm):
    t, k = act.shape
    assert t % tm == 0
    return pl.pallas_call(
        _ffn_down_body,
        grid=(t // tm,),
        in_specs=[
            pl.BlockSpec((tm, k), lambda i: (i, 0)),
            pl.BlockSpec((tm, D_MODEL), lambda i: (i, 0)),
            pl.BlockSpec((k, D_MODEL), lambda i: (0, 0), pipeline_mode=pl.Buffered(1)),
            pl.BlockSpec((1, D_MODEL), lambda i: (0, 0)),
        ],
        out_specs=pl.BlockSpec((tm, D_MODEL), lambda i: (i, 0)),
        out_shape=jax.ShapeDtypeStruct((t, D_MODEL), F32),
        compiler_params=_cparams(("arbitrary",)),
        name="ffn_down",
    )(act, x1, w_down, nw)


def _head_expander():
    e = np.zeros((LANES, SSM_WIDTH), np.float32)
    for h in range(SSM_HEADS):
        e[h, h * SSM_HEAD_DIM:(h + 1) * SSM_HEAD_DIM] = 1.0
    return jnp.asarray(e, BF16)


def _pad_lanes(v):
    return jnp.pad(v, (0, LANES - v.shape[0]))[None, :]


def _prep_params(norm_mix_pre, w_in, gate_ln_g, gate_ln_b, gate_w_s, gate_b_s, ssm_conv_w, ssm_conv_b, ssm_dt_bias,
                 ssm_a_log, ssm_d, ssm_norm_w, w_out, norm_mix_post, norm_ffn_pre, ffn_w_up, ffn_conv_w, ffn_conv_b,
                 ffn_w_down, norm_ffn_post, seg_sample):
    w_in_t = jnp.swapaxes(w_in, 0, 1)
    ws_small = gate_w_s[:, :seg_sample, :seg_sample]
    return dict(
        nw_pre=norm_mix_pre[None, :],
        w_in_t=w_in_t,
        w_dt=jnp.pad(w_in_t[PROJ_MAIN:, :], ((0, LANES - SSM_HEADS), (0, 0))).astype(BF16),
        ln_g=gate_ln_g[None, :], ln_b=gate_ln_b[None, :],
        wt_p=gate_w_s, btT_p=gate_b_s.T,
        wt_s=jnp.tile(ws_small, (1, SEQ_TILE, SEQ_TILE)), btT_s=jnp.tile(gate_b_s[:, :seg_sample], (1, SEQ_TILE)).T,
        conv_w=ssm_conv_w, conv_b=ssm_conv_b[None, :],
        dtb=_pad_lanes(ssm_dt_bias), alog=_pad_lanes(ssm_a_log),
        dexp=jnp.repeat(ssm_d, SSM_HEAD_DIM)[None, :], nwm=ssm_norm_w[None, :],
        e=_head_expander(),
        w_out=w_out, n_post=norm_mix_post[None, :], n_pre2=norm_ffn_pre[None, :],
        w_up=ffn_w_up, fcw=ffn_conv_w, fcb=ffn_conv_b[None, :],
        w_down=ffn_w_down, n_post2=norm_ffn_post[None, :],
    )


def _row_tile(t, cap):
    tm = cap
    while t % tm:
        tm //= 2
    assert tm >= 64
    return tm


TM_STREAM = 1024
TM_OUT_PROJ = 512
TM_FFN_DOWN = 256
TM_PRE_NORM = 512


def _layer_prompt(x, prm):
    nb, seq, _ = x.shape
    assert seq % CHUNK == 0
    nc = seq // CHUNK
    x2d = x.reshape(nb * seq, D_MODEL)
    tm = _row_tile(seq, TM_STREAM)
    h, dt = _pre_norm(x2d, prm["nw_pre"], prm["w_dt"], tm=_row_tile(seq, TM_PRE_NORM))
    proj, w_out, w_up = _in_proj(h, prm["w_in_t"], tm=tm, tn=1024, sub=min(tm, 256), cast=(prm["w_out"], prm["w_up"]))
    mixin, cv, st, sc = _mixer_prompt(proj, dt, prm, nb=nb, nc=nc)
    x1, h2 = _out_proj(mixin, x2d, w_out, prm["n_post"], prm["n_pre2"], tm=_row_tile(seq, TM_OUT_PROJ))
    act, fst_g, fst_u, w_down = _ffn_up(h2, w_up, prm["fcw"], prm["fcb"], None, nseq=nb, seg=seq, tm=tm, tn=512,
                                        cast=(prm["w_down"],))
    y = _ffn_down(act, x1, w_down, prm["n_post2"], tm=_row_tile(seq, TM_FFN_DOWN))
    return dict(w_out=w_out, w_up=w_up, w_down=w_down), (
            y.reshape(nb, seq, D_MODEL),
            st.reshape(nb, SSM_HEADS, SSM_HEAD_DIM, SSM_STATE),
            sc,
            jnp.concatenate([fst_g, fst_u], axis=-1),
            cv.reshape(nb, CHUNK, GATE_HEADS, GATE_HEAD_DIM))


def _layer_sample(x, state_ssm, state_sconv, state_fconv, prm, wb):
    nb, seg, _ = x.shape
    assert seg == SUBLANES and nb % SEQ_TILE == 0
    t = nb * seg
    x2d = x.reshape(t, D_MODEL)
    tm = _row_tile(t, TM_STREAM)
    h, dt = _pre_norm(x2d, prm["nw_pre"], prm["w_dt"], tm=_row_tile(t, TM_PRE_NORM))
    proj, = _in_proj(h, prm["w_in_t"], tm=tm, tn=1024, sub=min(tm, 256))
    dec = _state_decay(dt, prm, seg=seg)[:, :SSM_HEADS].reshape(nb * SSM_HEADS)
    mixin, cv, st, sc = _mixer_sample(dec, proj, dt, state_sconv,
                                      state_ssm.reshape(nb, SSM_WIDTH, SSM_STATE), prm, seg=seg)
    x1, h2 = _out_proj(mixin, x2d, wb["w_out"], prm["n_post"], prm["n_pre2"], tm=_row_tile(t, TM_OUT_PROJ))
    act, fst_g, fst_u = _ffn_up(h2, wb["w_up"], prm["fcw"], prm["fcb"], state_fconv, nseq=nb, seg=seg, tm=tm, tn=512)
    y = _ffn_down(act, x1, wb["w_down"], prm["n_post2"], tm=_row_tile(t, TM_FFN_DOWN))
    return (y.reshape(nb, seg, D_MODEL),
            st.reshape(nb, SSM_HEADS, SSM_HEAD_DIM, SSM_STATE),
            sc,
            jnp.concatenate([fst_g, fst_u], axis=-1),
            cv.reshape(nb, seg, GATE_HEADS, GATE_HEAD_DIM))


def kernel(x_prompt, x_sample, state_ssm, state_ssm_conv, state_ffn_conv, norm_mix_pre, w_in, gate_ln_g, gate_ln_b,
           gate_w_s, gate_b_s, ssm_conv_w, ssm_conv_b, ssm_dt_bias, ssm_a_log, ssm_d, ssm_norm_w, w_out,
           norm_mix_post, norm_ffn_pre, ffn_w_up, ffn_conv_w, ffn_conv_b, ffn_w_down, norm_ffn_post):
    depth = w_in.shape[0]
    yp, ys = x_prompt, x_sample
    outs_p, outs_s = [], []
    for l in range(depth):
        prm = _prep_params(norm_mix_pre[l], w_in[l], gate_ln_g[l], gate_ln_b[l], gate_w_s[l], gate_b_s[l],
                           ssm_conv_w[l], ssm_conv_b[l], ssm_dt_bias[l], ssm_a_log[l], ssm_d[l], ssm_norm_w[l],
                           w_out[l], norm_mix_post[l], norm_ffn_pre[l], ffn_w_up[l], ffn_conv_w[l], ffn_conv_b[l],
                           ffn_w_down[l], norm_ffn_post[l], x_sample.shape[1])
        wb, (yp, *rest_p) = _layer_prompt(yp, prm)
        ys, *rest_s = _layer_sample(ys, state_ssm[l], state_ssm_conv[l], state_ffn_conv[l], prm, wb)
        outs_p.append(rest_p)
        outs_s.append(rest_s)
    stack = lambda outs, k: jnp.stack([o[k] for o in outs])
    return (yp, ys,
            stack(outs_p, 0), stack(outs_p, 1), stack(outs_p, 2), stack(outs_p, 3),
            stack(outs_s, 0), stack(outs_s, 1), stack(outs_s, 2), stack(outs_s, 3))
```

```python
import functools

import jax
import jax.numpy as jnp
import numpy as np
from jax import lax
from jax.experimental import pallas as pl
from jax.experimental.pallas import tpu as pltpu

F32 = jnp.float32
BF16 = jnp.bfloat16

D_MODEL = 2048
GATE_WIDTH = 2048
GATE_HEADS = 16
GATE_HEAD_DIM = 128
CHUNK = 128
SSM_WIDTH = 2048
SSM_HEAD_DIM = 64
SSM_HEADS = 32
SSM_GROUPS = 4
SSM_STATE = 128
SSM_CONV = 4
SSM_CONV_DIM = SSM_WIDTH + 2 * SSM_GROUPS * SSM_STATE
PROJ_MAIN = 2 * GATE_WIDTH + SSM_WIDTH + SSM_CONV_DIM
D_FF = 5632
FFN_CONV = 3
EPS = 1e-6
HEADS_PER_GROUP = SSM_HEADS // SSM_GROUPS
GROUP_WIDTH = SSM_WIDTH // SSM_GROUPS

LANES = 128
SUBLANES = 8
VMEM_LIMIT = 56 * 1024 * 1024


def _cparams(sem):
    return pltpu.CompilerParams(dimension_semantics=sem, vmem_limit_bytes=VMEM_LIMIT)


def _rms(x, w):
    return x * lax.rsqrt(jnp.mean(x * x, axis=-1, keepdims=True) + EPS) * w


def _gelu_erf(x):
    return 0.5 * x * (1.0 + lax.erf(x * np.float32(0.7071067811865476)))


def _split_bf16(x, n):
    parts = []
    r = x
    for k in range(n):
        p = r.astype(BF16)
        parts.append(p)
        if k + 1 < n:
            r = r - p.astype(F32)
    return parts


def _dot(a, b):
    return jnp.dot(a, b, preferred_element_type=F32)


def _dot_nt(a, b):
    return lax.dot_general(a, b, (((1,), (1,)), ((), ())), preferred_element_type=F32)


def _mm_split(m_bf, x, n):
    acc = None
    for p in _split_bf16(x, n):
        t = _dot(m_bf, p)
        acc = t if acc is None else acc + t
    return acc


def _expand_heads(x, e_bf):
    acc = None
    for p in _split_bf16(x, 2):
        t = _dot(p, e_bf)
        acc = t if acc is None else acc + t
    return acc


def _history_block(p_ref, sq, seg):
    k1, c = p_ref.shape[1], p_ref.shape[2]
    s = sq.stop - sq.start
    t = lax.broadcasted_iota(jnp.int32, (s, seg, c), 1)
    out = jnp.zeros((s, seg, c), F32)
    for k in range(k1):
        out = jnp.where(t == seg - k1 + k, jnp.broadcast_to(p_ref[sq, k:k + 1, :], (s, seg, c)), out)
    return out


def _seg_masks(rows, seg):
    r = lax.broadcasted_iota(jnp.int32, (rows, rows), 0)
    c = lax.broadcasted_iota(jnp.int32, (rows, rows), 1)
    same = (r // seg) == (c // seg)
    return same & (c <= r), same


BF16_ROWS = 2 * SUBLANES


def _cast_rows(rows, nsteps):
    per = BF16_ROWS
    while rows % per or rows // per > nsteps:
        per += BF16_ROWS
        assert per <= rows
    return per


def _side_cast_specs(weights, nsteps, step_of):
    ins, outs, shapes = [], [], []
    for w in weights:
        rows, cols = w.shape
        per = _cast_rows(rows, nsteps)
        spec = pl.BlockSpec((per, cols), lambda *g, nb=rows // per: (jnp.minimum(step_of(*g), nb - 1), 0))
        ins.append(spec)
        outs.append(spec)
        shapes.append(jax.ShapeDtypeStruct((rows, cols), BF16))
    return ins, outs, shapes


def _side_cast(cast_in, cast_out):
    for w_ref, o_ref in zip(cast_in, cast_out):
        o_ref[...] = w_ref[...].astype(BF16)


def _pre_norm_body(x_ref, nw_ref, wdt_ref, h_ref, dt_ref):
    h = _rms(x_ref[...], nw_ref[...]).astype(BF16)
    h_ref[...] = h
    dt_ref[...] = _dot_nt(h, wdt_ref[...])


def _pre_norm(x2d, nw, w_dt, *, tm):
    t = x2d.shape[0]
    assert t % tm == 0
    return pl.pallas_call(
        _pre_norm_body,
        grid=(t // tm,),
        in_specs=[
            pl.BlockSpec((tm, D_MODEL), lambda i: (i, 0)),
            pl.BlockSpec((1, D_MODEL), lambda i: (0, 0)),
            pl.BlockSpec((LANES, D_MODEL), lambda i: (0, 0)),
        ],
        out_specs=[
            pl.BlockSpec((tm, D_MODEL), lambda i: (i, 0)),
            pl.BlockSpec((tm, LANES), lambda i: (i, 0)),
        ],
        out_shape=[
            jax.ShapeDtypeStruct((t, D_MODEL), BF16),
            jax.ShapeDtypeStruct((t, LANES), F32),
        ],
        compiler_params=_cparams(("arbitrary",)),
        name="pre_norm",
    )(x2d, nw, w_dt)


def _in_proj_body(h_ref, w_ref, *rest, n_gelu, sub, n_cast, cast_w):
    cast_in, o_ref, cast_out, wb_s = rest[:n_cast], rest[n_cast], rest[n_cast + 1:2 * n_cast + 1], rest[-1]
    j = pl.program_id(0)
    i = pl.program_id(1)
    tm = h_ref.shape[0]
    _side_cast(cast_in, cast_out)

    @pl.when(i == 0)
    def _():
        wb_s[...] = w_ref[...].astype(BF16)
        if cast_w:
            rest[-2][...] = wb_s[...]

    def run(epilogue):
        for rs in _row_subs(tm, sub):
            o_ref[rs, :] = epilogue(_dot_nt(h_ref[rs, :], wb_s[...])).astype(o_ref.dtype)

    @pl.when(j < n_gelu)
    def _():
        run(_gelu_erf)

    @pl.when(j >= n_gelu)
    def _():
        run(lambda a: a)


def _in_proj(h, w_in_t, *, tm, tn, sub, cast=()):
    t = h.shape[0]
    assert t % tm == 0 and PROJ_MAIN % tn == 0 and (2 * GATE_WIDTH) % tn == 0 and tm % sub == 0
    nj, ni = PROJ_MAIN // tn, t // tm
    c_in, c_out, c_shapes = _side_cast_specs(cast, nj * ni, lambda j, i: j * ni + i)
    w_spec = pl.BlockSpec((tn, D_MODEL), lambda j, i: (j, 0))
    cast_w = w_in_t.dtype != BF16
    return pl.pallas_call(
        functools.partial(_in_proj_body, n_gelu=2 * GATE_WIDTH // tn, sub=sub, n_cast=len(cast), cast_w=cast_w),
        grid=(nj, ni),
        in_specs=[pl.BlockSpec((tm, D_MODEL), lambda j, i: (i, 0)), w_spec] + c_in,
        out_specs=[pl.BlockSpec((tm, tn), lambda j, i: (i, j))] + c_out + [w_spec] * cast_w,
        out_shape=([jax.ShapeDtypeStruct((t, PROJ_MAIN), BF16)] + c_shapes
                   + [jax.ShapeDtypeStruct((PROJ_MAIN, D_MODEL), BF16)] * cast_w),
        scratch_shapes=[pltpu.VMEM((tn, D_MODEL), BF16)],
        compiler_params=_cparams(("arbitrary", "arbitrary")),
        name="in_proj",
    )(h, w_in_t, *cast)


def _gate_tile(gu_ref, gv_ref, lng_ref, lnb_ref, wm_s, btT_ref, mix_ref):
    g = gv_ref[...].astype(F32)
    mu = jnp.mean(g, axis=-1, keepdims=True)
    xc = g - mu
    v = xc * lax.rsqrt(jnp.mean(xc * xc, axis=-1, keepdims=True) + EPS) * lng_ref[...] + lnb_ref[...]
    vb = v.astype(BF16)
    rows = v.shape[0]
    for h in range(GATE_HEADS):
        sl = slice(h * GATE_HEAD_DIM, (h + 1) * GATE_HEAD_DIM)
        s = _dot(wm_s[h], vb[:, sl]) + jnp.broadcast_to(btT_ref[:, h:h + 1], (rows, GATE_HEAD_DIM))
        mix_ref[:, sl] = (gu_ref[:, sl].astype(F32) * s).astype(mix_ref.dtype)
    return v


def _ssd_token_level(act, dt_raw, dtb_ref, alog_ref, maskf, segf):
    xs = act[:, :SSM_WIDTH]
    bm = act[:, SSM_WIDTH:SSM_WIDTH + GROUP_WIDTH]
    cm = act[:, SSM_WIDTH + GROUP_WIDTH:]
    dt = jax.nn.softplus(dt_raw + dtb_ref[...])
    a = -jnp.exp(alog_ref[...])
    da = dt * a
    cs = _mm_split(maskf, da, 3)
    cl = _mm_split(segf, da, 3)
    return xs, bm, cm, dt, cs, cl


def _ssd_diag_pair(cb, cs, cs_t, dt_t, mask, h):
    seg = cs[:, h:h + 1] - cs_t[h:h + 1, :]
    decay = jnp.exp(jnp.where(mask, seg, -jnp.inf))
    return cb * decay * dt_t[h:h + 1, :]


def _ssd_finish(y, xs, z_ref, dexp_ref, nwm_ref, mix_ref):
    y = y + dexp_ref[...] * xs
    y = y * jax.nn.silu(z_ref[...].astype(F32))
    for g in range(SSM_GROUPS):
        sl = slice(g * GROUP_WIDTH, (g + 1) * GROUP_WIDTH)
        yg = y[:, sl]
        yg = yg * lax.rsqrt(jnp.mean(yg * yg, axis=-1, keepdims=True) + EPS) * nwm_ref[:, sl]
        mix_ref[:, GATE_WIDTH + g * GROUP_WIDTH:GATE_WIDTH + (g + 1) * GROUP_WIDTH] = yg.astype(mix_ref.dtype)


def _mixer_prompt_body(gu_ref, gv_ref, z_ref, xbc_ref, dt_ref, lng_ref, lnb_ref, wt_ref, btT_ref, cw_ref, cb_ref,
                       dtb_ref, alog_ref, dexp_ref, nwm_ref, e_ref,
                       mix_ref, cv_ref, st_ref, sc_ref,
                       wm_s, shift_s, xx_s, st_s, y_s):
    b = pl.program_id(0)
    c = pl.program_id(1)
    last = c == pl.num_programs(1) - 1
    rows = CHUNK
    keep = 2 * SUBLANES
    mask, same = _seg_masks(rows, rows)

    @pl.when((b == 0) & (c == 0))
    def _():
        for h in range(GATE_HEADS):
            wm_s[h] = jnp.where(mask, wt_ref[h], 0.0).astype(BF16)
        r = lax.broadcasted_iota(jnp.int32, shift_s.shape, 0)
        col = lax.broadcasted_iota(jnp.int32, shift_s.shape, 1)
        shift_s[...] = jnp.where(col == rows + r % rows - (r // rows + 1), 1.0, 0.0).astype(BF16)
        xx_s[0:rows, :] = jnp.zeros((rows, SSM_CONV_DIM), BF16)

    @pl.when(c == 0)
    def _():
        xx_s[rows - keep:rows, :] = jnp.zeros((keep, SSM_CONV_DIM), BF16)
        st_s[...] = jnp.zeros(st_s.shape, F32)

    v = _gate_tile(gu_ref, gv_ref, lng_ref, lnb_ref, wm_s, btT_ref, mix_ref)

    @pl.when(last)
    def _():
        cv_ref[0] = v

    xx_s[rows:2 * rows, :] = xbc_ref[...]
    sh = _dot(shift_s[...], xx_s[...])
    conv = cb_ref[...] + cw_ref[SSM_CONV - 1:SSM_CONV, :] * xbc_ref[...].astype(F32)
    for d in range(1, SSM_CONV):
        conv = conv + cw_ref[SSM_CONV - 1 - d:SSM_CONV - d, :] * sh[(d - 1) * rows:d * rows, :]
    act = jax.nn.silu(conv)

    @pl.when(last)
    def _():
        sc_ref[0] = xbc_ref[rows - (SSM_CONV - 1):rows, :].astype(F32)

    xx_s[rows - keep:rows, :] = xx_s[2 * rows - keep:2 * rows, :]

    maskf = mask.astype(BF16)
    segf = same.astype(BF16)
    xs, bm, cm, dt, cs, cl = _ssd_token_level(act, dt_ref[...], dtb_ref, alog_ref, maskf, segf)
    cs_t = cs.T
    dt_t = dt.T
    ecs = jnp.exp(cs)
    e_bf = e_ref[...]
    coef_x = _expand_heads(dt * jnp.exp(cl - cs), e_bf)
    dlast_x = _expand_heads(jnp.exp(cl[0:SUBLANES, :]), e_bf)[0:1, :]
    lane = lax.broadcasted_iota(jnp.int32, (rows, LANES), 1)
    xs_b = xs.astype(BF16)
    for g in range(SSM_GROUPS):
        cg = cm[:, g * SSM_STATE:(g + 1) * SSM_STATE]
        bg = bm[:, g * SSM_STATE:(g + 1) * SSM_STATE]
        cb = lax.dot_general(cg.astype(BF16), bg.astype(BF16), (((1,), (1,)), ((), ())), preferred_element_type=F32)
        for p in range(HEADS_PER_GROUP // 2):
            h0 = g * HEADS_PER_GROUP + 2 * p
            sl = slice((h0 // 2) * LANES, (h0 // 2 + 1) * LANES)
            rhs = jnp.concatenate([xs_b[:, sl], st_s[:, sl].astype(BF16)], axis=0)
            ys = []
            for h in (h0, h0 + 1):
                m_h = _ssd_diag_pair(cb, cs, cs_t, dt_t, mask, h)
                c_h = cg * jnp.broadcast_to(ecs[:, h:h + 1], (rows, SSM_STATE))
                lhs = jnp.concatenate([m_h.astype(BF16), c_h.astype(BF16)], axis=1)
                ys.append(_dot(lhs, rhs))
            y_s[:, sl] = jnp.where(lane < SSM_HEAD_DIM, ys[0], ys[1])
    wc = (xs * coef_x).astype(BF16)
    for g in range(SSM_GROUPS):
        sl = slice(g * GROUP_WIDTH, (g + 1) * GROUP_WIDTH)
        bg = bm[:, g * SSM_STATE:(g + 1) * SSM_STATE].astype(BF16)
        upd = lax.dot_general(bg, wc[:, sl], (((0,), (0,)), ((), ())), preferred_element_type=F32)
        st_s[:, sl] = st_s[:, sl] * dlast_x[:, sl] + upd

    @pl.when(last)
    def _():
        st_ref[0] = st_s[...].T

    _ssd_finish(y_s[...], xs, z_ref, dexp_ref, nwm_ref, mix_ref)


def _mixer_prompt(proj, dt, prm, *, nb, nc):
    rows = CHUNK
    t = nb * nc * rows
    row = lambda b, c: b * nc + c
    full = lambda shape: pl.BlockSpec(shape, lambda b, c: (0,) * len(shape))
    return pl.pallas_call(
        _mixer_prompt_body,
        grid=(nb, nc),
        in_specs=[
            pl.BlockSpec((rows, GATE_WIDTH), lambda b, c: (row(b, c), 0)),
            pl.BlockSpec((rows, GATE_WIDTH), lambda b, c: (row(b, c), 1)),
            pl.BlockSpec((rows, SSM_WIDTH), lambda b, c: (row(b, c), 2)),
            pl.BlockSpec((rows, SSM_CONV_DIM), lambda b, c: (row(b, c), 2)),
            pl.BlockSpec((rows, LANES), lambda b, c: (row(b, c), 0)),
            full((1, GATE_WIDTH)), full((1, GATE_WIDTH)),
            full((GATE_HEADS, rows, rows)), full((rows, GATE_HEADS)),
            full((SSM_CONV, SSM_CONV_DIM)), full((1, SSM_CONV_DIM)),
            full((1, LANES)), full((1, LANES)), full((1, SSM_WIDTH)), full((1, SSM_WIDTH)),
            full((LANES, SSM_WIDTH)),
        ],
        out_specs=[
            pl.BlockSpec((rows, 2 * GATE_WIDTH), lambda b, c: (row(b, c), 0)),
            pl.BlockSpec((1, rows, GATE_WIDTH), lambda b, c: (b, 0, 0)),
            pl.BlockSpec((1, SSM_WIDTH, SSM_STATE), lambda b, c: (b, 0, 0)),
            pl.BlockSpec((1, SSM_CONV - 1, SSM_CONV_DIM), lambda b, c: (b, 0, 0)),
        ],
        out_shape=[
            jax.ShapeDtypeStruct((t, 2 * GATE_WIDTH), BF16),
            jax.ShapeDtypeStruct((nb, rows, GATE_WIDTH), F32),
            jax.ShapeDtypeStruct((nb, SSM_WIDTH, SSM_STATE), F32),
            jax.ShapeDtypeStruct((nb, SSM_CONV - 1, SSM_CONV_DIM), F32),
        ],
        scratch_shapes=[
            pltpu.VMEM((GATE_HEADS, rows, rows), BF16),
            pltpu.VMEM(((SSM_CONV - 1) * rows, 2 * rows), BF16),
            pltpu.VMEM((2 * rows, SSM_CONV_DIM), BF16),
            pltpu.VMEM((SSM_STATE, SSM_WIDTH), F32),
            pltpu.VMEM((rows, SSM_WIDTH), F32),
        ],
        compiler_params=_cparams(("arbitrary", "arbitrary")),
        name="mixer_prompt",
    )(proj, proj, proj, proj, dt, prm["ln_g"], prm["ln_b"], prm["wt_p"], prm["btT_p"], prm["conv_w"], prm["conv_b"],
      prm["dtb"], prm["alog"], prm["dexp"], prm["nwm"], prm["e"])


SEQ_TILE = 8


def _state_decay_body(dt_ref, dtb_ref, alog_ref, o_ref, *, seg):
    nseq = o_ref.shape[0]
    a = -jnp.exp(alog_ref[...])
    tot = jnp.zeros(o_ref.shape, F32)
    for t in range(seg):
        d = jax.nn.softplus(dt_ref[pl.ds(t, nseq, stride=seg), :] + dtb_ref[...])
        tot = tot + d * a
    o_ref[...] = jnp.exp(tot)


def _state_decay(dt, prm, *, seg):
    nseq = dt.shape[0] // seg
    return pl.pallas_call(
        functools.partial(_state_decay_body, seg=seg),
        out_shape=jax.ShapeDtypeStruct((nseq, LANES), F32),
        name="state_decay",
    )(dt, prm["dtb"], prm["alog"])


def _mixer_sample_body(dec_ref, gu_ref, gv_ref, z_ref, xbc_ref, dt_ref, prev_ref, sin_ref,
                       lng_ref, lnb_ref, wt_ref, btT_ref, cw_ref, cb_ref,
                       dtb_ref, alog_ref, dexp_ref, nwm_ref, e_ref,
                       mix_ref, cv_ref, sout_ref, sc_ref,
                       wm_s, y_s, cm_s, bm_s, wct_s, ex_s, *, seg):
    i = pl.program_id(0)
    rows = SEQ_TILE * seg
    mask, same = _seg_masks(rows, seg)

    @pl.when(i == 0)
    def _():
        for h in range(GATE_HEADS):
            wm_s[h] = jnp.where(mask, wt_ref[h], 0.0).astype(BF16)

    cv_ref[...] = _gate_tile(gu_ref, gv_ref, lng_ref, lnb_ref, wm_s, btT_ref, mix_ref)

    x3 = xbc_ref[...].astype(F32).reshape(SEQ_TILE, seg, SSM_CONV_DIM)
    p3 = _history_block(prev_ref, slice(0, SEQ_TILE), seg)
    tpos = lax.broadcasted_iota(jnp.int32, x3.shape, 1)
    conv = cb_ref[...] + cw_ref[SSM_CONV - 1:SSM_CONV, :] * x3
    for d in range(1, SSM_CONV):
        shifted = jnp.where(tpos >= d, pltpu.roll(x3, d, axis=1), pltpu.roll(p3, d, axis=1))
        conv = conv + cw_ref[SSM_CONV - 1 - d:SSM_CONV - d, :] * shifted
    act = jax.nn.silu(conv).reshape(rows, SSM_CONV_DIM)
    sc_ref[...] = x3[:, seg - (SSM_CONV - 1):, :]

    maskf = mask.astype(BF16)
    segf = same.astype(BF16)
    xs, bm, cm, dt, cs, cl = _ssd_token_level(act, dt_ref[...], dtb_ref, alog_ref, maskf, segf)
    cs_t = cs.T
    dt_t = dt.T
    e_bf = e_ref[...]
    coef_x = _expand_heads(dt * jnp.exp(cl - cs), e_bf)
    ecs_x = _expand_heads(jnp.exp(cs), e_bf)
    lane = lax.broadcasted_iota(jnp.int32, (rows, LANES), 1)
    xs_b = xs.astype(BF16)
    for g in range(SSM_GROUPS):
        cg = cm[:, g * SSM_STATE:(g + 1) * SSM_STATE]
        bg = bm[:, g * SSM_STATE:(g + 1) * SSM_STATE]
        cb = lax.dot_general(cg.astype(BF16), bg.astype(BF16), (((1,), (1,)), ((), ())), preferred_element_type=F32)
        for p in range(HEADS_PER_GROUP // 2):
            h0 = g * HEADS_PER_GROUP + 2 * p
            sl = slice((h0 // 2) * LANES, (h0 // 2 + 1) * LANES)
            ys = [_dot(_ssd_diag_pair(cb, cs, cs_t, dt_t, mask, h).astype(BF16), xs_b[:, sl]) for h in (h0, h0 + 1)]
            y_s[:, sl] = jnp.where(lane < SSM_HEAD_DIM, ys[0], ys[1])
    cm_s[...] = cm
    bm_s[...] = bm
    wct_s[...] = (xs * coef_x).T
    ex_s[...] = ecs_x

    rowid = lax.broadcasted_iota(jnp.int32, (rows, SSM_STATE), 0)

    for s in range(SEQ_TILE):
        r8 = slice(s * seg, (s + 1) * seg)
        for g in range(SSM_GROUPS):
            gsl = slice(g * GROUP_WIDTH, (g + 1) * GROUP_WIDTH)
            nsl = slice(g * SSM_STATE, (g + 1) * SSM_STATE)
            st = sin_ref[s, gsl, :]
            c8 = cm_s[r8, nsl].astype(BF16)
            yo = lax.dot_general(c8, st.astype(BF16), (((1,), (1,)), ((), ())), preferred_element_type=F32)
            y_s[r8, gsl] = y_s[r8, gsl] + yo * ex_s[r8, gsl]
            bmask = jnp.where(rowid // seg == s, bm_s[:, nsl], 0.0).astype(BF16)
            upd = _dot(wct_s[gsl, :].astype(BF16), bmask)
            for r in range(HEADS_PER_GROUP):
                d = dec_ref[(i * SEQ_TILE + s) * SSM_HEADS + g * HEADS_PER_GROUP + r]
                hsl = slice(r * SSM_HEAD_DIM, (r + 1) * SSM_HEAD_DIM)
                osl = slice(g * GROUP_WIDTH + r * SSM_HEAD_DIM, g * GROUP_WIDTH + (r + 1) * SSM_HEAD_DIM)
                sout_ref[s, osl, :] = st[hsl, :] * d + upd[hsl, :]

    _ssd_finish(y_s[...], xs, z_ref, dexp_ref, nwm_ref, mix_ref)


def _mixer_sample(dec, proj, dt, prev, state, prm, *, seg):
    rows = SEQ_TILE * seg
    t = proj.shape[0]
    nseq = t // seg
    assert t % rows == 0
    full = lambda shape: pl.BlockSpec(shape, lambda i: (0,) * len(shape))
    return pl.pallas_call(
        functools.partial(_mixer_sample_body, seg=seg),
        grid=(t // rows,),
        in_specs=[
            pl.BlockSpec(memory_space=pltpu.SMEM),
            pl.BlockSpec((rows, GATE_WIDTH), lambda i: (i, 0)),
            pl.BlockSpec((rows, GATE_WIDTH), lambda i: (i, 1)),
            pl.BlockSpec((rows, SSM_WIDTH), lambda i: (i, 2)),
            pl.BlockSpec((rows, SSM_CONV_DIM), lambda i: (i, 2)),
            pl.BlockSpec((rows, LANES), lambda i: (i, 0)),
            pl.BlockSpec((SEQ_TILE, SSM_CONV - 1, SSM_CONV_DIM), lambda i: (i, 0, 0)),
            pl.BlockSpec((SEQ_TILE, SSM_WIDTH, SSM_STATE), lambda i: (i, 0, 0)),
            full((1, GATE_WIDTH)), full((1, GATE_WIDTH)),
            full((GATE_HEADS, rows, rows)), full((rows, GATE_HEADS)),
            full((SSM_CONV, SSM_CONV_DIM)), full((1, SSM_CONV_DIM)),
            full((1, LANES)), full((1, LANES)), full((1, SSM_WIDTH)), full((1, SSM_WIDTH)),
            full((LANES, SSM_WIDTH)),
        ],
        out_specs=[
            pl.BlockSpec((rows, 2 * GATE_WIDTH), lambda i: (i, 0)),
            pl.BlockSpec((rows, GATE_WIDTH), lambda i: (i, 0)),
            pl.BlockSpec((SEQ_TILE, SSM_WIDTH, SSM_STATE), lambda i: (i, 0, 0)),
            pl.BlockSpec((SEQ_TILE, SSM_CONV - 1, SSM_CONV_DIM), lambda i: (i, 0, 0)),
        ],
        out_shape=[
            jax.ShapeDtypeStruct((t, 2 * GATE_WIDTH), BF16),
            jax.ShapeDtypeStruct((t, GATE_WIDTH), F32),
            jax.ShapeDtypeStruct((nseq, SSM_WIDTH, SSM_STATE), F32),
            jax.ShapeDtypeStruct((nseq, SSM_CONV - 1, SSM_CONV_DIM), F32),
        ],
        scratch_shapes=[
            pltpu.VMEM((GATE_HEADS, rows, rows), BF16),
            pltpu.VMEM((rows, SSM_WIDTH), F32),
            pltpu.VMEM((rows, GROUP_WIDTH), F32),
            pltpu.VMEM((rows, GROUP_WIDTH), F32),
            pltpu.VMEM((SSM_WIDTH, rows), F32),
            pltpu.VMEM((rows, SSM_WIDTH), F32),
        ],
        compiler_params=_cparams(("arbitrary",)),
        name="mixer_sample",
    )(dec, proj, proj, proj, proj, dt, prev, state, prm["ln_g"], prm["ln_b"], prm["wt_s"], prm["btT_s"],
      prm["conv_w"], prm["conv_b"], prm["dtb"], prm["alog"], prm["dexp"], prm["nwm"], prm["e"])


ROW_SUB = 128


def _row_subs(tm, sub=ROW_SUB):
    sub = min(sub, tm)
    assert tm % sub == 0
    return [slice(r * sub, (r + 1) * sub) for r in range(tm // sub)]


def _out_proj_body(m_ref, x_ref, w_ref, npost_ref, npre_ref, x1_ref, h2_ref):
    for rs in _row_subs(m_ref.shape[0]):
        mix = _dot(m_ref[rs, :], w_ref[...])
        x1 = x_ref[rs, :] + _rms(mix, npost_ref[...])
        x1_ref[rs, :] = x1
        h2_ref[rs, :] = _rms(x1, npre_ref[...]).astype(h2_ref.dtype)


def _out_proj(mixin, x2d, w_out, npost, npre, *, tm):
    t, k = mixin.shape
    assert t % tm == 0
    return pl.pallas_call(
        _out_proj_body,
        grid=(t // tm,),
        in_specs=[
            pl.BlockSpec((tm, k), lambda i: (i, 0)),
            pl.BlockSpec((tm, D_MODEL), lambda i: (i, 0)),
            pl.BlockSpec((k, D_MODEL), lambda i: (0, 0), pipeline_mode=pl.Buffered(1)),
            pl.BlockSpec((1, D_MODEL), lambda i: (0, 0)),
            pl.BlockSpec((1, D_MODEL), lambda i: (0, 0)),
        ],
        out_specs=[
            pl.BlockSpec((tm, D_MODEL), lambda i: (i, 0)),
            pl.BlockSpec((tm, D_MODEL), lambda i: (i, 0)),
        ],
        out_shape=[
            jax.ShapeDtypeStruct((t, D_MODEL), F32),
            jax.ShapeDtypeStruct((t, D_MODEL), BF16),
        ],
        compiler_params=_cparams(("arbitrary",)),
        name="out_proj",
    )(mixin, x2d, w_out, npost, npre)


def _ffn_conv_taps(cur, shifted, cw_ref, cb_ref):
    out = cb_ref[...] + cw_ref[FFN_CONV - 1:FFN_CONV, :] * cur
    for d in range(1, FFN_CONV):
        out = out + cw_ref[FFN_CONV - 1 - d:FFN_CONV - d, :] * shifted[d - 1]
    return out


def _ffn_up_prompt_body(h_ref, wg_ref, wu_ref, cwg_ref, cwu_ref, cbg_ref, cbu_ref, *rest, tiles_per_seq, n_cast):
    cast_in, (a_ref, stg_ref, stu_ref) = rest[:n_cast], rest[n_cast:n_cast + 3]
    cast_out, (tg_s, tu_s) = rest[n_cast + 3:2 * n_cast + 3], rest[2 * n_cast + 3:]
    i = pl.program_id(1)
    tm = h_ref.shape[0]
    _side_cast(cast_in, cast_out)

    @pl.when((i % tiles_per_seq) == 0)
    def _():
        for t_s in (tg_s, tu_s):
            t_s[...] = jnp.zeros(t_s.shape, F32)

    tails = [tg_s[...], tu_s[...]]
    row8 = lax.broadcasted_iota(jnp.int32, tg_s.shape, 0)
    for rs in _row_subs(tm, 128):
        h = h_ref[rs, :]
        convs = []
        for k, (w_ref, cw_ref, cb_ref) in enumerate(((wg_ref, cwg_ref, cbg_ref), (wu_ref, cwu_ref, cbu_ref))):
            x = _dot(h, w_ref[...])
            shifted = []
            for d in range(1, FFN_CONV):
                r = pltpu.roll(x, d, axis=0)
                head = jnp.where(row8 >= d, r[0:SUBLANES], pltpu.roll(tails[k], d, axis=0))
                shifted.append(jnp.concatenate([head, r[SUBLANES:]], axis=0))
            convs.append(_ffn_conv_taps(x, shifted, cw_ref, cb_ref))
            tails[k] = x[x.shape[0] - SUBLANES:]
        a_ref[rs, :] = (jax.nn.gelu(convs[0], approximate=True) * convs[1]).astype(a_ref.dtype)
    for tail, t_s, st_ref in zip(tails, (tg_s, tu_s), (stg_ref, stu_ref)):
        t_s[...] = tail
        st_ref[0] = tail[SUBLANES - (FFN_CONV - 1):]


def _ffn_up_sample_body(h_ref, wg_ref, wu_ref, cwg_ref, cwu_ref, cbg_ref, cbu_ref, pg_ref, pu_ref,
                        a_ref, stg_ref, stu_ref, *, seg):
    tm = h_ref.shape[0]
    tn = wg_ref.shape[1]
    for rs in _row_subs(tm):
        n = rs.stop - rs.start
        nseq = n // seg
        sq = slice(rs.start // seg, rs.stop // seg)
        h = h_ref[rs, :]
        tpos = lax.broadcasted_iota(jnp.int32, (nseq, seg, tn), 1)
        convs = []
        for w_ref, p_ref, cw_ref, cb_ref, st_ref in ((wg_ref, pg_ref, cwg_ref, cbg_ref, stg_ref),
                                                     (wu_ref, pu_ref, cwu_ref, cbu_ref, stu_ref)):
            x3 = _dot(h, w_ref[...]).reshape(nseq, seg, tn)
            p3 = _history_block(p_ref, sq, seg)
            shifted = [jnp.where(tpos >= d, pltpu.roll(x3, d, axis=1), pltpu.roll(p3, d, axis=1))
                       for d in range(1, FFN_CONV)]
            convs.append(_ffn_conv_taps(x3, shifted, cw_ref, cb_ref))
            st_ref[sq] = x3[:, seg - (FFN_CONV - 1):, :]
        act = jax.nn.gelu(convs[0], approximate=True) * convs[1]
        a_ref[rs, :] = act.reshape(n, tn).astype(a_ref.dtype)


def _ffn_up(h2, w_up, cw, cb, prev, *, nseq, seg, tm, tn, cast=()):
    t = h2.shape[0]
    nj, ni = D_FF // tn, t // tm
    assert t % tm == 0 and D_FF % tn == 0 and t == nseq * seg
    common_in = [
        pl.BlockSpec((tm, D_MODEL), lambda j, i: (i, 0)),
        pl.BlockSpec((D_MODEL, tn), lambda j, i: (0, j)),
        pl.BlockSpec((D_MODEL, tn), lambda j, i: (0, j + nj)),
        pl.BlockSpec((FFN_CONV, tn), lambda j, i: (0, j)),
        pl.BlockSpec((FFN_CONV, tn), lambda j, i: (0, j + nj)),
        pl.BlockSpec((1, tn), lambda j, i: (0, j)),
        pl.BlockSpec((1, tn), lambda j, i: (0, j + nj)),
    ]
    a_spec = pl.BlockSpec((tm, tn), lambda j, i: (i, j))
    st_shape = jax.ShapeDtypeStruct((nseq, FFN_CONV - 1, D_FF), F32)
    out_shape = [jax.ShapeDtypeStruct((t, D_FF), BF16), st_shape, st_shape]
    if prev is None:
        assert seg % tm == 0
        tps = seg // tm
        c_in, c_out, c_shapes = _side_cast_specs(cast, nj * ni, lambda j, i: j * ni + i)
        return pl.pallas_call(
            functools.partial(_ffn_up_prompt_body, tiles_per_seq=tps, n_cast=len(cast)),
            grid=(nj, ni),
            in_specs=common_in + c_in,
            out_specs=[a_spec] + [pl.BlockSpec((1, FFN_CONV - 1, tn), lambda j, i: (i // tps, 0, j))] * 2 + c_out,
            out_shape=out_shape + c_shapes,
            scratch_shapes=[pltpu.VMEM((SUBLANES, tn), F32), pltpu.VMEM((SUBLANES, tn), F32)],
            compiler_params=_cparams(("arbitrary", "arbitrary")),
            name="ffn_up_prompt",
        )(h2, w_up, w_up, cw, cw, cb, cb, *cast)
    assert tm % seg == 0 and seg == SUBLANES and not cast
    return pl.pallas_call(
        functools.partial(_ffn_up_sample_body, seg=seg),
        grid=(nj, ni),
        in_specs=common_in + [
            pl.BlockSpec((tm // seg, FFN_CONV - 1, tn), lambda j, i: (i, 0, j)),
            pl.BlockSpec((tm // seg, FFN_CONV - 1, tn), lambda j, i: (i, 0, j + nj)),
        ],
        out_specs=[a_spec] + [pl.BlockSpec((tm // seg, FFN_CONV - 1, tn), lambda j, i: (i, 0, j))] * 2,
        out_shape=out_shape,
        compiler_params=_cparams(("arbitrary", "arbitrary")),
        name="ffn_up_sample",
    )(h2, w_up, w_up, cw, cw, cb, cb, prev, prev)


def _ffn_down_body(a_ref, x_ref, w_ref, nw_ref, y_ref):
    for rs in _row_subs(a_ref.shape[0], 128):
        f = _dot(a_ref[rs, :], w_ref[...])
        y_ref[rs, :] = x_ref[rs, :] + _rms(f, nw_ref[...])


def _ffn_down(act, x1, w_down, nw, *, tm):
    t, k = act.shape
    assert t % tm == 0
    return pl.pallas_call(
        _ffn_down_body,
        grid=(t // tm,),
        in_specs=[
            pl.BlockSpec((tm, k), lambda i: (i, 0)),
            pl.BlockSpec((tm, D_MODEL), lambda i: (i, 0)),
            pl.BlockSpec((k, D_MODEL), lambda i: (0, 0), pipeline_mode=pl.Buffered(1)),
            pl.BlockSpec((1, D_MODEL), lambda i: (0, 0)),
        ],
        out_specs=pl.BlockSpec((tm, D_MODEL), lambda i: (i, 0)),
        out_shape=jax.ShapeDtypeStruct((t, D_MODEL), F32),
        compiler_params=_cparams(("arbitrary",)),
        name="ffn_down",
    )(act, x1, w_down, nw)


def _head_expander():
    e = np.zeros((LANES, SSM_WIDTH), np.float32)
    for h in range(SSM_HEADS):
        e[h, h * SSM_HEAD_DIM:(h + 1) * SSM_HEAD_DIM] = 1.0
    return jnp.asarray(e, BF16)


def _pad_lanes(v):
    return jnp.pad(v, (0, LANES - v.shape[0]))[None, :]


def _prep_params(norm_mix_pre, w_in, gate_ln_g, gate_ln_b, gate_w_s, gate_b_s, ssm_conv_w, ssm_conv_b, ssm_dt_bias,
                 ssm_a_log, ssm_d, ssm_norm_w, w_out, norm_mix_post, norm_ffn_pre, ffn_w_up, ffn_conv_w, ffn_conv_b,
                 ffn_w_down, norm_ffn_post, seg_sample):
    w_in_t = jnp.swapaxes(w_in, 0, 1)
    ws_small = gate_w_s[:, :seg_sample, :seg_sample]
    return dict(
        nw_pre=norm_mix_pre[None, :],
        w_in_t=w_in_t,
        w_dt=jnp.pad(w_in_t[PROJ_MAIN:, :], ((0, LANES - SSM_HEADS), (0, 0))).astype(BF16),
        ln_g=gate_ln_g[None, :], ln_b=gate_ln_b[None, :],
        wt_p=gate_w_s, btT_p=gate_b_s.T,
        wt_s=jnp.tile(ws_small, (1, SEQ_TILE, SEQ_TILE)), btT_s=jnp.tile(gate_b_s[:, :seg_sample], (1, SEQ_TILE)).T,
        conv_w=ssm_conv_w, conv_b=ssm_conv_b[None, :],
        dtb=_pad_lanes(ssm_dt_bias), alog=_pad_lanes(ssm_a_log),
        dexp=jnp.repeat(ssm_d, SSM_HEAD_DIM)[None, :], nwm=ssm_norm_w[None, :],
        e=_head_expander(),
        w_out=w_out, n_post=norm_mix_post[None, :], n_pre2=norm_ffn_pre[None, :],
        w_up=ffn_w_up, fcw=ffn_conv_w, fcb=ffn_conv_b[None, :],
        w_down=ffn_w_down, n_post2=norm_ffn_post[None, :],
    )


def _row_tile(t, cap):
    tm = cap
    while t % tm:
        tm //= 2
    assert tm >= 64
    return tm


TM_STREAM = 1024
TM_OUT_PROJ = 512
TM_FFN_DOWN = 512
TM_PRE_NORM = 1024


def _layer_prompt(x, prm):
    nb, seq, _ = x.shape
    assert seq % CHUNK == 0
    nc = seq // CHUNK
    x2d = x.reshape(nb * seq, D_MODEL)
    tm = _row_tile(seq, TM_STREAM)
    h, dt = _pre_norm(x2d, prm["nw_pre"], prm["w_dt"], tm=_row_tile(seq, TM_PRE_NORM))
    proj, w_out, w_up, w_in_b = _in_proj(h, prm["w_in_t"], tm=tm, tn=1024, sub=min(tm, 256),
                                         cast=(prm["w_out"], prm["w_up"]))
    mixin, cv, st, sc = _mixer_prompt(proj, dt, prm, nb=nb, nc=nc)
    x1, h2 = _out_proj(mixin, x2d, w_out, prm["n_post"], prm["n_pre2"], tm=_row_tile(seq, TM_OUT_PROJ))
    act, fst_g, fst_u, w_down = _ffn_up(h2, w_up, prm["fcw"], prm["fcb"], None, nseq=nb, seg=seq, tm=tm, tn=512,
                                        cast=(prm["w_down"],))
    y = _ffn_down(act, x1, w_down, prm["n_post2"], tm=_row_tile(seq, TM_FFN_DOWN))
    return dict(w_in_t=w_in_b, w_out=w_out, w_up=w_up, w_down=w_down), (
            y.reshape(nb, seq, D_MODEL),
            st.reshape(nb, SSM_HEADS, SSM_HEAD_DIM, SSM_STATE),
            sc,
            jnp.concatenate([fst_g, fst_u], axis=-1),
            cv.reshape(nb, CHUNK, GATE_HEADS, GATE_HEAD_DIM))


def _layer_sample(x, state_ssm, state_sconv, state_fconv, prm, wb):
    nb, seg, _ = x.shape
    assert seg == SUBLANES and nb % SEQ_TILE == 0
    t = nb * seg
    x2d = x.reshape(t, D_MODEL)
    tm = _row_tile(t, TM_STREAM)
    h, dt = _pre_norm(x2d, prm["nw_pre"], prm["w_dt"], tm=_row_tile(t, TM_PRE_NORM))
    proj, = _in_proj(h, wb["w_in_t"], tm=tm, tn=1024, sub=min(tm, 256))
    dec = _state_decay(dt, prm, seg=seg)[:, :SSM_HEADS].reshape(nb * SSM_HEADS)
    mixin, cv, st, sc = _mixer_sample(dec, proj, dt, state_sconv,
                                      state_ssm.reshape(nb, SSM_WIDTH, SSM_STATE), prm, seg=seg)
    x1, h2 = _out_proj(mixin, x2d, wb["w_out"], prm["n_post"], prm["n_pre2"], tm=_row_tile(t, TM_OUT_PROJ))
    act, fst_g, fst_u = _ffn_up(h2, wb["w_up"], prm["fcw"], prm["fcb"], state_fconv, nseq=nb, seg=seg, tm=tm, tn=512)
    y = _ffn_down(act, x1, wb["w_down"], prm["n_post2"], tm=_row_tile(t, TM_FFN_DOWN))
    return (y.reshape(nb, seg, D_MODEL),
            st.reshape(nb, SSM_HEADS, SSM_HEAD_DIM, SSM_STATE),
            sc,
            jnp.concatenate([fst_g, fst_u], axis=-1),
            cv.reshape(nb, seg, GATE_HEADS, GATE_HEAD_DIM))


def kernel(x_prompt, x_sample, state_ssm, state_ssm_conv, state_ffn_conv, norm_mix_pre, w_in, gate_ln_g, gate_ln_b,
           gate_w_s, gate_b_s, ssm_conv_w, ssm_conv_b, ssm_dt_bias, ssm_a_log, ssm_d, ssm_norm_w, w_out,
           norm_mix_post, norm_ffn_pre, ffn_w_up, ffn_conv_w, ffn_conv_b, ffn_w_down, norm_ffn_post):
    depth = w_in.shape[0]
    yp, ys = x_prompt, x_sample
    outs_p, outs_s = [], []
    for l in range(depth):
        prm = _prep_params(norm_mix_pre[l], w_in[l], gate_ln_g[l], gate_ln_b[l], gate_w_s[l], gate_b_s[l],
                           ssm_conv_w[l], ssm_conv_b[l], ssm_dt_bias[l], ssm_a_log[l], ssm_d[l], ssm_norm_w[l],
                           w_out[l], norm_mix_post[l], norm_ffn_pre[l], ffn_w_up[l], ffn_conv_w[l], ffn_conv_b[l],
                           ffn_w_down[l], norm_ffn_post[l], x_sample.shape[1])
        wb, (yp, *rest_p) = _layer_prompt(yp, prm)
        ys, *rest_s = _layer_sample(ys, state_ssm[l], state_ssm_conv[l], state_ffn_conv[l], prm, wb)
        outs_p.append(rest_p)
        outs_s.append(rest_s)
    stack = lambda outs, k: jnp.stack([o[k] for o in outs])
    return (yp, ys,
            stack(outs_p, 0), stack(outs_p, 1), stack(outs_p, 2), stack(outs_p, 3),
            stack(outs_s, 0), stack(outs_s, 1), stack(outs_s, 2), stack(outs_s, 3))
```

```python
import functools

import jax
import jax.numpy as jnp
import numpy as np
from jax import lax
from jax.experimental import pallas as pl
from jax.experimental.pallas import tpu as pltpu

F32 = jnp.float32
BF16 = jnp.bfloat16

D_MODEL = 2048
GATE_WIDTH = 2048
GATE_HEADS = 16
GATE_HEAD_DIM = 128
CHUNK = 128
SSM_WIDTH = 2048
SSM_HEAD_DIM = 64
SSM_HEADS = 32
SSM_GROUPS = 4
SSM_STATE = 128
SSM_CONV = 4
SSM_CONV_DIM = SSM_WIDTH + 2 * SSM_GROUPS * SSM_STATE
PROJ_MAIN = 2 * GATE_WIDTH + SSM_WIDTH + SSM_CONV_DIM
D_FF = 5632
FFN_CONV = 3
EPS = 1e-6
HEADS_PER_GROUP = SSM_HEADS // SSM_GROUPS
GROUP_WIDTH = SSM_WIDTH // SSM_GROUPS

LANES = 128
SUBLANES = 8
VMEM_LIMIT = 56 * 1024 * 1024


def _cparams(sem):
    return pltpu.CompilerParams(dimension_semantics=sem, vmem_limit_bytes=VMEM_LIMIT)


def _rms(x, w):
    return x * lax.rsqrt(jnp.mean(x * x, axis=-1, keepdims=True) + EPS) * w


def _gelu_erf(x):
    return 0.5 * x * (1.0 + lax.erf(x * np.float32(0.7071067811865476)))


def _split_bf16(x, n):
    parts = []
    r = x
    for k in range(n):
        p = r.astype(BF16)
        parts.append(p)
        if k + 1 < n:
            r = r - p.astype(F32)
    return parts


def _dot(a, b):
    return jnp.dot(a, b, preferred_element_type=F32)


def _dot_nt(a, b):
    return lax.dot_general(a, b, (((1,), (1,)), ((), ())), preferred_element_type=F32)


def _mm_split(m_bf, x, n):
    acc = None
    for p in _split_bf16(x, n):
        t = _dot(m_bf, p)
        acc = t if acc is None else acc + t
    return acc


def _expand_heads(x, e_bf):
    acc = None
    for p in _split_bf16(x, 2):
        t = _dot(p, e_bf)
        acc = t if acc is None else acc + t
    return acc


def _history_block(p_ref, sq, seg):
    k1, c = p_ref.shape[1], p_ref.shape[2]
    s = sq.stop - sq.start
    t = lax.broadcasted_iota(jnp.int32, (s, seg, c), 1)
    out = jnp.zeros((s, seg, c), F32)
    for k in range(k1):
        out = jnp.where(t == seg - k1 + k, jnp.broadcast_to(p_ref[sq, k:k + 1, :], (s, seg, c)), out)
    return out


def _seg_masks(rows, seg):
    r = lax.broadcasted_iota(jnp.int32, (rows, rows), 0)
    c = lax.broadcasted_iota(jnp.int32, (rows, rows), 1)
    same = (r // seg) == (c // seg)
    return same & (c <= r), same


BF16_ROWS = 2 * SUBLANES


def _cast_rows(rows, nsteps):
    per = BF16_ROWS
    while rows % per or rows // per > nsteps:
        per += BF16_ROWS
        assert per <= rows
    return per


def _side_cast_specs(weights, nsteps, step_of):
    ins, outs, shapes = [], [], []
    for w in weights:
        rows, cols = w.shape
        per = _cast_rows(rows, nsteps)
        spec = pl.BlockSpec((per, cols), lambda *g, nb=rows // per: (jnp.minimum(step_of(*g), nb - 1), 0))
        ins.append(spec)
        outs.append(spec)
        shapes.append(jax.ShapeDtypeStruct((rows, cols), BF16))
    return ins, outs, shapes


def _side_cast(cast_in, cast_out):
    for w_ref, o_ref in zip(cast_in, cast_out):
        o_ref[...] = w_ref[...].astype(BF16)


def _pre_norm_body(x_ref, nw_ref, wdt_ref, h_ref, dt_ref):
    h = _rms(x_ref[...], nw_ref[...]).astype(BF16)
    h_ref[...] = h
    dt_ref[...] = _dot_nt(h, wdt_ref[...])


def _pre_norm(x2d, nw, w_dt, *, tm):
    t = x2d.shape[0]
    assert t % tm == 0
    return pl.pallas_call(
        _pre_norm_body,
        grid=(t // tm,),
        in_specs=[
            pl.BlockSpec((tm, D_MODEL), lambda i: (i, 0)),
            pl.BlockSpec((1, D_MODEL), lambda i: (0, 0)),
            pl.BlockSpec((LANES, D_MODEL), lambda i: (0, 0)),
        ],
        out_specs=[
            pl.BlockSpec((tm, D_MODEL), lambda i: (i, 0)),
            pl.BlockSpec((tm, LANES), lambda i: (i, 0)),
        ],
        out_shape=[
            jax.ShapeDtypeStruct((t, D_MODEL), BF16),
            jax.ShapeDtypeStruct((t, LANES), F32),
        ],
        compiler_params=_cparams(("arbitrary",)),
        name="pre_norm",
    )(x2d, nw, w_dt)


def _in_proj_body(h_ref, w_ref, *rest, n_gelu, sub, n_cast, cast_w):
    cast_in, o_ref, cast_out, wb_s = rest[:n_cast], rest[n_cast], rest[n_cast + 1:2 * n_cast + 1], rest[-1]
    j = pl.program_id(0)
    i = pl.program_id(1)
    tm = h_ref.shape[0]
    _side_cast(cast_in, cast_out)

    @pl.when(i == 0)
    def _():
        wb_s[...] = w_ref[...].astype(BF16)
        if cast_w:
            rest[-2][...] = wb_s[...]

    def run(epilogue):
        for rs in _row_subs(tm, sub):
            o_ref[rs, :] = epilogue(_dot_nt(h_ref[rs, :], wb_s[...])).astype(o_ref.dtype)

    @pl.when(j < n_gelu)
    def _():
        run(_gelu_erf)

    @pl.when(j >= n_gelu)
    def _():
        run(lambda a: a)


def _in_proj(h, w_in_t, *, tm, tn, sub, cast=()):
    t = h.shape[0]
    assert t % tm == 0 and PROJ_MAIN % tn == 0 and (2 * GATE_WIDTH) % tn == 0 and tm % sub == 0
    nj, ni = PROJ_MAIN // tn, t // tm
    c_in, c_out, c_shapes = _side_cast_specs(cast, nj * ni, lambda j, i: j * ni + i)
    w_spec = pl.BlockSpec((tn, D_MODEL), lambda j, i: (j, 0))
    cast_w = w_in_t.dtype != BF16
    return pl.pallas_call(
        functools.partial(_in_proj_body, n_gelu=2 * GATE_WIDTH // tn, sub=sub, n_cast=len(cast), cast_w=cast_w),
        grid=(nj, ni),
        in_specs=[pl.BlockSpec((tm, D_MODEL), lambda j, i: (i, 0)), w_spec] + c_in,
        out_specs=[pl.BlockSpec((tm, tn), lambda j, i: (i, j))] + c_out + [w_spec] * cast_w,
        out_shape=([jax.ShapeDtypeStruct((t, PROJ_MAIN), BF16)] + c_shapes
                   + [jax.ShapeDtypeStruct((PROJ_MAIN, D_MODEL), BF16)] * cast_w),
        scratch_shapes=[pltpu.VMEM((tn, D_MODEL), BF16)],
        compiler_params=_cparams(("arbitrary", "arbitrary")),
        name="in_proj",
    )(h, w_in_t, *cast)


def _gate_tile(gu_ref, gv_ref, lng_ref, lnb_ref, wm_s, btT_ref, mix_ref):
    g = gv_ref[...].astype(F32)
    mu = jnp.mean(g, axis=-1, keepdims=True)
    xc = g - mu
    v = xc * lax.rsqrt(jnp.mean(xc * xc, axis=-1, keepdims=True) + EPS) * lng_ref[...] + lnb_ref[...]
    vb = v.astype(BF16)
    rows = v.shape[0]
    for h in range(GATE_HEADS):
        sl = slice(h * GATE_HEAD_DIM, (h + 1) * GATE_HEAD_DIM)
        s = _dot(wm_s[h], vb[:, sl]) + jnp.broadcast_to(btT_ref[:, h:h + 1], (rows, GATE_HEAD_DIM))
        mix_ref[:, sl] = (gu_ref[:, sl].astype(F32) * s).astype(mix_ref.dtype)
    return v


def _ssd_token_level(act, dt_raw, dtb_ref, alog_ref, maskf, segf):
    xs = act[:, :SSM_WIDTH]
    bm = act[:, SSM_WIDTH:SSM_WIDTH + GROUP_WIDTH]
    cm = act[:, SSM_WIDTH + GROUP_WIDTH:]
    dt = jax.nn.softplus(dt_raw + dtb_ref[...])
    a = -jnp.exp(alog_ref[...])
    da = dt * a
    cs = _mm_split(maskf, da, 3)
    cl = _mm_split(segf, da, 3)
    return xs, bm, cm, dt, cs, cl


def _ssd_diag_pair(cb, cs, cs_t, dt_t, mask, h):
    seg = cs[:, h:h + 1] - cs_t[h:h + 1, :]
    decay = jnp.exp(jnp.where(mask, seg, -jnp.inf))
    return cb * decay * dt_t[h:h + 1, :]


def _ssd_finish(y, xs, z_ref, dexp_ref, nwm_ref, mix_ref):
    y = y + dexp_ref[...] * xs
    y = y * jax.nn.silu(z_ref[...].astype(F32))
    for g in range(SSM_GROUPS):
        sl = slice(g * GROUP_WIDTH, (g + 1) * GROUP_WIDTH)
        yg = y[:, sl]
        yg = yg * lax.rsqrt(jnp.mean(yg * yg, axis=-1, keepdims=True) + EPS) * nwm_ref[:, sl]
        mix_ref[:, GATE_WIDTH + g * GROUP_WIDTH:GATE_WIDTH + (g + 1) * GROUP_WIDTH] = yg.astype(mix_ref.dtype)


def _mixer_prompt_body(gu_ref, gv_ref, z_ref, xbc_ref, dt_ref, lng_ref, lnb_ref, wt_ref, btT_ref, cw_ref, cb_ref,
                       dtb_ref, alog_ref, dexp_ref, nwm_ref, e_ref, *rest, n_cast):
    cast_in, (mix_ref, cv_ref, st_ref, sc_ref) = rest[:n_cast], rest[n_cast:n_cast + 4]
    cast_out, (wm_s, shift_s, xx_s, st_s, y_s) = rest[n_cast + 4:2 * n_cast + 4], rest[2 * n_cast + 4:]
    _side_cast(cast_in, cast_out)
    b = pl.program_id(0)
    c = pl.program_id(1)
    last = c == pl.num_programs(1) - 1
    rows = CHUNK
    keep = 2 * SUBLANES
    mask, same = _seg_masks(rows, rows)

    @pl.when((b == 0) & (c == 0))
    def _():
        for h in range(GATE_HEADS):
            wm_s[h] = jnp.where(mask, wt_ref[h], 0.0).astype(BF16)
        r = lax.broadcasted_iota(jnp.int32, shift_s.shape, 0)
        col = lax.broadcasted_iota(jnp.int32, shift_s.shape, 1)
        shift_s[...] = jnp.where(col == rows + r % rows - (r // rows + 1), 1.0, 0.0).astype(BF16)
        xx_s[0:rows, :] = jnp.zeros((rows, SSM_CONV_DIM), BF16)

    @pl.when(c == 0)
    def _():
        xx_s[rows - keep:rows, :] = jnp.zeros((keep, SSM_CONV_DIM), BF16)
        st_s[...] = jnp.zeros(st_s.shape, F32)

    v = _gate_tile(gu_ref, gv_ref, lng_ref, lnb_ref, wm_s, btT_ref, mix_ref)

    @pl.when(last)
    def _():
        cv_ref[0] = v

    xx_s[rows:2 * rows, :] = xbc_ref[...]
    sh = _dot(shift_s[...], xx_s[...])
    conv = cb_ref[...] + cw_ref[SSM_CONV - 1:SSM_CONV, :] * xbc_ref[...].astype(F32)
    for d in range(1, SSM_CONV):
        conv = conv + cw_ref[SSM_CONV - 1 - d:SSM_CONV - d, :] * sh[(d - 1) * rows:d * rows, :]
    act = jax.nn.silu(conv)

    @pl.when(last)
    def _():
        sc_ref[0] = xbc_ref[rows - (SSM_CONV - 1):rows, :].astype(F32)

    xx_s[rows - keep:rows, :] = xx_s[2 * rows - keep:2 * rows, :]

    maskf = mask.astype(BF16)
    segf = same.astype(BF16)
    xs, bm, cm, dt, cs, cl = _ssd_token_level(act, dt_ref[...], dtb_ref, alog_ref, maskf, segf)
    cs_t = cs.T
    dt_t = dt.T
    ecs = jnp.exp(cs)
    e_bf = e_ref[...]
    coef_x = _expand_heads(dt * jnp.exp(cl - cs), e_bf)
    dlast_x = _expand_heads(jnp.exp(cl[0:SUBLANES, :]), e_bf)[0:1, :]
    lane = lax.broadcasted_iota(jnp.int32, (rows, LANES), 1)
    xs_b = xs.astype(BF16)
    for g in range(SSM_GROUPS):
        cg = cm[:, g * SSM_STATE:(g + 1) * SSM_STATE]
        bg = bm[:, g * SSM_STATE:(g + 1) * SSM_STATE]
        cb = lax.dot_general(cg.astype(BF16), bg.astype(BF16), (((1,), (1,)), ((), ())), preferred_element_type=F32)
        for p in range(HEADS_PER_GROUP // 2):
            h0 = g * HEADS_PER_GROUP + 2 * p
            sl = slice((h0 // 2) * LANES, (h0 // 2 + 1) * LANES)
            rhs = jnp.concatenate([xs_b[:, sl], st_s[:, sl].astype(BF16)], axis=0)
            ys = []
            for h in (h0, h0 + 1):
                m_h = _ssd_diag_pair(cb, cs, cs_t, dt_t, mask, h)
                c_h = cg * jnp.broadcast_to(ecs[:, h:h + 1], (rows, SSM_STATE))
                lhs = jnp.concatenate([m_h.astype(BF16), c_h.astype(BF16)], axis=1)
                ys.append(_dot(lhs, rhs))
            y_s[:, sl] = jnp.where(lane < SSM_HEAD_DIM, ys[0], ys[1])
    wc = (xs * coef_x).astype(BF16)
    for g in range(SSM_GROUPS):
        sl = slice(g * GROUP_WIDTH, (g + 1) * GROUP_WIDTH)
        bg = bm[:, g * SSM_STATE:(g + 1) * SSM_STATE].astype(BF16)
        upd = lax.dot_general(bg, wc[:, sl], (((0,), (0,)), ((), ())), preferred_element_type=F32)
        st_s[:, sl] = st_s[:, sl] * dlast_x[:, sl] + upd

    @pl.when(last)
    def _():
        st_ref[0] = st_s[...].T

    _ssd_finish(y_s[...], xs, z_ref, dexp_ref, nwm_ref, mix_ref)


def _mixer_prompt(proj, dt, prm, *, nb, nc, cast=()):
    rows = CHUNK
    t = nb * nc * rows
    row = lambda b, c: b * nc + c
    full = lambda shape: pl.BlockSpec(shape, lambda b, c: (0,) * len(shape))
    c_in, c_out, c_shapes = _side_cast_specs(cast, nb * nc, row)
    return pl.pallas_call(
        functools.partial(_mixer_prompt_body, n_cast=len(cast)),
        grid=(nb, nc),
        in_specs=[
            pl.BlockSpec((rows, GATE_WIDTH), lambda b, c: (row(b, c), 0)),
            pl.BlockSpec((rows, GATE_WIDTH), lambda b, c: (row(b, c), 1)),
            pl.BlockSpec((rows, SSM_WIDTH), lambda b, c: (row(b, c), 2)),
            pl.BlockSpec((rows, SSM_CONV_DIM), lambda b, c: (row(b, c), 2)),
            pl.BlockSpec((rows, LANES), lambda b, c: (row(b, c), 0)),
            full((1, GATE_WIDTH)), full((1, GATE_WIDTH)),
            full((GATE_HEADS, rows, rows)), full((rows, GATE_HEADS)),
            full((SSM_CONV, SSM_CONV_DIM)), full((1, SSM_CONV_DIM)),
            full((1, LANES)), full((1, LANES)), full((1, SSM_WIDTH)), full((1, SSM_WIDTH)),
            full((LANES, SSM_WIDTH)),
        ] + c_in,
        out_specs=[
            pl.BlockSpec((rows, 2 * GATE_WIDTH), lambda b, c: (row(b, c), 0)),
            pl.BlockSpec((1, rows, GATE_WIDTH), lambda b, c: (b, 0, 0)),
            pl.BlockSpec((1, SSM_WIDTH, SSM_STATE), lambda b, c: (b, 0, 0)),
            pl.BlockSpec((1, SSM_CONV - 1, SSM_CONV_DIM), lambda b, c: (b, 0, 0)),
        ] + c_out,
        out_shape=[
            jax.ShapeDtypeStruct((t, 2 * GATE_WIDTH), BF16),
            jax.ShapeDtypeStruct((nb, rows, GATE_WIDTH), F32),
            jax.ShapeDtypeStruct((nb, SSM_WIDTH, SSM_STATE), F32),
            jax.ShapeDtypeStruct((nb, SSM_CONV - 1, SSM_CONV_DIM), F32),
        ] + c_shapes,
        scratch_shapes=[
            pltpu.VMEM((GATE_HEADS, rows, rows), BF16),
            pltpu.VMEM(((SSM_CONV - 1) * rows, 2 * rows), BF16),
            pltpu.VMEM((2 * rows, SSM_CONV_DIM), BF16),
            pltpu.VMEM((SSM_STATE, SSM_WIDTH), F32),
            pltpu.VMEM((rows, SSM_WIDTH), F32),
        ],
        compiler_params=_cparams(("arbitrary", "arbitrary")),
        name="mixer_prompt",
    )(proj, proj, proj, proj, dt, prm["ln_g"], prm["ln_b"], prm["wt_p"], prm["btT_p"], prm["conv_w"], prm["conv_b"],
      prm["dtb"], prm["alog"], prm["dexp"], prm["nwm"], prm["e"], *cast)


SEQ_TILE = 8


def _state_decay_body(dt_ref, dtb_ref, alog_ref, o_ref, *, seg):
    nseq = o_ref.shape[0]
    a = -jnp.exp(alog_ref[...])
    tot = jnp.zeros(o_ref.shape, F32)
    for t in range(seg):
        d = jax.nn.softplus(dt_ref[pl.ds(t, nseq, stride=seg), :] + dtb_ref[...])
        tot = tot + d * a
    o_ref[...] = jnp.exp(tot)


def _state_decay(dt, prm, *, seg):
    nseq = dt.shape[0] // seg
    return pl.pallas_call(
        functools.partial(_state_decay_body, seg=seg),
        out_shape=jax.ShapeDtypeStruct((nseq, LANES), F32),
        name="state_decay",
    )(dt, prm["dtb"], prm["alog"])


def _mixer_sample_body(dec_ref, gu_ref, gv_ref, z_ref, xbc_ref, dt_ref, prev_ref, sin_ref,
                       lng_ref, lnb_ref, wt_ref, btT_ref, cw_ref, cb_ref,
                       dtb_ref, alog_ref, dexp_ref, nwm_ref, e_ref,
                       mix_ref, cv_ref, sout_ref, sc_ref,
                       wm_s, y_s, cm_s, bm_s, wct_s, ex_s, *, seg):
    i = pl.program_id(0)
    rows = SEQ_TILE * seg
    mask, same = _seg_masks(rows, seg)

    @pl.when(i == 0)
    def _():
        for h in range(GATE_HEADS):
            wm_s[h] = jnp.where(mask, wt_ref[h], 0.0).astype(BF16)

    cv_ref[...] = _gate_tile(gu_ref, gv_ref, lng_ref, lnb_ref, wm_s, btT_ref, mix_ref)

    x3 = xbc_ref[...].astype(F32).reshape(SEQ_TILE, seg, SSM_CONV_DIM)
    p3 = _history_block(prev_ref, slice(0, SEQ_TILE), seg)
    tpos = lax.broadcasted_iota(jnp.int32, x3.shape, 1)
    conv = cb_ref[...] + cw_ref[SSM_CONV - 1:SSM_CONV, :] * x3
    for d in range(1, SSM_CONV):
        shifted = jnp.where(tpos >= d, pltpu.roll(x3, d, axis=1), pltpu.roll(p3, d, axis=1))
        conv = conv + cw_ref[SSM_CONV - 1 - d:SSM_CONV - d, :] * shifted
    act = jax.nn.silu(conv).reshape(rows, SSM_CONV_DIM)
    sc_ref[...] = x3[:, seg - (SSM_CONV - 1):, :]

    maskf = mask.astype(BF16)
    segf = same.astype(BF16)
    xs, bm, cm, dt, cs, cl = _ssd_token_level(act, dt_ref[...], dtb_ref, alog_ref, maskf, segf)
    cs_t = cs.T
    dt_t = dt.T
    e_bf = e_ref[...]
    coef_x = _expand_heads(dt * jnp.exp(cl - cs), e_bf)
    ecs_x = _expand_heads(jnp.exp(cs), e_bf)
    lane = lax.broadcasted_iota(jnp.int32, (rows, LANES), 1)
    xs_b = xs.astype(BF16)
    for g in range(SSM_GROUPS):
        cg = cm[:, g * SSM_STATE:(g + 1) * SSM_STATE]
        bg = bm[:, g * SSM_STATE:(g + 1) * SSM_STATE]
        cb = lax.dot_general(cg.astype(BF16), bg.astype(BF16), (((1,), (1,)), ((), ())), preferred_element_type=F32)
        for p in range(HEADS_PER_GROUP // 2):
            h0 = g * HEADS_PER_GROUP + 2 * p
            sl = slice((h0 // 2) * LANES, (h0 // 2 + 1) * LANES)
            ys = [_dot(_ssd_diag_pair(cb, cs, cs_t, dt_t, mask, h).astype(BF16), xs_b[:, sl]) for h in (h0, h0 + 1)]
            y_s[:, sl] = jnp.where(lane < SSM_HEAD_DIM, ys[0], ys[1])
    cm_s[...] = cm
    bm_s[...] = bm
    wct_s[...] = (xs * coef_x).T
    ex_s[...] = ecs_x

    rowid = lax.broadcasted_iota(jnp.int32, (rows, SSM_STATE), 0)

    for s in range(SEQ_TILE):
        r8 = slice(s * seg, (s + 1) * seg)
        for g in range(SSM_GROUPS):
            gsl = slice(g * GROUP_WIDTH, (g + 1) * GROUP_WIDTH)
            nsl = slice(g * SSM_STATE, (g + 1) * SSM_STATE)
            st = sin_ref[s, gsl, :]
            c8 = cm_s[r8, nsl].astype(BF16)
            yo = lax.dot_general(c8, st.astype(BF16), (((1,), (1,)), ((), ())), preferred_element_type=F32)
            y_s[r8, gsl] = y_s[r8, gsl] + yo * ex_s[r8, gsl]
            bmask = jnp.where(rowid // seg == s, bm_s[:, nsl], 0.0).astype(BF16)
            upd = _dot(wct_s[gsl, :].astype(BF16), bmask)
            for r in range(HEADS_PER_GROUP):
                d = dec_ref[(i * SEQ_TILE + s) * SSM_HEADS + g * HEADS_PER_GROUP + r]
                hsl = slice(r * SSM_HEAD_DIM, (r + 1) * SSM_HEAD_DIM)
                osl = slice(g * GROUP_WIDTH + r * SSM_HEAD_DIM, g * GROUP_WIDTH + (r + 1) * SSM_HEAD_DIM)
                sout_ref[s, osl, :] = st[hsl, :] * d + upd[hsl, :]

    _ssd_finish(y_s[...], xs, z_ref, dexp_ref, nwm_ref, mix_ref)


def _mixer_sample(dec, proj, dt, prev, state, prm, *, seg):
    rows = SEQ_TILE * seg
    t = proj.shape[0]
    nseq = t // seg
    assert t % rows == 0
    full = lambda shape: pl.BlockSpec(shape, lambda i: (0,) * len(shape))
    return pl.pallas_call(
        functools.partial(_mixer_sample_body, seg=seg),
        grid=(t // rows,),
        in_specs=[
            pl.BlockSpec(memory_space=pltpu.SMEM),
            pl.BlockSpec((rows, GATE_WIDTH), lambda i: (i, 0)),
            pl.BlockSpec((rows, GATE_WIDTH), lambda i: (i, 1)),
            pl.BlockSpec((rows, SSM_WIDTH), lambda i: (i, 2)),
            pl.BlockSpec((rows, SSM_CONV_DIM), lambda i: (i, 2)),
            pl.BlockSpec((rows, LANES), lambda i: (i, 0)),
            pl.BlockSpec((SEQ_TILE, SSM_CONV - 1, SSM_CONV_DIM), lambda i: (i, 0, 0)),
            pl.BlockSpec((SEQ_TILE, SSM_WIDTH, SSM_STATE), lambda i: (i, 0, 0)),
            full((1, GATE_WIDTH)), full((1, GATE_WIDTH)),
            full((GATE_HEADS, rows, rows)), full((rows, GATE_HEADS)),
            full((SSM_CONV, SSM_CONV_DIM)), full((1, SSM_CONV_DIM)),
            full((1, LANES)), full((1, LANES)), full((1, SSM_WIDTH)), full((1, SSM_WIDTH)),
            full((LANES, SSM_WIDTH)),
        ],
        out_specs=[
            pl.BlockSpec((rows, 2 * GATE_WIDTH), lambda i: (i, 0)),
            pl.BlockSpec((rows, GATE_WIDTH), lambda i: (i, 0)),
            pl.BlockSpec((SEQ_TILE, SSM_WIDTH, SSM_STATE), lambda i: (i, 0, 0)),
            pl.BlockSpec((SEQ_TILE, SSM_CONV - 1, SSM_CONV_DIM), lambda i: (i, 0, 0)),
        ],
        out_shape=[
            jax.ShapeDtypeStruct((t, 2 * GATE_WIDTH), BF16),
            jax.ShapeDtypeStruct((t, GATE_WIDTH), F32),
            jax.ShapeDtypeStruct((nseq, SSM_WIDTH, SSM_STATE), F32),
            jax.ShapeDtypeStruct((nseq, SSM_CONV - 1, SSM_CONV_DIM), F32),
        ],
        scratch_shapes=[
            pltpu.VMEM((GATE_HEADS, rows, rows), BF16),
            pltpu.VMEM((rows, SSM_WIDTH), F32),
            pltpu.VMEM((rows, GROUP_WIDTH), F32),
            pltpu.VMEM((rows, GROUP_WIDTH), F32),
            pltpu.VMEM((SSM_WIDTH, rows), F32),
            pltpu.VMEM((rows, SSM_WIDTH), F32),
        ],
        compiler_params=_cparams(("arbitrary",)),
        name="mixer_sample",
    )(dec, proj, proj, proj, proj, dt, prev, state, prm["ln_g"], prm["ln_b"], prm["wt_s"], prm["btT_s"],
      prm["conv_w"], prm["conv_b"], prm["dtb"], prm["alog"], prm["dexp"], prm["nwm"], prm["e"])


ROW_SUB = 128


def _row_subs(tm, sub=ROW_SUB):
    sub = min(sub, tm)
    assert tm % sub == 0
    return [slice(r * sub, (r + 1) * sub) for r in range(tm // sub)]


def _out_proj_body(m_ref, x_ref, w_ref, npost_ref, npre_ref, x1_ref, h2_ref):
    for rs in _row_subs(m_ref.shape[0]):
        mix = _dot(m_ref[rs, :], w_ref[...])
        x1 = x_ref[rs, :] + _rms(mix, npost_ref[...])
        x1_ref[rs, :] = x1
        h2_ref[rs, :] = _rms(x1, npre_ref[...]).astype(h2_ref.dtype)


def _out_proj(mixin, x2d, w_out, npost, npre, *, tm):
    t, k = mixin.shape
    assert t % tm == 0
    return pl.pallas_call(
        _out_proj_body,
        grid=(t // tm,),
        in_specs=[
            pl.BlockSpec((tm, k), lambda i: (i, 0)),
            pl.BlockSpec((tm, D_MODEL), lambda i: (i, 0)),
            pl.BlockSpec((k, D_MODEL), lambda i: (0, 0), pipeline_mode=pl.Buffered(1)),
            pl.BlockSpec((1, D_MODEL), lambda i: (0, 0)),
            pl.BlockSpec((1, D_MODEL), lambda i: (0, 0)),
        ],
        out_specs=[
            pl.BlockSpec((tm, D_MODEL), lambda i: (i, 0)),
            pl.BlockSpec((tm, D_MODEL), lambda i: (i, 0)),
        ],
        out_shape=[
            jax.ShapeDtypeStruct((t, D_MODEL), F32),
            jax.ShapeDtypeStruct((t, D_MODEL), BF16),
        ],
        compiler_params=_cparams(("arbitrary",)),
        name="out_proj",
    )(mixin, x2d, w_out, npost, npre)


def _ffn_conv_taps(cur, shifted, cw_ref, cb_ref):
    out = cb_ref[...] + cw_ref[FFN_CONV - 1:FFN_CONV, :] * cur
    for d in range(1, FFN_CONV):
        out = out + cw_ref[FFN_CONV - 1 - d:FFN_CONV - d, :] * shifted[d - 1]
    return out


def _ffn_up_prompt_body(h_ref, wg_ref, wu_ref, cwg_ref, cwu_ref, cbg_ref, cbu_ref, *rest, tiles_per_seq, n_cast):
    cast_in, (a_ref, stg_ref, stu_ref) = rest[:n_cast], rest[n_cast:n_cast + 3]
    cast_out, (tg_s, tu_s) = rest[n_cast + 3:2 * n_cast + 3], rest[2 * n_cast + 3:]
    i = pl.program_id(1)
    tm = h_ref.shape[0]
    _side_cast(cast_in, cast_out)

    @pl.when((i % tiles_per_seq) == 0)
    def _():
        for t_s in (tg_s, tu_s):
            t_s[...] = jnp.zeros(t_s.shape, F32)

    tails = [tg_s[...], tu_s[...]]
    row8 = lax.broadcasted_iota(jnp.int32, tg_s.shape, 0)
    for rs in _row_subs(tm, 128):
        h = h_ref[rs, :]
        convs = []
        for k, (w_ref, cw_ref, cb_ref) in enumerate(((wg_ref, cwg_ref, cbg_ref), (wu_ref, cwu_ref, cbu_ref))):
            x = _dot(h, w_ref[...])
            shifted = []
            for d in range(1, FFN_CONV):
                r = pltpu.roll(x, d, axis=0)
                head = jnp.where(row8 >= d, r[0:SUBLANES], pltpu.roll(tails[k], d, axis=0))
                shifted.append(jnp.concatenate([head, r[SUBLANES:]], axis=0))
            convs.append(_ffn_conv_taps(x, shifted, cw_ref, cb_ref))
            tails[k] = x[x.shape[0] - SUBLANES:]
        a_ref[rs, :] = (jax.nn.gelu(convs[0], approximate=True) * convs[1]).astype(a_ref.dtype)
    for tail, t_s, st_ref in zip(tails, (tg_s, tu_s), (stg_ref, stu_ref)):
        t_s[...] = tail
        st_ref[0] = tail[SUBLANES - (FFN_CONV - 1):]


def _ffn_up_sample_body(h_ref, wg_ref, wu_ref, cwg_ref, cwu_ref, cbg_ref, cbu_ref, pg_ref, pu_ref,
                        a_ref, stg_ref, stu_ref, *, seg):
    tm = h_ref.shape[0]
    tn = wg_ref.shape[1]
    for rs in _row_subs(tm):
        n = rs.stop - rs.start
        nseq = n // seg
        sq = slice(rs.start // seg, rs.stop // seg)
        h = h_ref[rs, :]
        tpos = lax.broadcasted_iota(jnp.int32, (nseq, seg, tn), 1)
        convs = []
        for w_ref, p_ref, cw_ref, cb_ref, st_ref in ((wg_ref, pg_ref, cwg_ref, cbg_ref, stg_ref),
                                                     (wu_ref, pu_ref, cwu_ref, cbu_ref, stu_ref)):
            x3 = _dot(h, w_ref[...]).reshape(nseq, seg, tn)
            p3 = _history_block(p_ref, sq, seg)
            shifted = [jnp.where(tpos >= d, pltpu.roll(x3, d, axis=1), pltpu.roll(p3, d, axis=1))
                       for d in range(1, FFN_CONV)]
            convs.append(_ffn_conv_taps(x3, shifted, cw_ref, cb_ref))
            st_ref[sq] = x3[:, seg - (FFN_CONV - 1):, :]
        act = jax.nn.gelu(convs[0], approximate=True) * convs[1]
        a_ref[rs, :] = act.reshape(n, tn).astype(a_ref.dtype)


def _ffn_up(h2, w_up, cw, cb, prev, *, nseq, seg, tm, tn, cast=()):
    t = h2.shape[0]
    nj, ni = D_FF // tn, t // tm
    assert t % tm == 0 and D_FF % tn == 0 and t == nseq * seg
    common_in = [
        pl.BlockSpec((tm, D_MODEL), lambda j, i: (i, 0)),
        pl.BlockSpec((D_MODEL, tn), lambda j, i: (0, j)),
        pl.BlockSpec((D_MODEL, tn), lambda j, i: (0, j + nj)),
        pl.BlockSpec((FFN_CONV, tn), lambda j, i: (0, j)),
        pl.BlockSpec((FFN_CONV, tn), lambda j, i: (0, j + nj)),
        pl.BlockSpec((1, tn), lambda j, i: (0, j)),
        pl.BlockSpec((1, tn), lambda j, i: (0, j + nj)),
    ]
    a_spec = pl.BlockSpec((tm, tn), lambda j, i: (i, j))
    st_shape = jax.ShapeDtypeStruct((nseq, FFN_CONV - 1, D_FF), F32)
    out_shape = [jax.ShapeDtypeStruct((t, D_FF), BF16), st_shape, st_shape]
    if prev is None:
        assert seg % tm == 0
        tps = seg // tm
        c_in, c_out, c_shapes = _side_cast_specs(cast, nj * ni, lambda j, i: j * ni + i)
        return pl.pallas_call(
            functools.partial(_ffn_up_prompt_body, tiles_per_seq=tps, n_cast=len(cast)),
            grid=(nj, ni),
            in_specs=common_in + c_in,
            out_specs=[a_spec] + [pl.BlockSpec((1, FFN_CONV - 1, tn), lambda j, i: (i // tps, 0, j))] * 2 + c_out,
            out_shape=out_shape + c_shapes,
            scratch_shapes=[pltpu.VMEM((SUBLANES, tn), F32), pltpu.VMEM((SUBLANES, tn), F32)],
            compiler_params=_cparams(("arbitrary", "arbitrary")),
            name="ffn_up_prompt",
        )(h2, w_up, w_up, cw, cw, cb, cb, *cast)
    assert tm % seg == 0 and seg == SUBLANES and not cast
    return pl.pallas_call(
        functools.partial(_ffn_up_sample_body, seg=seg),
        grid=(nj, ni),
        in_specs=common_in + [
            pl.BlockSpec((tm // seg, FFN_CONV - 1, tn), lambda j, i: (i, 0, j)),
            pl.BlockSpec((tm // seg, FFN_CONV - 1, tn), lambda j, i: (i, 0, j + nj)),
        ],
        out_specs=[a_spec] + [pl.BlockSpec((tm // seg, FFN_CONV - 1, tn), lambda j, i: (i, 0, j))] * 2,
        out_shape=out_shape,
        compiler_params=_cparams(("arbitrary", "arbitrary")),
        name="ffn_up_sample",
    )(h2, w_up, w_up, cw, cw, cb, cb, prev, prev)


def _ffn_down_body(a_ref, x_ref, w_ref, nw_ref, y_ref):
    for rs in _row_subs(a_ref.shape[0], 128):
        f = _dot(a_ref[rs, :], w_ref[...])
        y_ref[rs, :] = x_ref[rs, :] + _rms(f, nw_ref[...])


def _ffn_down(act, x1, w_down, nw, *, tm):
    t, k = act.shape
    assert t % tm == 0
    return pl.pallas_call(
        _ffn_down_body,
        grid=(t // tm,),
        in_specs=[
            pl.BlockSpec((tm, k), lambda i: (i, 0)),
            pl.BlockSpec((tm, D_MODEL), lambda i: (i, 0)),
            pl.BlockSpec((k, D_MODEL), lambda i: (0, 0), pipeline_mode=pl.Buffered(1)),
            pl.BlockSpec((1, D_MODEL), lambda i: (0, 0)),
        ],
        out_specs=pl.BlockSpec((tm, D_MODEL), lambda i: (i, 0)),
        out_shape=jax.ShapeDtypeStruct((t, D_MODEL), F32),
        compiler_params=_cparams(("arbitrary",)),
        name="ffn_down",
    )(act, x1, w_down, nw)


def _head_expander():
    e = np.zeros((LANES, SSM_WIDTH), np.float32)
    for h in range(SSM_HEADS):
        e[h, h * SSM_HEAD_DIM:(h + 1) * SSM_HEAD_DIM] = 1.0
    return jnp.asarray(e, BF16)


def _pad_lanes(v):
    return jnp.pad(v, (0, LANES - v.shape[0]))[None, :]


def _prep_params(norm_mix_pre, w_in, gate_ln_g, gate_ln_b, gate_w_s, gate_b_s, ssm_conv_w, ssm_conv_b, ssm_dt_bias,
                 ssm_a_log, ssm_d, ssm_norm_w, w_out, norm_mix_post, norm_ffn_pre, ffn_w_up, ffn_conv_w, ffn_conv_b,
                 ffn_w_down, norm_ffn_post, seg_sample):
    w_in_t = jnp.swapaxes(w_in, 0, 1)
    ws_small = gate_w_s[:, :seg_sample, :seg_sample]
    return dict(
        nw_pre=norm_mix_pre[None, :],
        w_in_t=w_in_t,
        w_dt=jnp.pad(w_in_t[PROJ_MAIN:, :], ((0, LANES - SSM_HEADS), (0, 0))).astype(BF16),
        ln_g=gate_ln_g[None, :], ln_b=gate_ln_b[None, :],
        wt_p=gate_w_s, btT_p=gate_b_s.T,
        wt_s=jnp.tile(ws_small, (1, SEQ_TILE, SEQ_TILE)), btT_s=jnp.tile(gate_b_s[:, :seg_sample], (1, SEQ_TILE)).T,
        conv_w=ssm_conv_w, conv_b=ssm_conv_b[None, :],
        dtb=_pad_lanes(ssm_dt_bias), alog=_pad_lanes(ssm_a_log),
        dexp=jnp.repeat(ssm_d, SSM_HEAD_DIM)[None, :], nwm=ssm_norm_w[None, :],
        e=_head_expander(),
        w_out=w_out, n_post=norm_mix_post[None, :], n_pre2=norm_ffn_pre[None, :],
        w_up=ffn_w_up, fcw=ffn_conv_w, fcb=ffn_conv_b[None, :],
        w_down=ffn_w_down, n_post2=norm_ffn_post[None, :],
    )


def _row_tile(t, cap):
    tm = cap
    while t % tm:
        tm //= 2
    assert tm >= 64
    return tm


TM_STREAM = 1024
TM_FFN_UP = 2048
TM_OUT_PROJ = 512
TM_FFN_DOWN = 512
TM_PRE_NORM = 1024


def _layer_prompt(x, prm):
    nb, seq, _ = x.shape
    assert seq % CHUNK == 0
    nc = seq // CHUNK
    x2d = x.reshape(nb * seq, D_MODEL)
    tm = _row_tile(seq, TM_STREAM)
    h, dt = _pre_norm(x2d, prm["nw_pre"], prm["w_dt"], tm=_row_tile(seq, TM_PRE_NORM))
    proj, w_out, w_in_b = _in_proj(h, prm["w_in_t"], tm=tm, tn=1024, sub=min(tm, 256), cast=(prm["w_out"],))
    mixin, cv, st, sc, w_up = _mixer_prompt(proj, dt, prm, nb=nb, nc=nc, cast=(prm["w_up"],))
    x1, h2 = _out_proj(mixin, x2d, w_out, prm["n_post"], prm["n_pre2"], tm=_row_tile(seq, TM_OUT_PROJ))
    act, fst_g, fst_u, w_down = _ffn_up(h2, w_up, prm["fcw"], prm["fcb"], None, nseq=nb, seg=seq,
                                        tm=_row_tile(seq, TM_FFN_UP), tn=512, cast=(prm["w_down"],))
    y = _ffn_down(act, x1, w_down, prm["n_post2"], tm=_row_tile(seq, TM_FFN_DOWN))
    return dict(w_in_t=w_in_b, w_out=w_out, w_up=w_up, w_down=w_down), (
            y.reshape(nb, seq, D_MODEL),
            st.reshape(nb, SSM_HEADS, SSM_HEAD_DIM, SSM_STATE),
            sc,
            jnp.concatenate([fst_g, fst_u], axis=-1),
            cv.reshape(nb, CHUNK, GATE_HEADS, GATE_HEAD_DIM))


def _layer_sample(x, state_ssm, state_sconv, state_fconv, prm, wb):
    nb, seg, _ = x.shape
    assert seg == SUBLANES and nb % SEQ_TILE == 0
    t = nb * seg
    x2d = x.reshape(t, D_MODEL)
    tm = _row_tile(t, TM_STREAM)
    h, dt = _pre_norm(x2d, prm["nw_pre"], prm["w_dt"], tm=_row_tile(t, TM_PRE_NORM))
    proj, = _in_proj(h, wb["w_in_t"], tm=tm, tn=1024, sub=min(tm, 256))
    dec = _state_decay(dt, prm, seg=seg)[:, :SSM_HEADS].reshape(nb * SSM_HEADS)
    mixin, cv, st, sc = _mixer_sample(dec, proj, dt, state_sconv,
                                      state_ssm.reshape(nb, SSM_WIDTH, SSM_STATE), prm, seg=seg)
    x1, h2 = _out_proj(mixin, x2d, wb["w_out"], prm["n_post"], prm["n_pre2"], tm=_row_tile(t, TM_OUT_PROJ))
    act, fst_g, fst_u = _ffn_up(h2, wb["w_up"], prm["fcw"], prm["fcb"], state_fconv, nseq=nb, seg=seg, tm=tm, tn=512)
    y = _ffn_down(act, x1, wb["w_down"], prm["n_post2"], tm=_row_tile(t, TM_FFN_DOWN))
    return (y.reshape(nb, seg, D_MODEL),
            st.reshape(nb, SSM_HEADS, SSM_HEAD_DIM, SSM_STATE),
            sc,
            jnp.concatenate([fst_g, fst_u], axis=-1),
            cv.reshape(nb, seg, GATE_HEADS, GATE_HEAD_DIM))


def kernel(x_prompt, x_sample, state_ssm, state_ssm_conv, state_ffn_conv, norm_mix_pre, w_in, gate_ln_g, gate_ln_b,
           gate_w_s, gate_b_s, ssm_conv_w, ssm_conv_b, ssm_dt_bias, ssm_a_log, ssm_d, ssm_norm_w, w_out,
           norm_mix_post, norm_ffn_pre, ffn_w_up, ffn_conv_w, ffn_conv_b, ffn_w_down, norm_ffn_post):
    depth = w_in.shape[0]
    yp, ys = x_prompt, x_sample
    outs_p, outs_s = [], []
    for l in range(depth):
        prm = _prep_params(norm_mix_pre[l], w_in[l], gate_ln_g[l], gate_ln_b[l], gate_w_s[l], gate_b_s[l],
                           ssm_conv_w[l], ssm_conv_b[l], ssm_dt_bias[l], ssm_a_log[l], ssm_d[l], ssm_norm_w[l],
                           w_out[l], norm_mix_post[l], norm_ffn_pre[l], ffn_w_up[l], ffn_conv_w[l], ffn_conv_b[l],
                           ffn_w_down[l], norm_ffn_post[l], x_sample.shape[1])
        wb, (yp, *rest_p) = _layer_prompt(yp, prm)
        ys, *rest_s = _layer_sample(ys, state_ssm[l], state_ssm_conv[l], state_ffn_conv[l], prm, wb)
        outs_p.append(rest_p)
        outs_s.append(rest_s)
    stack = lambda outs, k: jnp.stack([o[k] for o in outs])
    return (yp, ys,
            stack(outs_p, 0), stack(outs_p, 1), stack(outs_p, 2), stack(outs_p, 3),
            stack(outs_s, 0), stack(outs_s, 1), stack(outs_s, 2), stack(outs_s, 3))
```

```python
import functools

import jax
import jax.numpy as jnp
import numpy as np
from jax import lax
from jax.experimental import pallas as pl
from jax.experimental.pallas import tpu as pltpu

F32 = jnp.float32
BF16 = jnp.bfloat16

D_MODEL = 2048
GATE_WIDTH = 2048
GATE_HEADS = 16
GATE_HEAD_DIM = 128
CHUNK = 128
SSM_WIDTH = 2048
SSM_HEAD_DIM = 64
SSM_HEADS = 32
SSM_GROUPS = 4
SSM_STATE = 128
SSM_CONV = 4
SSM_CONV_DIM = SSM_WIDTH + 2 * SSM_GROUPS * SSM_STATE
PROJ_MAIN = 2 * GATE_WIDTH + SSM_WIDTH + SSM_CONV_DIM
D_FF = 5632
FFN_CONV = 3
EPS = 1e-6
HEADS_PER_GROUP = SSM_HEADS // SSM_GROUPS
GROUP_WIDTH = SSM_WIDTH // SSM_GROUPS

LANES = 128
SUBLANES = 8
VMEM_LIMIT = 56 * 1024 * 1024


def _cparams(sem):
    return pltpu.CompilerParams(dimension_semantics=sem, vmem_limit_bytes=VMEM_LIMIT)


def _rms(x, w):
    return x * lax.rsqrt(jnp.mean(x * x, axis=-1, keepdims=True) + EPS) * w


def _gelu_erf(x):
    return 0.5 * x * (1.0 + lax.erf(x * np.float32(0.7071067811865476)))


def _split_bf16(x, n):
    parts = []
    r = x
    for k in range(n):
        p = r.astype(BF16)
        parts.append(p)
        if k + 1 < n:
            r = r - p.astype(F32)
    return parts


def _dot(a, b):
    return jnp.dot(a, b, preferred_element_type=F32)


def _dot_nt(a, b):
    return lax.dot_general(a, b, (((1,), (1,)), ((), ())), preferred_element_type=F32)


def _mm_split(m_bf, x, n):
    acc = None
    for p in _split_bf16(x, n):
        t = _dot(m_bf, p)
        acc = t if acc is None else acc + t
    return acc


def _expand_heads(x, e_bf):
    acc = None
    for p in _split_bf16(x, 2):
        t = _dot(p, e_bf)
        acc = t if acc is None else acc + t
    return acc


def _seq_shift(x3, p_ref, sq, d, tpos):
    k1 = p_ref.shape[1]
    out = pltpu.roll(x3, d, axis=1)
    for t in range(d):
        out = jnp.where(tpos == t, jnp.broadcast_to(p_ref[sq, k1 + t - d:k1 + t - d + 1, :], x3.shape), out)
    return out


def _seg_masks(rows, seg):
    r = lax.broadcasted_iota(jnp.int32, (rows, rows), 0)
    c = lax.broadcasted_iota(jnp.int32, (rows, rows), 1)
    same = (r // seg) == (c // seg)
    return same & (c <= r), same


BF16_ROWS = 2 * SUBLANES


def _cast_rows(rows, nsteps):
    per = BF16_ROWS
    while rows % per or rows // per > nsteps:
        per += BF16_ROWS
        assert per <= rows
    return per


def _side_cast_specs(weights, nsteps, step_of):
    ins, outs, shapes = [], [], []
    for w in weights:
        rows, cols = w.shape
        per = _cast_rows(rows, nsteps)
        spec = pl.BlockSpec((per, cols), lambda *g, nb=rows // per: (jnp.minimum(step_of(*g), nb - 1), 0))
        ins.append(spec)
        outs.append(spec)
        shapes.append(jax.ShapeDtypeStruct((rows, cols), BF16))
    return ins, outs, shapes


def _side_cast(cast_in, cast_out):
    for w_ref, o_ref in zip(cast_in, cast_out):
        o_ref[...] = w_ref[...].astype(BF16)


def _pre_norm_body(x_ref, nw_ref, wdt_ref, h_ref, dt_ref):
    h = _rms(x_ref[...], nw_ref[...]).astype(BF16)
    h_ref[...] = h
    dt_ref[...] = _dot_nt(h, wdt_ref[...])


def _pre_norm(x2d, nw, w_dt, *, tm):
    t = x2d.shape[0]
    assert t % tm == 0
    return pl.pallas_call(
        _pre_norm_body,
        grid=(t // tm,),
        in_specs=[
            pl.BlockSpec((tm, D_MODEL), lambda i: (i, 0)),
            pl.BlockSpec((1, D_MODEL), lambda i: (0, 0)),
            pl.BlockSpec((LANES, D_MODEL), lambda i: (0, 0)),
        ],
        out_specs=[
            pl.BlockSpec((tm, D_MODEL), lambda i: (i, 0)),
            pl.BlockSpec((tm, LANES), lambda i: (i, 0)),
        ],
        out_shape=[
            jax.ShapeDtypeStruct((t, D_MODEL), BF16),
            jax.ShapeDtypeStruct((t, LANES), F32),
        ],
        compiler_params=_cparams(("arbitrary",)),
        name="pre_norm",
    )(x2d, nw, w_dt)


def _in_proj_body(h_ref, w_ref, *rest, n_gelu, n_silu, sub, n_cast, cast_w):
    cast_in, o_ref, cast_out, wb_s = rest[:n_cast], rest[n_cast], rest[n_cast + 1:2 * n_cast + 1], rest[-1]
    j = pl.program_id(0)
    i = pl.program_id(1)
    tm = h_ref.shape[0]
    _side_cast(cast_in, cast_out)

    @pl.when(i == 0)
    def _():
        wb_s[...] = w_ref[...].astype(BF16)
        if cast_w:
            rest[-2][...] = wb_s[...]

    def run(epilogue):
        for rs in _row_subs(tm, sub):
            o_ref[rs, :] = epilogue(_dot_nt(h_ref[rs, :], wb_s[...])).astype(o_ref.dtype)

    @pl.when(j < n_gelu)
    def _():
        run(_gelu_erf)

    @pl.when((j >= n_gelu) & (j < n_gelu + n_silu))
    def _():
        run(jax.nn.silu)

    @pl.when(j >= n_gelu + n_silu)
    def _():
        run(lambda a: a)


def _in_proj(h, w_in_t, *, tm, tn, sub, cast=()):
    t = h.shape[0]
    assert t % tm == 0 and PROJ_MAIN % tn == 0 and GATE_WIDTH % tn == 0 and SSM_WIDTH % tn == 0 and tm % sub == 0
    nj, ni = PROJ_MAIN // tn, t // tm
    c_in, c_out, c_shapes = _side_cast_specs(cast, nj * ni, lambda j, i: j * ni + i)
    w_spec = pl.BlockSpec((tn, D_MODEL), lambda j, i: (j, 0))
    cast_w = w_in_t.dtype != BF16
    return pl.pallas_call(
        functools.partial(_in_proj_body, n_gelu=2 * GATE_WIDTH // tn, n_silu=SSM_WIDTH // tn, sub=sub,
                          n_cast=len(cast), cast_w=cast_w),
        grid=(nj, ni),
        in_specs=[pl.BlockSpec((tm, D_MODEL), lambda j, i: (i, 0)), w_spec] + c_in,
        out_specs=[pl.BlockSpec((tm, tn), lambda j, i: (i, j))] + c_out + [w_spec] * cast_w,
        out_shape=([jax.ShapeDtypeStruct((t, PROJ_MAIN), BF16)] + c_shapes
                   + [jax.ShapeDtypeStruct((PROJ_MAIN, D_MODEL), BF16)] * cast_w),
        scratch_shapes=[pltpu.VMEM((tn, D_MODEL), BF16)],
        compiler_params=_cparams(("arbitrary", "arbitrary")),
        name="in_proj",
    )(h, w_in_t, *cast)


def _gate_tile(gu_ref, gv_ref, lng_ref, lnb_ref, wm_s, btT_ref, mix_ref):
    g = gv_ref[...].astype(F32)
    mu = jnp.mean(g, axis=-1, keepdims=True)
    xc = g - mu
    v = xc * lax.rsqrt(jnp.mean(xc * xc, axis=-1, keepdims=True) + EPS) * lng_ref[...] + lnb_ref[...]
    vb = v.astype(BF16)
    rows = v.shape[0]
    for h in range(GATE_HEADS):
        sl = slice(h * GATE_HEAD_DIM, (h + 1) * GATE_HEAD_DIM)
        s = _dot(wm_s[h], vb[:, sl]) + jnp.broadcast_to(btT_ref[:, h:h + 1], (rows, GATE_HEAD_DIM))
        mix_ref[:, sl] = gu_ref[:, sl] * s.astype(mix_ref.dtype)
    return v


def _ssd_token_level(act, dt_raw, dtb_ref, alog_ref, maskf, segf):
    xs = act[:, :SSM_WIDTH]
    bm = act[:, SSM_WIDTH:SSM_WIDTH + GROUP_WIDTH]
    cm = act[:, SSM_WIDTH + GROUP_WIDTH:]
    dt = jax.nn.softplus(dt_raw + dtb_ref[...])
    a = -jnp.exp(alog_ref[...])
    da = dt * a
    cs = _mm_split(maskf, da, 3)
    cl = _mm_split(segf, da, 3)
    return xs, bm, cm, dt, cs, cl


def _ssd_diag_pair(cb, cs, cs_t, dt_t, mask, h):
    seg = cs[:, h:h + 1] - cs_t[h:h + 1, :]
    decay = jnp.exp(jnp.where(mask, seg, -jnp.inf))
    return cb * decay * dt_t[h:h + 1, :]


def _ssd_finish(y, xs, z_ref, dexp_ref, nwm_ref, mix_ref):
    y = y + dexp_ref[...] * xs
    y = y * z_ref[...].astype(F32)
    for g in range(SSM_GROUPS):
        sl = slice(g * GROUP_WIDTH, (g + 1) * GROUP_WIDTH)
        yg = y[:, sl]
        yg = yg * lax.rsqrt(jnp.mean(yg * yg, axis=-1, keepdims=True) + EPS) * nwm_ref[:, sl]
        mix_ref[:, GATE_WIDTH + g * GROUP_WIDTH:GATE_WIDTH + (g + 1) * GROUP_WIDTH] = yg.astype(mix_ref.dtype)


def _mixer_prompt_body(gu_ref, gv_ref, z_ref, xbc_ref, dt_ref, lng_ref, lnb_ref, wt_ref, btT_ref, cw_ref, cb_ref,
                       dtb_ref, alog_ref, dexp_ref, nwm_ref, e_ref,
                       mix_ref, cv_ref, st_ref, sc_ref,
                       wm_s, shift_s, xx_s, st_s, y_s):
    b = pl.program_id(0)
    c = pl.program_id(1)
    last = c == pl.num_programs(1) - 1
    rows = CHUNK
    keep = 2 * SUBLANES
    mask, same = _seg_masks(rows, rows)

    @pl.when((b == 0) & (c == 0))
    def _():
        for h in range(GATE_HEADS):
            wm_s[h] = jnp.where(mask, wt_ref[h], 0.0).astype(BF16)
        r = lax.broadcasted_iota(jnp.int32, shift_s.shape, 0)
        col = lax.broadcasted_iota(jnp.int32, shift_s.shape, 1)
        shift_s[...] = jnp.where(col == rows + r % rows - (r // rows + 1), 1.0, 0.0).astype(BF16)
        xx_s[0:rows, :] = jnp.zeros((rows, SSM_CONV_DIM), BF16)

    @pl.when(c == 0)
    def _():
        xx_s[rows - keep:rows, :] = jnp.zeros((keep, SSM_CONV_DIM), BF16)
        st_s[...] = jnp.zeros(st_s.shape, F32)

    v = _gate_tile(gu_ref, gv_ref, lng_ref, lnb_ref, wm_s, btT_ref, mix_ref)

    @pl.when(last)
    def _():
        cv_ref[0] = v

    xx_s[rows:2 * rows, :] = xbc_ref[...]
    sh = _dot(shift_s[...], xx_s[...])
    conv = cb_ref[...] + cw_ref[SSM_CONV - 1:SSM_CONV, :] * xbc_ref[...].astype(F32)
    for d in range(1, SSM_CONV):
        conv = conv + cw_ref[SSM_CONV - 1 - d:SSM_CONV - d, :] * sh[(d - 1) * rows:d * rows, :]
    act = jax.nn.silu(conv)

    @pl.when(last)
    def _():
        sc_ref[0] = xbc_ref[rows - (SSM_CONV - 1):rows, :].astype(F32)

    xx_s[rows - keep:rows, :] = xx_s[2 * rows - keep:2 * rows, :]

    maskf = mask.astype(BF16)
    segf = same.astype(BF16)
    xs, bm, cm, dt, cs, cl = _ssd_token_level(act, dt_ref[...], dtb_ref, alog_ref, maskf, segf)
    cs_t = cs.T
    dt_t = dt.T
    ecs = jnp.exp(cs)
    e_bf = e_ref[...]
    coef_x = _expand_heads(dt * jnp.exp(cl - cs), e_bf)
    dlast_x = _expand_heads(jnp.exp(cl[0:SUBLANES, :]), e_bf)[0:1, :]
    lane = lax.broadcasted_iota(jnp.int32, (rows, LANES), 1)
    xs_b = xs.astype(BF16)
    for g in range(SSM_GROUPS):
        cg = cm[:, g * SSM_STATE:(g + 1) * SSM_STATE]
        bg = bm[:, g * SSM_STATE:(g + 1) * SSM_STATE]
        cb = lax.dot_general(cg.astype(BF16), bg.astype(BF16), (((1,), (1,)), ((), ())), preferred_element_type=F32)
        for p in range(HEADS_PER_GROUP // 2):
            h0 = g * HEADS_PER_GROUP + 2 * p
            sl = slice((h0 // 2) * LANES, (h0 // 2 + 1) * LANES)
            rhs = jnp.concatenate([xs_b[:, sl], st_s[:, sl].astype(BF16)], axis=0)
            ys = []
            for h in (h0, h0 + 1):
                m_h = _ssd_diag_pair(cb, cs, cs_t, dt_t, mask, h)
                c_h = cg * jnp.broadcast_to(ecs[:, h:h + 1], (rows, SSM_STATE))
                lhs = jnp.concatenate([m_h.astype(BF16), c_h.astype(BF16)], axis=1)
                ys.append(_dot(lhs, rhs))
            y_s[:, sl] = jnp.where(lane < SSM_HEAD_DIM, ys[0], ys[1])
    wc = (xs * coef_x).astype(BF16)
    for g in range(SSM_GROUPS):
        sl = slice(g * GROUP_WIDTH, (g + 1) * GROUP_WIDTH)
        bg = bm[:, g * SSM_STATE:(g + 1) * SSM_STATE].astype(BF16)
        upd = lax.dot_general(bg, wc[:, sl], (((0,), (0,)), ((), ())), preferred_element_type=F32)
        st_s[:, sl] = st_s[:, sl] * dlast_x[:, sl] + upd

    @pl.when(last)
    def _():
        st_ref[0] = st_s[...].T

    _ssd_finish(y_s[...], xs, z_ref, dexp_ref, nwm_ref, mix_ref)


def _mixer_prompt(proj, dt, prm, *, nb, nc):
    rows = CHUNK
    t = nb * nc * rows
    row = lambda b, c: b * nc + c
    full = lambda shape: pl.BlockSpec(shape, lambda b, c: (0,) * len(shape))
    return pl.pallas_call(
        _mixer_prompt_body,
        grid=(nb, nc),
        in_specs=[
            pl.BlockSpec((rows, GATE_WIDTH), lambda b, c: (row(b, c), 0)),
            pl.BlockSpec((rows, GATE_WIDTH), lambda b, c: (row(b, c), 1)),
            pl.BlockSpec((rows, SSM_WIDTH), lambda b, c: (row(b, c), 2)),
            pl.BlockSpec((rows, SSM_CONV_DIM), lambda b, c: (row(b, c), 2)),
            pl.BlockSpec((rows, LANES), lambda b, c: (row(b, c), 0)),
            full((1, GATE_WIDTH)), full((1, GATE_WIDTH)),
            full((GATE_HEADS, rows, rows)), full((rows, GATE_HEADS)),
            full((SSM_CONV, SSM_CONV_DIM)), full((1, SSM_CONV_DIM)),
            full((1, LANES)), full((1, LANES)), full((1, SSM_WIDTH)), full((1, SSM_WIDTH)),
            full((LANES, SSM_WIDTH)),
        ],
        out_specs=[
            pl.BlockSpec((rows, 2 * GATE_WIDTH), lambda b, c: (row(b, c), 0)),
            pl.BlockSpec((1, rows, GATE_WIDTH), lambda b, c: (b, 0, 0)),
            pl.BlockSpec((1, SSM_WIDTH, SSM_STATE), lambda b, c: (b, 0, 0)),
            pl.BlockSpec((1, SSM_CONV - 1, SSM_CONV_DIM), lambda b, c: (b, 0, 0)),
        ],
        out_shape=[
            jax.ShapeDtypeStruct((t, 2 * GATE_WIDTH), BF16),
            jax.ShapeDtypeStruct((nb, rows, GATE_WIDTH), F32),
            jax.ShapeDtypeStruct((nb, SSM_WIDTH, SSM_STATE), F32),
            jax.ShapeDtypeStruct((nb, SSM_CONV - 1, SSM_CONV_DIM), F32),
        ],
        scratch_shapes=[
            pltpu.VMEM((GATE_HEADS, rows, rows), BF16),
            pltpu.VMEM(((SSM_CONV - 1) * rows, 2 * rows), BF16),
            pltpu.VMEM((2 * rows, SSM_CONV_DIM), BF16),
            pltpu.VMEM((SSM_STATE, SSM_WIDTH), F32),
            pltpu.VMEM((rows, SSM_WIDTH), F32),
        ],
        compiler_params=_cparams(("arbitrary", "arbitrary")),
        name="mixer_prompt",
    )(proj, proj, proj, proj, dt, prm["ln_g"], prm["ln_b"], prm["wt_p"], prm["btT_p"], prm["conv_w"], prm["conv_b"],
      prm["dtb"], prm["alog"], prm["dexp"], prm["nwm"], prm["e"])


SEQ_TILE = 8


def _state_decay_body(dt_ref, dtb_ref, alog_ref, o_ref, *, seg):
    nseq = o_ref.shape[0]
    a = -jnp.exp(alog_ref[...])
    tot = jnp.zeros(o_ref.shape, F32)
    for t in range(seg):
        d = jax.nn.softplus(dt_ref[pl.ds(t, nseq, stride=seg), :] + dtb_ref[...])
        tot = tot + d * a
    o_ref[...] = jnp.exp(tot)


def _state_decay(dt, prm, *, seg):
    nseq = dt.shape[0] // seg
    return pl.pallas_call(
        functools.partial(_state_decay_body, seg=seg),
        out_shape=jax.ShapeDtypeStruct((nseq, LANES), F32),
        name="state_decay",
    )(dt, prm["dtb"], prm["alog"])


def _mixer_sample_body(dec_ref, gu_ref, gv_ref, z_ref, xbc_ref, dt_ref, prev_ref, sin_ref,
                       lng_ref, lnb_ref, wt_ref, btT_ref, cw_ref, cb_ref,
                       dtb_ref, alog_ref, dexp_ref, nwm_ref, e_ref,
                       mix_ref, cv_ref, sout_ref, sc_ref,
                       wm_s, y_s, cm_s, bm_s, wct_s, ex_s, *, seg):
    i = pl.program_id(0)
    rows = SEQ_TILE * seg
    mask, same = _seg_masks(rows, seg)

    @pl.when(i == 0)
    def _():
        for h in range(GATE_HEADS):
            wm_s[h] = jnp.where(mask, wt_ref[h], 0.0).astype(BF16)

    cv_ref[...] = _gate_tile(gu_ref, gv_ref, lng_ref, lnb_ref, wm_s, btT_ref, mix_ref)

    x3 = xbc_ref[...].astype(F32).reshape(SEQ_TILE, seg, SSM_CONV_DIM)
    tpos = lax.broadcasted_iota(jnp.int32, x3.shape, 1)
    conv = cb_ref[...] + cw_ref[SSM_CONV - 1:SSM_CONV, :] * x3
    for d in range(1, SSM_CONV):
        shifted = _seq_shift(x3, prev_ref, slice(0, SEQ_TILE), d, tpos)
        conv = conv + cw_ref[SSM_CONV - 1 - d:SSM_CONV - d, :] * shifted
    act = jax.nn.silu(conv).reshape(rows, SSM_CONV_DIM)
    sc_ref[...] = x3[:, seg - (SSM_CONV - 1):, :]

    maskf = mask.astype(BF16)
    segf = same.astype(BF16)
    xs, bm, cm, dt, cs, cl = _ssd_token_level(act, dt_ref[...], dtb_ref, alog_ref, maskf, segf)
    cs_t = cs.T
    dt_t = dt.T
    e_bf = e_ref[...]
    coef_x = _expand_heads(dt * jnp.exp(cl - cs), e_bf)
    ecs_x = _expand_heads(jnp.exp(cs), e_bf)
    lane = lax.broadcasted_iota(jnp.int32, (rows, LANES), 1)
    xs_b = xs.astype(BF16)
    for g in range(SSM_GROUPS):
        cg = cm[:, g * SSM_STATE:(g + 1) * SSM_STATE]
        bg = bm[:, g * SSM_STATE:(g + 1) * SSM_STATE]
        cb = lax.dot_general(cg.astype(BF16), bg.astype(BF16), (((1,), (1,)), ((), ())), preferred_element_type=F32)
        for p in range(HEADS_PER_GROUP // 2):
            h0 = g * HEADS_PER_GROUP + 2 * p
            sl = slice((h0 // 2) * LANES, (h0 // 2 + 1) * LANES)
            ys = [_dot(_ssd_diag_pair(cb, cs, cs_t, dt_t, mask, h).astype(BF16), xs_b[:, sl]) for h in (h0, h0 + 1)]
            y_s[:, sl] = jnp.where(lane < SSM_HEAD_DIM, ys[0], ys[1])
    cm_s[...] = cm
    bm_s[...] = bm
    wct_s[...] = (xs * coef_x).T
    ex_s[...] = ecs_x

    rowid = lax.broadcasted_iota(jnp.int32, (rows, SSM_STATE), 0)

    for s in range(SEQ_TILE):
        r8 = slice(s * seg, (s + 1) * seg)
        for g in range(SSM_GROUPS):
            gsl = slice(g * GROUP_WIDTH, (g + 1) * GROUP_WIDTH)
            nsl = slice(g * SSM_STATE, (g + 1) * SSM_STATE)
            st = sin_ref[s, gsl, :]
            c8 = cm_s[r8, nsl].astype(BF16)
            yo = lax.dot_general(c8, st.astype(BF16), (((1,), (1,)), ((), ())), preferred_element_type=F32)
            y_s[r8, gsl] = y_s[r8, gsl] + yo * ex_s[r8, gsl]
            bmask = jnp.where(rowid // seg == s, bm_s[:, nsl], 0.0).astype(BF16)
            upd = _dot(wct_s[gsl, :].astype(BF16), bmask)
            for r in range(HEADS_PER_GROUP):
                d = dec_ref[(i * SEQ_TILE + s) * SSM_HEADS + g * HEADS_PER_GROUP + r]
                hsl = slice(r * SSM_HEAD_DIM, (r + 1) * SSM_HEAD_DIM)
                osl = slice(g * GROUP_WIDTH + r * SSM_HEAD_DIM, g * GROUP_WIDTH + (r + 1) * SSM_HEAD_DIM)
                sout_ref[s, osl, :] = st[hsl, :] * d + upd[hsl, :]

    _ssd_finish(y_s[...], xs, z_ref, dexp_ref, nwm_ref, mix_ref)


def _mixer_sample(dec, proj, dt, prev, state, prm, *, seg):
    rows = SEQ_TILE * seg
    t = proj.shape[0]
    nseq = t // seg
    assert t % rows == 0
    full = lambda shape: pl.BlockSpec(shape, lambda i: (0,) * len(shape))
    return pl.pallas_call(
        functools.partial(_mixer_sample_body, seg=seg),
        grid=(t // rows,),
        in_specs=[
            pl.BlockSpec(memory_space=pltpu.SMEM),
            pl.BlockSpec((rows, GATE_WIDTH), lambda i: (i, 0)),
            pl.BlockSpec((rows, GATE_WIDTH), lambda i: (i, 1)),
            pl.BlockSpec((rows, SSM_WIDTH), lambda i: (i, 2)),
            pl.BlockSpec((rows, SSM_CONV_DIM), lambda i: (i, 2)),
            pl.BlockSpec((rows, LANES), lambda i: (i, 0)),
            pl.BlockSpec((SEQ_TILE, SSM_CONV - 1, SSM_CONV_DIM), lambda i: (i, 0, 0)),
            pl.BlockSpec((SEQ_TILE, SSM_WIDTH, SSM_STATE), lambda i: (i, 0, 0)),
            full((1, GATE_WIDTH)), full((1, GATE_WIDTH)),
            full((GATE_HEADS, rows, rows)), full((rows, GATE_HEADS)),
            full((SSM_CONV, SSM_CONV_DIM)), full((1, SSM_CONV_DIM)),
            full((1, LANES)), full((1, LANES)), full((1, SSM_WIDTH)), full((1, SSM_WIDTH)),
            full((LANES, SSM_WIDTH)),
        ],
        out_specs=[
            pl.BlockSpec((rows, 2 * GATE_WIDTH), lambda i: (i, 0)),
            pl.BlockSpec((rows, GATE_WIDTH), lambda i: (i, 0)),
            pl.BlockSpec((SEQ_TILE, SSM_WIDTH, SSM_STATE), lambda i: (i, 0, 0)),
            pl.BlockSpec((SEQ_TILE, SSM_CONV - 1, SSM_CONV_DIM), lambda i: (i, 0, 0)),
        ],
        out_shape=[
            jax.ShapeDtypeStruct((t, 2 * GATE_WIDTH), BF16),
            jax.ShapeDtypeStruct((t, GATE_WIDTH), F32),
            jax.ShapeDtypeStruct((nseq, SSM_WIDTH, SSM_STATE), F32),
            jax.ShapeDtypeStruct((nseq, SSM_CONV - 1, SSM_CONV_DIM), F32),
        ],
        scratch_shapes=[
            pltpu.VMEM((GATE_HEADS, rows, rows), BF16),
            pltpu.VMEM((rows, SSM_WIDTH), F32),
            pltpu.VMEM((rows, GROUP_WIDTH), F32),
            pltpu.VMEM((rows, GROUP_WIDTH), F32),
            pltpu.VMEM((SSM_WIDTH, rows), F32),
            pltpu.VMEM((rows, SSM_WIDTH), F32),
        ],
        compiler_params=_cparams(("arbitrary",)),
        name="mixer_sample",
    )(dec, proj, proj, proj, proj, dt, prev, state, prm["ln_g"], prm["ln_b"], prm["wt_s"], prm["btT_s"],
      prm["conv_w"], prm["conv_b"], prm["dtb"], prm["alog"], prm["dexp"], prm["nwm"], prm["e"])


ROW_SUB = 128


def _row_subs(tm, sub=ROW_SUB):
    sub = min(sub, tm)
    assert tm % sub == 0
    return [slice(r * sub, (r + 1) * sub) for r in range(tm // sub)]


def _out_proj_body(m_ref, x_ref, w_ref, npost_ref, npre_ref, x1_ref, h2_ref):
    for rs in _row_subs(m_ref.shape[0]):
        mix = _dot(m_ref[rs, :], w_ref[...])
        x1 = x_ref[rs, :] + _rms(mix, npost_ref[...])
        x1_ref[rs, :] = x1
        h2_ref[rs, :] = _rms(x1, npre_ref[...]).astype(h2_ref.dtype)


def _out_proj(mixin, x2d, w_out, npost, npre, *, tm):
    t, k = mixin.shape
    assert t % tm == 0
    return pl.pallas_call(
        _out_proj_body,
        grid=(t // tm,),
        in_specs=[
            pl.BlockSpec((tm, k), lambda i: (i, 0)),
            pl.BlockSpec((tm, D_MODEL), lambda i: (i, 0)),
            pl.BlockSpec((k, D_MODEL), lambda i: (0, 0), pipeline_mode=pl.Buffered(1)),
            pl.BlockSpec((1, D_MODEL), lambda i: (0, 0)),
            pl.BlockSpec((1, D_MODEL), lambda i: (0, 0)),
        ],
        out_specs=[
            pl.BlockSpec((tm, D_MODEL), lambda i: (i, 0)),
            pl.BlockSpec((tm, D_MODEL), lambda i: (i, 0)),
        ],
        out_shape=[
            jax.ShapeDtypeStruct((t, D_MODEL), F32),
            jax.ShapeDtypeStruct((t, D_MODEL), BF16),
        ],
        compiler_params=_cparams(("arbitrary",)),
        name="out_proj",
    )(mixin, x2d, w_out, npost, npre)


def _ffn_conv_taps(cur, shifted, cw_ref, cb_ref):
    out = cb_ref[...] + cw_ref[FFN_CONV - 1:FFN_CONV, :] * cur
    for d in range(1, FFN_CONV):
        out = out + cw_ref[FFN_CONV - 1 - d:FFN_CONV - d, :] * shifted[d - 1]
    return out


def _ffn_up_prompt_body(h_ref, wg_ref, wu_ref, cwg_ref, cwu_ref, cbg_ref, cbu_ref, *rest, tiles_per_seq, n_cast):
    cast_in, (a_ref, stg_ref, stu_ref) = rest[:n_cast], rest[n_cast:n_cast + 3]
    cast_out, (tg_s, tu_s) = rest[n_cast + 3:2 * n_cast + 3], rest[2 * n_cast + 3:]
    i = pl.program_id(1)
    tm = h_ref.shape[0]
    _side_cast(cast_in, cast_out)

    @pl.when((i % tiles_per_seq) == 0)
    def _():
        for t_s in (tg_s, tu_s):
            t_s[...] = jnp.zeros(t_s.shape, F32)

    tails = [tg_s[...], tu_s[...]]
    row8 = lax.broadcasted_iota(jnp.int32, tg_s.shape, 0)
    for rs in _row_subs(tm, 128):
        h = h_ref[rs, :]
        convs = []
        for k, (w_ref, cw_ref, cb_ref) in enumerate(((wg_ref, cwg_ref, cbg_ref), (wu_ref, cwu_ref, cbu_ref))):
            x = _dot(h, w_ref[...])
            shifted = []
            for d in range(1, FFN_CONV):
                r = pltpu.roll(x, d, axis=0)
                head = jnp.where(row8 >= d, r[0:SUBLANES], pltpu.roll(tails[k], d, axis=0))
                shifted.append(jnp.concatenate([head, r[SUBLANES:]], axis=0))
            convs.append(_ffn_conv_taps(x, shifted, cw_ref, cb_ref))
            tails[k] = x[x.shape[0] - SUBLANES:]
        a_ref[rs, :] = (jax.nn.gelu(convs[0], approximate=True) * convs[1]).astype(a_ref.dtype)
    for tail, t_s, st_ref in zip(tails, (tg_s, tu_s), (stg_ref, stu_ref)):
        t_s[...] = tail
        st_ref[0] = tail[SUBLANES - (FFN_CONV - 1):]


def _ffn_up_sample_body(h_ref, wg_ref, wu_ref, cwg_ref, cwu_ref, cbg_ref, cbu_ref, pg_ref, pu_ref,
                        a_ref, stg_ref, stu_ref, *, seg):
    tm = h_ref.shape[0]
    tn = wg_ref.shape[1]
    for rs in _row_subs(tm):
        n = rs.stop - rs.start
        nseq = n // seg
        sq = slice(rs.start // seg, rs.stop // seg)
        h = h_ref[rs, :]
        tpos = lax.broadcasted_iota(jnp.int32, (nseq, seg, tn), 1)
        convs = []
        for w_ref, p_ref, cw_ref, cb_ref, st_ref in ((wg_ref, pg_ref, cwg_ref, cbg_ref, stg_ref),
                                                     (wu_ref, pu_ref, cwu_ref, cbu_ref, stu_ref)):
            x3 = _dot(h, w_ref[...]).reshape(nseq, seg, tn)
            shifted = [_seq_shift(x3, p_ref, sq, d, tpos) for d in range(1, FFN_CONV)]
            convs.append(_ffn_conv_taps(x3, shifted, cw_ref, cb_ref))
            st_ref[sq] = x3[:, seg - (FFN_CONV - 1):, :]
        act = jax.nn.gelu(convs[0], approximate=True) * convs[1]
        a_ref[rs, :] = act.reshape(n, tn).astype(a_ref.dtype)


def _ffn_up(h2, w_up, cw, cb, prev, *, nseq, seg, tm, tn, cast=()):
    t = h2.shape[0]
    nj, ni = D_FF // tn, t // tm
    assert t % tm == 0 and D_FF % tn == 0 and t == nseq * seg
    common_in = [
        pl.BlockSpec((tm, D_MODEL), lambda j, i: (i, 0)),
        pl.BlockSpec((D_MODEL, tn), lambda j, i: (0, j)),
        pl.BlockSpec((D_MODEL, tn), lambda j, i: (0, j + nj)),
        pl.BlockSpec((FFN_CONV, tn), lambda j, i: (0, j)),
        pl.BlockSpec((FFN_CONV, tn), lambda j, i: (0, j + nj)),
        pl.BlockSpec((1, tn), lambda j, i: (0, j)),
        pl.BlockSpec((1, tn), lambda j, i: (0, j + nj)),
    ]
    a_spec = pl.BlockSpec((tm, tn), lambda j, i: (i, j))
    st_shape = jax.ShapeDtypeStruct((nseq, FFN_CONV - 1, D_FF), F32)
    out_shape = [jax.ShapeDtypeStruct((t, D_FF), BF16), st_shape, st_shape]
    if prev is None:
        assert seg % tm == 0
        tps = seg // tm
        c_in, c_out, c_shapes = _side_cast_specs(cast, nj * ni, lambda j, i: j * ni + i)
        return pl.pallas_call(
            functools.partial(_ffn_up_prompt_body, tiles_per_seq=tps, n_cast=len(cast)),
            grid=(nj, ni),
            in_specs=common_in + c_in,
            out_specs=[a_spec] + [pl.BlockSpec((1, FFN_CONV - 1, tn), lambda j, i: (i // tps, 0, j))] * 2 + c_out,
            out_shape=out_shape + c_shapes,
            scratch_shapes=[pltpu.VMEM((SUBLANES, tn), F32), pltpu.VMEM((SUBLANES, tn), F32)],
            compiler_params=_cparams(("arbitrary", "arbitrary")),
            name="ffn_up_prompt",
        )(h2, w_up, w_up, cw, cw, cb, cb, *cast)
    assert tm % seg == 0 and seg == SUBLANES and not cast
    return pl.pallas_call(
        functools.partial(_ffn_up_sample_body, seg=seg),
        grid=(nj, ni),
        in_specs=common_in + [
            pl.BlockSpec((tm // seg, FFN_CONV - 1, tn), lambda j, i: (i, 0, j)),
            pl.BlockSpec((tm // seg, FFN_CONV - 1, tn), lambda j, i: (i, 0, j + nj)),
        ],
        out_specs=[a_spec] + [pl.BlockSpec((tm // seg, FFN_CONV - 1, tn), lambda j, i: (i, 0, j))] * 2,
        out_shape=out_shape,
        compiler_params=_cparams(("arbitrary", "arbitrary")),
        name="ffn_up_sample",
    )(h2, w_up, w_up, cw, cw, cb, cb, prev, prev)


def _ffn_down_body(a_ref, x_ref, w_ref, nw_ref, y_ref):
    for rs in _row_subs(a_ref.shape[0], 128):
        f = _dot(a_ref[rs, :], w_ref[...])
        y_ref[rs, :] = x_ref[rs, :] + _rms(f, nw_ref[...])


def _ffn_down(act, x1, w_down, nw, *, tm):
    t, k = act.shape
    assert t % tm == 0
    return pl.pallas_call(
        _ffn_down_body,
        grid=(t // tm,),
        in_specs=[
            pl.BlockSpec((tm, k), lambda i: (i, 0)),
            pl.BlockSpec((tm, D_MODEL), lambda i: (i, 0)),
            pl.BlockSpec((k, D_MODEL), lambda i: (0, 0), pipeline_mode=pl.Buffered(1)),
            pl.BlockSpec((1, D_MODEL), lambda i: (0, 0)),
        ],
        out_specs=pl.BlockSpec((tm, D_MODEL), lambda i: (i, 0)),
        out_shape=jax.ShapeDtypeStruct((t, D_MODEL), F32),
        compiler_params=_cparams(("arbitrary",)),
        name="ffn_down",
    )(act, x1, w_down, nw)


def _head_expander():
    e = np.zeros((LANES, SSM_WIDTH), np.float32)
    for h in range(SSM_HEADS):
        e[h, h * SSM_HEAD_DIM:(h + 1) * SSM_HEAD_DIM] = 1.0
    return jnp.asarray(e, BF16)


def _pad_lanes(v):
    return jnp.pad(v, (0, LANES - v.shape[0]))[None, :]


def _prep_params(norm_mix_pre, w_in, gate_ln_g, gate_ln_b, gate_w_s, gate_b_s, ssm_conv_w, ssm_conv_b, ssm_dt_bias,
                 ssm_a_log, ssm_d, ssm_norm_w, w_out, norm_mix_post, norm_ffn_pre, ffn_w_up, ffn_conv_w, ffn_conv_b,
                 ffn_w_down, norm_ffn_post, seg_sample):
    w_in_t = jnp.swapaxes(w_in, 0, 1)
    ws_small = gate_w_s[:, :seg_sample, :seg_sample]
    return dict(
        nw_pre=norm_mix_pre[None, :],
        w_in_t=w_in_t,
        w_dt=jnp.pad(w_in_t[PROJ_MAIN:, :], ((0, LANES - SSM_HEADS), (0, 0))).astype(BF16),
        ln_g=gate_ln_g[None, :], ln_b=gate_ln_b[None, :],
        wt_p=gate_w_s, btT_p=gate_b_s.T,
        wt_s=jnp.tile(ws_small, (1, SEQ_TILE, SEQ_TILE)), btT_s=jnp.tile(gate_b_s[:, :seg_sample], (1, SEQ_TILE)).T,
        conv_w=ssm_conv_w, conv_b=ssm_conv_b[None, :],
        dtb=_pad_lanes(ssm_dt_bias), alog=_pad_lanes(ssm_a_log),
        dexp=jnp.repeat(ssm_d, SSM_HEAD_DIM)[None, :], nwm=ssm_norm_w[None, :],
        e=_head_expander(),
        w_out=w_out, n_post=norm_mix_post[None, :], n_pre2=norm_ffn_pre[None, :],
        w_up=ffn_w_up, fcw=ffn_conv_w, fcb=ffn_conv_b[None, :],
        w_down=ffn_w_down, n_post2=norm_ffn_post[None, :],
    )


def _row_tile(t, cap):
    tm = cap
    while t % tm:
        tm //= 2
    assert tm >= 64
    return tm


TM_STREAM = 1024
TM_OUT_PROJ = 512
TM_FFN_DOWN = 512
TM_PRE_NORM = 1024


def _layer_prompt(x, prm):
    nb, seq, _ = x.shape
    assert seq % CHUNK == 0
    nc = seq // CHUNK
    x2d = x.reshape(nb * seq, D_MODEL)
    tm = _row_tile(seq, TM_STREAM)
    h, dt = _pre_norm(x2d, prm["nw_pre"], prm["w_dt"], tm=_row_tile(seq, TM_PRE_NORM))
    proj, w_out, w_up, w_in_b = _in_proj(h, prm["w_in_t"], tm=tm, tn=1024, sub=min(tm, 256),
                                         cast=(prm["w_out"], prm["w_up"]))
    mixin, cv, st, sc = _mixer_prompt(proj, dt, prm, nb=nb, nc=nc)
    x1, h2 = _out_proj(mixin, x2d, w_out, prm["n_post"], prm["n_pre2"], tm=_row_tile(seq, TM_OUT_PROJ))
    act, fst_g, fst_u, w_down = _ffn_up(h2, w_up, prm["fcw"], prm["fcb"], None, nseq=nb, seg=seq, tm=tm, tn=512,
                                        cast=(prm["w_down"],))
    y = _ffn_down(act, x1, w_down, prm["n_post2"], tm=_row_tile(seq, TM_FFN_DOWN))
    return dict(w_in_t=w_in_b, w_out=w_out, w_up=w_up, w_down=w_down), (
            y.reshape(nb, seq, D_MODEL),
            st.reshape(nb, SSM_HEADS, SSM_HEAD_DIM, SSM_STATE),
            sc,
            jnp.concatenate([fst_g, fst_u], axis=-1),
            cv.reshape(nb, CHUNK, GATE_HEADS, GATE_HEAD_DIM))


def _layer_sample(x, state_ssm, state_sconv, state_fconv, prm, wb):
    nb, seg, _ = x.shape
    assert seg == SUBLANES and nb % SEQ_TILE == 0
    t = nb * seg
    x2d = x.reshape(t, D_MODEL)
    tm = _row_tile(t, TM_STREAM)
    h, dt = _pre_norm(x2d, prm["nw_pre"], prm["w_dt"], tm=_row_tile(t, TM_PRE_NORM))
    proj, = _in_proj(h, wb["w_in_t"], tm=tm, tn=1024, sub=min(tm, 256))
    dec = _state_decay(dt, prm, seg=seg)[:, :SSM_HEADS].reshape(nb * SSM_HEADS)
    mixin, cv, st, sc = _mixer_sample(dec, proj, dt, state_sconv,
                                      state_ssm.reshape(nb, SSM_WIDTH, SSM_STATE), prm, seg=seg)
    x1, h2 = _out_proj(mixin, x2d, wb["w_out"], prm["n_post"], prm["n_pre2"], tm=_row_tile(t, TM_OUT_PROJ))
    act, fst_g, fst_u = _ffn_up(h2, wb["w_up"], prm["fcw"], prm["fcb"], state_fconv, nseq=nb, seg=seg, tm=tm, tn=512)
    y = _ffn_down(act, x1, wb["w_down"], prm["n_post2"], tm=_row_tile(t, TM_FFN_DOWN))
    return (y.reshape(nb, seg, D_MODEL),
            st.reshape(nb, SSM_HEADS, SSM_HEAD_DIM, SSM_STATE),
            sc,
            jnp.concatenate([fst_g, fst_u], axis=-1),
            cv.reshape(nb, seg, GATE_HEADS, GATE_HEAD_DIM))


def kernel(x_prompt, x_sample, state_ssm, state_ssm_conv, state_ffn_conv, norm_mix_pre, w_in, gate_ln_g, gate_ln_b,
           gate_w_s, gate_b_s, ssm_conv_w, ssm_conv_b, ssm_dt_bias, ssm_a_log, ssm_d, ssm_norm_w, w_out,
           norm_mix_post, norm_ffn_pre, ffn_w_up, ffn_conv_w, ffn_conv_b, ffn_w_down, norm_ffn_post):
    depth = w_in.shape[0]
    yp, ys = x_prompt, x_sample
    outs_p, outs_s = [], []
    for l in range(depth):
        prm = _prep_params(norm_mix_pre[l], w_in[l], gate_ln_g[l], gate_ln_b[l], gate_w_s[l], gate_b_s[l],
                           ssm_conv_w[l], ssm_conv_b[l], ssm_dt_bias[l], ssm_a_log[l], ssm_d[l], ssm_norm_w[l],
                           w_out[l], norm_mix_post[l], norm_ffn_pre[l], ffn_w_up[l], ffn_conv_w[l], ffn_conv_b[l],
                           ffn_w_down[l], norm_ffn_post[l], x_sample.shape[1])
        wb, (yp, *rest_p) = _layer_prompt(yp, prm)
        ys, *rest_s = _layer_sample(ys, state_ssm[l], state_ssm_conv[l], state_ffn_conv[l], prm, wb)
        outs_p.append(rest_p)
        outs_s.append(rest_s)
    stack = lambda outs, k: jnp.stack([o[k] for o in outs])
    return (yp, ys,
            stack(outs_p, 0), stack(outs_p, 1), stack(outs_p, 2), stack(outs_p, 3),
            stack(outs_s, 0), stack(outs_s, 1), stack(outs_s, 2), stack(outs_s, 3))
```

```python
import functools

import jax
import jax.numpy as jnp
import numpy as np
from jax import lax
from jax.experimental import pallas as pl
from jax.experimental.pallas import tpu as pltpu

F32 = jnp.float32
BF16 = jnp.bfloat16

D_MODEL = 2048
GATE_WIDTH = 2048
GATE_HEADS = 16
GATE_HEAD_DIM = 128
CHUNK = 128
SSM_WIDTH = 2048
SSM_HEAD_DIM = 64
SSM_HEADS = 32
SSM_GROUPS = 4
SSM_STATE = 128
SSM_CONV = 4
SSM_CONV_DIM = SSM_WIDTH + 2 * SSM_GROUPS * SSM_STATE
PROJ_GATE = 2 * GATE_WIDTH + SSM_WIDTH
PROJ_MAIN = PROJ_GATE + SSM_CONV_DIM
D_FF = 5632
FFN_CONV = 3
EPS = 1e-6
HEADS_PER_GROUP = SSM_HEADS // SSM_GROUPS
GROUP_WIDTH = SSM_WIDTH // SSM_GROUPS

LANES = 128
SUBLANES = 8
VMEM_LIMIT = 56 * 1024 * 1024


def _cparams(sem):
    return pltpu.CompilerParams(dimension_semantics=sem, vmem_limit_bytes=VMEM_LIMIT)


def _rms(x, w):
    return x * lax.rsqrt(jnp.mean(x * x, axis=-1, keepdims=True) + EPS) * w


def _gelu_erf(x):
    return 0.5 * x * (1.0 + lax.erf(x * np.float32(0.7071067811865476)))


def _split_bf16(x, n):
    parts = []
    r = x
    for k in range(n):
        p = r.astype(BF16)
        parts.append(p)
        if k + 1 < n:
            r = r - p.astype(F32)
    return parts


def _dot(a, b):
    return jnp.dot(a, b, preferred_element_type=F32)


def _dot_nt(a, b):
    return lax.dot_general(a, b, (((1,), (1,)), ((), ())), preferred_element_type=F32)


def _mm_split(m_bf, x, n):
    acc = None
    for p in _split_bf16(x, n):
        t = _dot(m_bf, p)
        acc = t if acc is None else acc + t
    return acc


def _expand_heads(x, e_bf):
    acc = None
    for p in _split_bf16(x, 2):
        t = _dot(p, e_bf)
        acc = t if acc is None else acc + t
    return acc


def _seq_shift(x3, p_ref, sq, d, tpos):
    k1 = p_ref.shape[1]
    out = pltpu.roll(x3, d, axis=1)
    for t in range(d):
        out = jnp.where(tpos == t, jnp.broadcast_to(p_ref[sq, k1 + t - d:k1 + t - d + 1, :], x3.shape), out)
    return out


def _seg_masks(rows, seg):
    r = lax.broadcasted_iota(jnp.int32, (rows, rows), 0)
    c = lax.broadcasted_iota(jnp.int32, (rows, rows), 1)
    same = (r // seg) == (c // seg)
    return same & (c <= r), same


BF16_ROWS = 2 * SUBLANES


def _cast_rows(rows, nsteps):
    per = BF16_ROWS
    while rows % per or rows // per > nsteps:
        per += BF16_ROWS
        assert per <= rows
    return per


def _side_cast_specs(weights, nsteps, step_of):
    ins, outs, shapes = [], [], []
    for w in weights:
        rows, cols = w.shape
        per = _cast_rows(rows, nsteps)
        spec = pl.BlockSpec((per, cols), lambda *g, nb=rows // per: (jnp.minimum(step_of(*g), nb - 1), 0))
        ins.append(spec)
        outs.append(spec)
        shapes.append(jax.ShapeDtypeStruct((rows, cols), BF16))
    return ins, outs, shapes


def _side_cast(cast_in, cast_out):
    for w_ref, o_ref in zip(cast_in, cast_out):
        o_ref[...] = w_ref[...].astype(BF16)


def _pre_norm_body(x_ref, nw_ref, wdt_ref, h_ref, dt_ref):
    h = _rms(x_ref[...], nw_ref[...]).astype(BF16)
    h_ref[...] = h
    dt_ref[...] = _dot_nt(h, wdt_ref[...])


def _pre_norm(x2d, nw, w_dt, *, tm):
    t = x2d.shape[0]
    assert t % tm == 0
    return pl.pallas_call(
        _pre_norm_body,
        grid=(t // tm,),
        in_specs=[
            pl.BlockSpec((tm, D_MODEL), lambda i: (i, 0)),
            pl.BlockSpec((1, D_MODEL), lambda i: (0, 0)),
            pl.BlockSpec((LANES, D_MODEL), lambda i: (0, 0)),
        ],
        out_specs=[
            pl.BlockSpec((tm, D_MODEL), lambda i: (i, 0)),
            pl.BlockSpec((tm, LANES), lambda i: (i, 0)),
        ],
        out_shape=[
            jax.ShapeDtypeStruct((t, D_MODEL), BF16),
            jax.ShapeDtypeStruct((t, LANES), F32),
        ],
        compiler_params=_cparams(("arbitrary",)),
        name="pre_norm",
    )(x2d, nw, w_dt)


def _in_proj_body(h_ref, w_ref, *rest, n_gelu, sub, n_cast, cast_w):
    cast_in, o_ref, cast_out, wb_s = rest[:n_cast], rest[n_cast], rest[n_cast + 1:2 * n_cast + 1], rest[-1]
    j = pl.program_id(0)
    i = pl.program_id(1)
    tm = h_ref.shape[0]
    _side_cast(cast_in, cast_out)

    @pl.when(i == 0)
    def _():
        wb_s[...] = w_ref[...].astype(BF16)
        if cast_w:
            rest[-2][...] = wb_s[...]

    def run(epilogue):
        for rs in _row_subs(tm, sub):
            o_ref[rs, :] = epilogue(_dot_nt(h_ref[rs, :], wb_s[...])).astype(o_ref.dtype)

    @pl.when(j < n_gelu)
    def _():
        run(_gelu_erf)

    @pl.when(j >= n_gelu)
    def _():
        run(jax.nn.silu)


def _in_proj(h, w_in_t, *, tm, tn, sub, cast=()):
    t = h.shape[0]
    assert t % tm == 0 and GATE_WIDTH % tn == 0 and SSM_WIDTH % tn == 0 and tm % sub == 0
    nj, ni = PROJ_GATE // tn, t // tm
    c_in, c_out, c_shapes = _side_cast_specs(cast, nj * ni, lambda j, i: j * ni + i)
    w_spec = pl.BlockSpec((tn, D_MODEL), lambda j, i: (j, 0))
    cast_w = w_in_t.dtype != BF16
    return pl.pallas_call(
        functools.partial(_in_proj_body, n_gelu=2 * GATE_WIDTH // tn, sub=sub, n_cast=len(cast), cast_w=cast_w),
        grid=(nj, ni),
        in_specs=[pl.BlockSpec((tm, D_MODEL), lambda j, i: (i, 0)), w_spec] + c_in,
        out_specs=[pl.BlockSpec((tm, tn), lambda j, i: (i, j))] + c_out + [w_spec] * cast_w,
        out_shape=([jax.ShapeDtypeStruct((t, PROJ_GATE), BF16)] + c_shapes
                   + [jax.ShapeDtypeStruct((PROJ_GATE, D_MODEL), BF16)] * cast_w),
        scratch_shapes=[pltpu.VMEM((tn, D_MODEL), BF16)],
        compiler_params=_cparams(("arbitrary", "arbitrary")),
        name="in_proj",
    )(h, w_in_t, *cast)


def _conv_silu(cur, shifted, cw_ref, cb_ref):
    out = cb_ref[...] + cw_ref[SSM_CONV - 1:SSM_CONV, :] * cur
    for d in range(1, SSM_CONV):
        out = out + cw_ref[SSM_CONV - 1 - d:SSM_CONV - d, :] * shifted[d - 1]
    return jax.nn.silu(out)


def _stage_weights(i, w_ref, wb_s, wout_ref):
    @pl.when(i == 0)
    def _():
        if wout_ref is not None:
            wout_ref[...] = w_ref[...].astype(BF16)
        wb_s[...] = w_ref[...].T.astype(BF16)


def _xbc_proj_prompt_body(h_ref, w_ref, cw_ref, cb_ref, o_ref, st_ref, *rest, tiles_per_seq, cast_w):
    wb_s, t_s = rest[-2:]
    i = pl.program_id(1)
    tm = h_ref.shape[0]
    _stage_weights(i, w_ref, wb_s, rest[0] if cast_w else None)

    @pl.when((i % tiles_per_seq) == 0)
    def _():
        t_s[...] = jnp.zeros(t_s.shape, F32)

    tail = t_s[...]
    row8 = lax.broadcasted_iota(jnp.int32, t_s.shape, 0)
    for rs in _row_subs(tm):
        x = _dot(h_ref[rs, :], wb_s[...])
        shifted = []
        for d in range(1, SSM_CONV):
            r = pltpu.roll(x, d, axis=0)
            head = jnp.where(row8 >= d, r[0:SUBLANES], pltpu.roll(tail, d, axis=0))
            shifted.append(jnp.concatenate([head, r[SUBLANES:]], axis=0))
        o_ref[rs, :] = _conv_silu(x, shifted, cw_ref, cb_ref).astype(o_ref.dtype)
        tail = x[x.shape[0] - SUBLANES:]
    t_s[...] = tail
    st_ref[0] = tail[SUBLANES - (SSM_CONV - 1):]


def _xbc_proj_sample_body(h_ref, w_ref, cw_ref, cb_ref, p_ref, o_ref, st_ref, *rest, seg, cast_w):
    wb_s = rest[-1]
    i = pl.program_id(1)
    tm = h_ref.shape[0]
    tn = w_ref.shape[0]
    _stage_weights(i, w_ref, wb_s, rest[0] if cast_w else None)
    for rs in _row_subs(tm):
        n = rs.stop - rs.start
        nseq = n // seg
        sq = slice(rs.start // seg, rs.stop // seg)
        x3 = _dot(h_ref[rs, :], wb_s[...]).reshape(nseq, seg, tn)
        tpos = lax.broadcasted_iota(jnp.int32, x3.shape, 1)
        shifted = [_seq_shift(x3, p_ref, sq, d, tpos) for d in range(1, SSM_CONV)]
        o_ref[rs, :] = _conv_silu(x3, shifted, cw_ref, cb_ref).reshape(n, tn).astype(o_ref.dtype)
        st_ref[sq] = x3[:, seg - (SSM_CONV - 1):, :]


def _xbc_proj(h, w_t, w_row0, cw, cb, prev, *, nseq, seg, tm, tn):
    t = h.shape[0]
    assert t % tm == 0 and SSM_CONV_DIM % tn == 0 and w_row0 % tn == 0 and t == nseq * seg
    nj, ni, j0 = SSM_CONV_DIM // tn, t // tm, w_row0 // tn
    cast_w = w_t.dtype != BF16
    in_specs = [
        pl.BlockSpec((tm, D_MODEL), lambda j, i: (i, 0)),
        pl.BlockSpec((tn, D_MODEL), lambda j, i: (j + j0, 0)),
        pl.BlockSpec((SSM_CONV, tn), lambda j, i: (0, j)),
        pl.BlockSpec((1, tn), lambda j, i: (0, j)),
    ]
    out_shape = [jax.ShapeDtypeStruct((t, SSM_CONV_DIM), BF16),
                 jax.ShapeDtypeStruct((nseq, SSM_CONV - 1, SSM_CONV_DIM), F32)]
    w_out_spec = [pl.BlockSpec((tn, D_MODEL), lambda j, i: (j, 0))] * cast_w
    w_out_shape = [jax.ShapeDtypeStruct((SSM_CONV_DIM, D_MODEL), BF16)] * cast_w
    wb_scratch = [pltpu.VMEM((D_MODEL, tn), BF16)]
    if prev is None:
        assert seg % tm == 0
        tps = seg // tm
        return pl.pallas_call(
            functools.partial(_xbc_proj_prompt_body, tiles_per_seq=tps, cast_w=cast_w),
            grid=(nj, ni),
            in_specs=in_specs,
            out_specs=[pl.BlockSpec((tm, tn), lambda j, i: (i, j)),
                       pl.BlockSpec((1, SSM_CONV - 1, tn), lambda j, i: (i // tps, 0, j))] + w_out_spec,
            out_shape=out_shape + w_out_shape,
            scratch_shapes=wb_scratch + [pltpu.VMEM((SUBLANES, tn), F32)],
            compiler_params=_cparams(("arbitrary", "arbitrary")),
            name="xbc_proj_prompt",
        )(h, w_t, cw, cb)
    assert tm % seg == 0 and seg == SUBLANES
    return pl.pallas_call(
        functools.partial(_xbc_proj_sample_body, seg=seg, cast_w=cast_w),
        grid=(nj, ni),
        in_specs=in_specs + [pl.BlockSpec((tm // seg, SSM_CONV - 1, tn), lambda j, i: (i, 0, j))],
        out_specs=[pl.BlockSpec((tm, tn), lambda j, i: (i, j)),
                   pl.BlockSpec((tm // seg, SSM_CONV - 1, tn), lambda j, i: (i, 0, j))] + w_out_spec,
        out_shape=out_shape + w_out_shape,
        scratch_shapes=wb_scratch,
        compiler_params=_cparams(("arbitrary", "arbitrary")),
        name="xbc_proj_sample",
    )(h, w_t, cw, cb, prev)


def _gate_tile(gu_ref, gv_ref, lng_ref, lnb_ref, wm_s, btT_ref, mix_ref):
    g = gv_ref[...].astype(F32)
    mu = jnp.mean(g, axis=-1, keepdims=True)
    xc = g - mu
    v = xc * lax.rsqrt(jnp.mean(xc * xc, axis=-1, keepdims=True) + EPS) * lng_ref[...] + lnb_ref[...]
    vb = v.astype(BF16)
    rows = v.shape[0]
    for h in range(GATE_HEADS):
        sl = slice(h * GATE_HEAD_DIM, (h + 1) * GATE_HEAD_DIM)
        s = _dot(wm_s[h], vb[:, sl]) + jnp.broadcast_to(btT_ref[:, h:h + 1], (rows, GATE_HEAD_DIM))
        mix_ref[:, sl] = gu_ref[:, sl] * s.astype(mix_ref.dtype)
    return v


def _ssd_token_level(act, dt_raw, dtb_ref, alog_ref, maskf, segf):
    xs = act[:, :SSM_WIDTH]
    bm = act[:, SSM_WIDTH:SSM_WIDTH + GROUP_WIDTH]
    cm = act[:, SSM_WIDTH + GROUP_WIDTH:]
    dt = jax.nn.softplus(dt_raw + dtb_ref[...])
    a = -jnp.exp(alog_ref[...])
    da = dt * a
    cs = _mm_split(maskf, da, 3)
    cl = _mm_split(segf, da, 3)
    return xs, bm, cm, dt, cs, cl


def _ssd_diag_pair(cb, cs, cs_t, dt_t, mask, h):
    seg = cs[:, h:h + 1] - cs_t[h:h + 1, :]
    decay = jnp.exp(jnp.where(mask, seg, -jnp.inf))
    return cb * decay * dt_t[h:h + 1, :]


def _ssd_finish(y, xs, z_ref, dexp_ref, nwm_ref, mix_ref):
    y = y + dexp_ref[...] * xs
    y = y * z_ref[...].astype(F32)
    for g in range(SSM_GROUPS):
        sl = slice(g * GROUP_WIDTH, (g + 1) * GROUP_WIDTH)
        yg = y[:, sl]
        yg = yg * lax.rsqrt(jnp.mean(yg * yg, axis=-1, keepdims=True) + EPS) * nwm_ref[:, sl]
        mix_ref[:, GATE_WIDTH + g * GROUP_WIDTH:GATE_WIDTH + (g + 1) * GROUP_WIDTH] = yg.astype(mix_ref.dtype)


def _mixer_prompt_body(gu_ref, gv_ref, z_ref, xbc_ref, dt_ref, lng_ref, lnb_ref, wt_ref, btT_ref,
                       dtb_ref, alog_ref, dexp_ref, nwm_ref, e_ref,
                       mix_ref, cv_ref, st_ref,
                       wm_s, st_s, y_s):
    b = pl.program_id(0)
    c = pl.program_id(1)
    last = c == pl.num_programs(1) - 1
    rows = CHUNK
    mask, same = _seg_masks(rows, rows)

    @pl.when((b == 0) & (c == 0))
    def _():
        for h in range(GATE_HEADS):
            wm_s[h] = jnp.where(mask, wt_ref[h], 0.0).astype(BF16)

    @pl.when(c == 0)
    def _():
        st_s[...] = jnp.zeros(st_s.shape, F32)

    v = _gate_tile(gu_ref, gv_ref, lng_ref, lnb_ref, wm_s, btT_ref, mix_ref)

    @pl.when(last)
    def _():
        cv_ref[0] = v

    act = xbc_ref[...].astype(F32)
    maskf = mask.astype(BF16)
    segf = same.astype(BF16)
    xs, bm, cm, dt, cs, cl = _ssd_token_level(act, dt_ref[...], dtb_ref, alog_ref, maskf, segf)
    cs_t = cs.T
    dt_t = dt.T
    ecs = jnp.exp(cs)
    e_bf = e_ref[...]
    coef_x = _expand_heads(dt * jnp.exp(cl - cs), e_bf)
    dlast_x = _expand_heads(jnp.exp(cl[0:SUBLANES, :]), e_bf)[0:1, :]
    lane = lax.broadcasted_iota(jnp.int32, (rows, LANES), 1)
    xs_b = xs.astype(BF16)
    for g in range(SSM_GROUPS):
        cg = cm[:, g * SSM_STATE:(g + 1) * SSM_STATE]
        bg = bm[:, g * SSM_STATE:(g + 1) * SSM_STATE]
        cb = lax.dot_general(cg.astype(BF16), bg.astype(BF16), (((1,), (1,)), ((), ())), preferred_element_type=F32)
        for p in range(HEADS_PER_GROUP // 2):
            h0 = g * HEADS_PER_GROUP + 2 * p
            sl = slice((h0 // 2) * LANES, (h0 // 2 + 1) * LANES)
            rhs = jnp.concatenate([xs_b[:, sl], st_s[:, sl].astype(BF16)], axis=0)
            ys = []
            for h in (h0, h0 + 1):
                m_h = _ssd_diag_pair(cb, cs, cs_t, dt_t, mask, h)
                c_h = cg * jnp.broadcast_to(ecs[:, h:h + 1], (rows, SSM_STATE))
                lhs = jnp.concatenate([m_h.astype(BF16), c_h.astype(BF16)], axis=1)
                ys.append(_dot(lhs, rhs))
            y_s[:, sl] = jnp.where(lane < SSM_HEAD_DIM, ys[0], ys[1])
    wc = (xs * coef_x).astype(BF16)
    for g in range(SSM_GROUPS):
        sl = slice(g * GROUP_WIDTH, (g + 1) * GROUP_WIDTH)
        bg = bm[:, g * SSM_STATE:(g + 1) * SSM_STATE].astype(BF16)
        upd = lax.dot_general(bg, wc[:, sl], (((0,), (0,)), ((), ())), preferred_element_type=F32)
        st_s[:, sl] = st_s[:, sl] * dlast_x[:, sl] + upd

    @pl.when(last)
    def _():
        st_ref[0] = st_s[...].T

    _ssd_finish(y_s[...], xs, z_ref, dexp_ref, nwm_ref, mix_ref)


def _mixer_prompt(proj, xbc, dt, prm, *, nb, nc):
    rows = CHUNK
    t = nb * nc * rows
    row = lambda b, c: b * nc + c
    full = lambda shape: pl.BlockSpec(shape, lambda b, c: (0,) * len(shape))
    return pl.pallas_call(
        _mixer_prompt_body,
        grid=(nb, nc),
        in_specs=[
            pl.BlockSpec((rows, GATE_WIDTH), lambda b, c: (row(b, c), 0)),
            pl.BlockSpec((rows, GATE_WIDTH), lambda b, c: (row(b, c), 1)),
            pl.BlockSpec((rows, SSM_WIDTH), lambda b, c: (row(b, c), 2)),
            pl.BlockSpec((rows, SSM_CONV_DIM), lambda b, c: (row(b, c), 0)),
            pl.BlockSpec((rows, LANES), lambda b, c: (row(b, c), 0)),
            full((1, GATE_WIDTH)), full((1, GATE_WIDTH)),
            full((GATE_HEADS, rows, rows)), full((rows, GATE_HEADS)),
            full((1, LANES)), full((1, LANES)), full((1, SSM_WIDTH)), full((1, SSM_WIDTH)),
            full((LANES, SSM_WIDTH)),
        ],
        out_specs=[
            pl.BlockSpec((rows, 2 * GATE_WIDTH), lambda b, c: (row(b, c), 0)),
            pl.BlockSpec((1, rows, GATE_WIDTH), lambda b, c: (b, 0, 0)),
            pl.BlockSpec((1, SSM_WIDTH, SSM_STATE), lambda b, c: (b, 0, 0)),
        ],
        out_shape=[
            jax.ShapeDtypeStruct((t, 2 * GATE_WIDTH), BF16),
            jax.ShapeDtypeStruct((nb, rows, GATE_WIDTH), F32),
            jax.ShapeDtypeStruct((nb, SSM_WIDTH, SSM_STATE), F32),
        ],
        scratch_shapes=[
            pltpu.VMEM((GATE_HEADS, rows, rows), BF16),
            pltpu.VMEM((SSM_STATE, SSM_WIDTH), F32),
            pltpu.VMEM((rows, SSM_WIDTH), F32),
        ],
        compiler_params=_cparams(("arbitrary", "arbitrary")),
        name="mixer_prompt",
    )(proj, proj, proj, xbc, dt, prm["ln_g"], prm["ln_b"], prm["wt_p"], prm["btT_p"],
      prm["dtb"], prm["alog"], prm["dexp"], prm["nwm"], prm["e"])


SEQ_TILE = 8


def _state_decay_body(dt_ref, dtb_ref, alog_ref, o_ref, *, seg):
    nseq = o_ref.shape[0]
    a = -jnp.exp(alog_ref[...])
    tot = jnp.zeros(o_ref.shape, F32)
    for t in range(seg):
        d = jax.nn.softplus(dt_ref[pl.ds(t, nseq, stride=seg), :] + dtb_ref[...])
        tot = tot + d * a
    o_ref[...] = jnp.exp(tot)


def _state_decay(dt, prm, *, seg):
    nseq = dt.shape[0] // seg
    return pl.pallas_call(
        functools.partial(_state_decay_body, seg=seg),
        out_shape=jax.ShapeDtypeStruct((nseq, LANES), F32),
        name="state_decay",
    )(dt, prm["dtb"], prm["alog"])


def _mixer_sample_body(dec_ref, gu_ref, gv_ref, z_ref, xbc_ref, dt_ref, sin_ref,
                       lng_ref, lnb_ref, wt_ref, btT_ref,
                       dtb_ref, alog_ref, dexp_ref, nwm_ref, e_ref,
                       mix_ref, cv_ref, sout_ref,
                       wm_s, y_s, cm_s, bm_s, wct_s, ex_s, *, seg):
    i = pl.program_id(0)
    rows = SEQ_TILE * seg
    mask, same = _seg_masks(rows, seg)

    @pl.when(i == 0)
    def _():
        for h in range(GATE_HEADS):
            wm_s[h] = jnp.where(mask, wt_ref[h], 0.0).astype(BF16)

    cv_ref[...] = _gate_tile(gu_ref, gv_ref, lng_ref, lnb_ref, wm_s, btT_ref, mix_ref)

    act = xbc_ref[...].astype(F32)
    maskf = mask.astype(BF16)
    segf = same.astype(BF16)
    xs, bm, cm, dt, cs, cl = _ssd_token_level(act, dt_ref[...], dtb_ref, alog_ref, maskf, segf)
    cs_t = cs.T
    dt_t = dt.T
    e_bf = e_ref[...]
    coef_x = _expand_heads(dt * jnp.exp(cl - cs), e_bf)
    ecs_x = _expand_heads(jnp.exp(cs), e_bf)
    lane = lax.broadcasted_iota(jnp.int32, (rows, LANES), 1)
    xs_b = xs.astype(BF16)
    for g in range(SSM_GROUPS):
        cg = cm[:, g * SSM_STATE:(g + 1) * SSM_STATE]
        bg = bm[:, g * SSM_STATE:(g + 1) * SSM_STATE]
        cb = lax.dot_general(cg.astype(BF16), bg.astype(BF16), (((1,), (1,)), ((), ())), preferred_element_type=F32)
        for p in range(HEADS_PER_GROUP // 2):
            h0 = g * HEADS_PER_GROUP + 2 * p
            sl = slice((h0 // 2) * LANES, (h0 // 2 + 1) * LANES)
            ys = [_dot(_ssd_diag_pair(cb, cs, cs_t, dt_t, mask, h).astype(BF16), xs_b[:, sl]) for h in (h0, h0 + 1)]
            y_s[:, sl] = jnp.where(lane < SSM_HEAD_DIM, ys[0], ys[1])
    cm_s[...] = cm
    bm_s[...] = bm
    wct_s[...] = (xs * coef_x).T
    ex_s[...] = ecs_x

    rowid = lax.broadcasted_iota(jnp.int32, (rows, SSM_STATE), 0)

    for s in range(SEQ_TILE):
        r8 = slice(s * seg, (s + 1) * seg)
        for g in range(SSM_GROUPS):
            gsl = slice(g * GROUP_WIDTH, (g + 1) * GROUP_WIDTH)
            nsl = slice(g * SSM_STATE, (g + 1) * SSM_STATE)
            st = sin_ref[s, gsl, :]
            c8 = cm_s[r8, nsl].astype(BF16)
            yo = lax.dot_general(c8, st.astype(BF16), (((1,), (1,)), ((), ())), preferred_element_type=F32)
            y_s[r8, gsl] = y_s[r8, gsl] + yo * ex_s[r8, gsl]
            bmask = jnp.where(rowid // seg == s, bm_s[:, nsl], 0.0).astype(BF16)
            upd = _dot(wct_s[gsl, :].astype(BF16), bmask)
            for r in range(HEADS_PER_GROUP):
                d = dec_ref[(i * SEQ_TILE + s) * SSM_HEADS + g * HEADS_PER_GROUP + r]
                hsl = slice(r * SSM_HEAD_DIM, (r + 1) * SSM_HEAD_DIM)
                osl = slice(g * GROUP_WIDTH + r * SSM_HEAD_DIM, g * GROUP_WIDTH + (r + 1) * SSM_HEAD_DIM)
                sout_ref[s, osl, :] = st[hsl, :] * d + upd[hsl, :]

    _ssd_finish(y_s[...], xs, z_ref, dexp_ref, nwm_ref, mix_ref)


def _mixer_sample(dec, proj, xbc, dt, state, prm, *, seg):
    rows = SEQ_TILE * seg
    t = proj.shape[0]
    nseq = t // seg
    assert t % rows == 0
    full = lambda shape: pl.BlockSpec(shape, lambda i: (0,) * len(shape))
    return pl.pallas_call(
        functools.partial(_mixer_sample_body, seg=seg),
        grid=(t // rows,),
        in_specs=[
            pl.BlockSpec(memory_space=pltpu.SMEM),
            pl.BlockSpec((rows, GATE_WIDTH), lambda i: (i, 0)),
            pl.BlockSpec((rows, GATE_WIDTH), lambda i: (i, 1)),
            pl.BlockSpec((rows, SSM_WIDTH), lambda i: (i, 2)),
            pl.BlockSpec((rows, SSM_CONV_DIM), lambda i: (i, 0)),
            pl.BlockSpec((rows, LANES), lambda i: (i, 0)),
            pl.BlockSpec((SEQ_TILE, SSM_WIDTH, SSM_STATE), lambda i: (i, 0, 0)),
            full((1, GATE_WIDTH)), full((1, GATE_WIDTH)),
            full((GATE_HEADS, rows, rows)), full((rows, GATE_HEADS)),
            full((1, LANES)), full((1, LANES)), full((1, SSM_WIDTH)), full((1, SSM_WIDTH)),
            full((LANES, SSM_WIDTH)),
        ],
        out_specs=[
            pl.BlockSpec((rows, 2 * GATE_WIDTH), lambda i: (i, 0)),
            pl.BlockSpec((rows, GATE_WIDTH), lambda i: (i, 0)),
            pl.BlockSpec((SEQ_TILE, SSM_WIDTH, SSM_STATE), lambda i: (i, 0, 0)),
        ],
        out_shape=[
            jax.ShapeDtypeStruct((t, 2 * GATE_WIDTH), BF16),
            jax.ShapeDtypeStruct((t, GATE_WIDTH), F32),
            jax.ShapeDtypeStruct((nseq, SSM_WIDTH, SSM_STATE), F32),
        ],
        scratch_shapes=[
            pltpu.VMEM((GATE_HEADS, rows, rows), BF16),
            pltpu.VMEM((rows, SSM_WIDTH), F32),
            pltpu.VMEM((rows, GROUP_WIDTH), F32),
            pltpu.VMEM((rows, GROUP_WIDTH), F32),
            pltpu.VMEM((SSM_WIDTH, rows), F32),
            pltpu.VMEM((rows, SSM_WIDTH), F32),
        ],
        compiler_params=_cparams(("arbitrary",)),
        name="mixer_sample",
    )(dec, proj, proj, proj, xbc, dt, state, prm["ln_g"], prm["ln_b"], prm["wt_s"], prm["btT_s"],
      prm["dtb"], prm["alog"], prm["dexp"], prm["nwm"], prm["e"])


ROW_SUB = 128


def _row_subs(tm, sub=ROW_SUB):
    sub = min(sub, tm)
    assert tm % sub == 0
    return [slice(r * sub, (r + 1) * sub) for r in range(tm // sub)]


def _out_proj_body(m_ref, x_ref, w_ref, npost_ref, npre_ref, x1_ref, h2_ref):
    for rs in _row_subs(m_ref.shape[0]):
        mix = _dot(m_ref[rs, :], w_ref[...])
        x1 = x_ref[rs, :] + _rms(mix, npost_ref[...])
        x1_ref[rs, :] = x1
        h2_ref[rs, :] = _rms(x1, npre_ref[...]).astype(h2_ref.dtype)


def _out_proj(mixin, x2d, w_out, npost, npre, *, tm):
    t, k = mixin.shape
    assert t % tm == 0
    return pl.pallas_call(
        _out_proj_body,
        grid=(t // tm,),
        in_specs=[
            pl.BlockSpec((tm, k), lambda i: (i, 0)),
            pl.BlockSpec((tm, D_MODEL), lambda i: (i, 0)),
            pl.BlockSpec((k, D_MODEL), lambda i: (0, 0), pipeline_mode=pl.Buffered(1)),
            pl.BlockSpec((1, D_MODEL), lambda i: (0, 0)),
            pl.BlockSpec((1, D_MODEL), lambda i: (0, 0)),
        ],
        out_specs=[
            pl.BlockSpec((tm, D_MODEL), lambda i: (i, 0)),
            pl.BlockSpec((tm, D_MODEL), lambda i: (i, 0)),
        ],
        out_shape=[
            jax.ShapeDtypeStruct((t, D_MODEL), F32),
            jax.ShapeDtypeStruct((t, D_MODEL), BF16),
        ],
        compiler_params=_cparams(("arbitrary",)),
        name="out_proj",
    )(mixin, x2d, w_out, npost, npre)


def _ffn_conv_taps(cur, shifted, cw_ref, cb_ref):
    out = cb_ref[...] + cw_ref[FFN_CONV - 1:FFN_CONV, :] * cur
    for d in range(1, FFN_CONV):
        out = out + cw_ref[FFN_CONV - 1 - d:FFN_CONV - d, :] * shifted[d - 1]
    return out


def _ffn_up_prompt_body(h_ref, wg_ref, wu_ref, cwg_ref, cwu_ref, cbg_ref, cbu_ref, *rest, tiles_per_seq, n_cast):
    cast_in, (a_ref, stg_ref, stu_ref) = rest[:n_cast], rest[n_cast:n_cast + 3]
    cast_out, (tg_s, tu_s) = rest[n_cast + 3:2 * n_cast + 3], rest[2 * n_cast + 3:]
    i = pl.program_id(1)
    tm = h_ref.shape[0]
    _side_cast(cast_in, cast_out)

    @pl.when((i % tiles_per_seq) == 0)
    def _():
        for t_s in (tg_s, tu_s):
            t_s[...] = jnp.zeros(t_s.shape, F32)

    tails = [tg_s[...], tu_s[...]]
    row8 = lax.broadcasted_iota(jnp.int32, tg_s.shape, 0)
    for rs in _row_subs(tm, 128):
        h = h_ref[rs, :]
        convs = []
        for k, (w_ref, cw_ref, cb_ref) in enumerate(((wg_ref, cwg_ref, cbg_ref), (wu_ref, cwu_ref, cbu_ref))):
            x = _dot(h, w_ref[...])
            shifted = []
            for d in range(1, FFN_CONV):
                r = pltpu.roll(x, d, axis=0)
                head = jnp.where(row8 >= d, r[0:SUBLANES], pltpu.roll(tails[k], d, axis=0))
                shifted.append(jnp.concatenate([head, r[SUBLANES:]], axis=0))
            convs.append(_ffn_conv_taps(x, shifted, cw_ref, cb_ref))
            tails[k] = x[x.shape[0] - SUBLANES:]
        a_ref[rs, :] = (jax.nn.gelu(convs[0], approximate=True) * convs[1]).astype(a_ref.dtype)
    for tail, t_s, st_ref in zip(tails, (tg_s, tu_s), (stg_ref, stu_ref)):
        t_s[...] = tail
        st_ref[0] = tail[SUBLANES - (FFN_CONV - 1):]


def _ffn_up_sample_body(h_ref, wg_ref, wu_ref, cwg_ref, cwu_ref, cbg_ref, cbu_ref, pg_ref, pu_ref,
                        a_ref, stg_ref, stu_ref, *, seg):
    tm = h_ref.shape[0]
    tn = wg_ref.shape[1]
    for rs in _row_subs(tm):
        n = rs.stop - rs.start
        nseq = n // seg
        sq = slice(rs.start // seg, rs.stop // seg)
        h = h_ref[rs, :]
        tpos = lax.broadcasted_iota(jnp.int32, (nseq, seg, tn), 1)
        convs = []
        for w_ref, p_ref, cw_ref, cb_ref, st_ref in ((wg_ref, pg_ref, cwg_ref, cbg_ref, stg_ref),
                                                     (wu_ref, pu_ref, cwu_ref, cbu_ref, stu_ref)):
            x3 = _dot(h, w_ref[...]).reshape(nseq, seg, tn)
            shifted = [_seq_shift(x3, p_ref, sq, d, tpos) for d in range(1, FFN_CONV)]
            convs.append(_ffn_conv_taps(x3, shifted, cw_ref, cb_ref))
            st_ref[sq] = x3[:, seg - (FFN_CONV - 1):, :]
        act = jax.nn.gelu(convs[0], approximate=True) * convs[1]
        a_ref[rs, :] = act.reshape(n, tn).astype(a_ref.dtype)


def _ffn_up(h2, w_up, cw, cb, prev, *, nseq, seg, tm, tn, cast=()):
    t = h2.shape[0]
    nj, ni = D_FF // tn, t // tm
    assert t % tm == 0 and D_FF % tn == 0 and t == nseq * seg
    common_in = [
        pl.BlockSpec((tm, D_MODEL), lambda j, i: (i, 0)),
        pl.BlockSpec((D_MODEL, tn), lambda j, i: (0, j)),
        pl.BlockSpec((D_MODEL, tn), lambda j, i: (0, j + nj)),
        pl.BlockSpec((FFN_CONV, tn), lambda j, i: (0, j)),
        pl.BlockSpec((FFN_CONV, tn), lambda j, i: (0, j + nj)),
        pl.BlockSpec((1, tn), lambda j, i: (0, j)),
        pl.BlockSpec((1, tn), lambda j, i: (0, j + nj)),
    ]
    a_spec = pl.BlockSpec((tm, tn), lambda j, i: (i, j))
    st_shape = jax.ShapeDtypeStruct((nseq, FFN_CONV - 1, D_FF), F32)
    out_shape = [jax.ShapeDtypeStruct((t, D_FF), BF16), st_shape, st_shape]
    if prev is None:
        assert seg % tm == 0
        tps = seg // tm
        c_in, c_out, c_shapes = _side_cast_specs(cast, nj * ni, lambda j, i: j * ni + i)
        return pl.pallas_call(
            functools.partial(_ffn_up_prompt_body, tiles_per_seq=tps, n_cast=len(cast)),
            grid=(nj, ni),
            in_specs=common_in + c_in,
            out_specs=[a_spec] + [pl.BlockSpec((1, FFN_CONV - 1, tn), lambda j, i: (i // tps, 0, j))] * 2 + c_out,
            out_shape=out_shape + c_shapes,
            scratch_shapes=[pltpu.VMEM((SUBLANES, tn), F32), pltpu.VMEM((SUBLANES, tn), F32)],
            compiler_params=_cparams(("arbitrary", "arbitrary")),
            name="ffn_up_prompt",
        )(h2, w_up, w_up, cw, cw, cb, cb, *cast)
    assert tm % seg == 0 and seg == SUBLANES and not cast
    return pl.pallas_call(
        functools.partial(_ffn_up_sample_body, seg=seg),
        grid=(nj, ni),
        in_specs=common_in + [
            pl.BlockSpec((tm // seg, FFN_CONV - 1, tn), lambda j, i: (i, 0, j)),
            pl.BlockSpec((tm // seg, FFN_CONV - 1, tn), lambda j, i: (i, 0, j + nj)),
        ],
        out_specs=[a_spec] + [pl.BlockSpec((tm // seg, FFN_CONV - 1, tn), lambda j, i: (i, 0, j))] * 2,
        out_shape=out_shape,
        compiler_params=_cparams(("arbitrary", "arbitrary")),
        name="ffn_up_sample",
    )(h2, w_up, w_up, cw, cw, cb, cb, prev, prev)


def _ffn_down_body(a_ref, x_ref, w_ref, nw_ref, y_ref):
    for rs in _row_subs(a_ref.shape[0], 128):
        f = _dot(a_ref[rs, :], w_ref[...])
        y_ref[rs, :] = x_ref[rs, :] + _rms(f, nw_ref[...])


def _ffn_down(act, x1, w_down, nw, *, tm):
    t, k = act.shape
    assert t % tm == 0
    return pl.pallas_call(
        _ffn_down_body,
        grid=(t // tm,),
        in_specs=[
            pl.BlockSpec((tm, k), lambda i: (i, 0)),
            pl.BlockSpec((tm, D_MODEL), lambda i: (i, 0)),
            pl.BlockSpec((k, D_MODEL), lambda i: (0, 0), pipeline_mode=pl.Buffered(1)),
            pl.BlockSpec((1, D_MODEL), lambda i: (0, 0)),
        ],
        out_specs=pl.BlockSpec((tm, D_MODEL), lambda i: (i, 0)),
        out_shape=jax.ShapeDtypeStruct((t, D_MODEL), F32),
        compiler_params=_cparams(("arbitrary",)),
        name="ffn_down",
    )(act, x1, w_down, nw)


def _head_expander():
    e = np.zeros((LANES, SSM_WIDTH), np.float32)
    for h in range(SSM_HEADS):
        e[h, h * SSM_HEAD_DIM:(h + 1) * SSM_HEAD_DIM] = 1.0
    return jnp.asarray(e, BF16)


def _pad_lanes(v):
    return jnp.pad(v, (0, LANES - v.shape[0]))[None, :]


def _prep_params(norm_mix_pre, w_in, gate_ln_g, gate_ln_b, gate_w_s, gate_b_s, ssm_conv_w, ssm_conv_b, ssm_dt_bias,
                 ssm_a_log, ssm_d, ssm_norm_w, w_out, norm_mix_post, norm_ffn_pre, ffn_w_up, ffn_conv_w, ffn_conv_b,
                 ffn_w_down, norm_ffn_post, seg_sample):
    w_in_t = jnp.swapaxes(w_in, 0, 1)
    ws_small = gate_w_s[:, :seg_sample, :seg_sample]
    return dict(
        nw_pre=norm_mix_pre[None, :],
        w_in_t=w_in_t,
        w_dt=jnp.pad(w_in_t[PROJ_MAIN:, :], ((0, LANES - SSM_HEADS), (0, 0))).astype(BF16),
        ln_g=gate_ln_g[None, :], ln_b=gate_ln_b[None, :],
        wt_p=gate_w_s, btT_p=gate_b_s.T,
        wt_s=jnp.tile(ws_small, (1, SEQ_TILE, SEQ_TILE)), btT_s=jnp.tile(gate_b_s[:, :seg_sample], (1, SEQ_TILE)).T,
        conv_w=ssm_conv_w, conv_b=ssm_conv_b[None, :],
        dtb=_pad_lanes(ssm_dt_bias), alog=_pad_lanes(ssm_a_log),
        dexp=jnp.repeat(ssm_d, SSM_HEAD_DIM)[None, :], nwm=ssm_norm_w[None, :],
        e=_head_expander(),
        w_out=w_out, n_post=norm_mix_post[None, :], n_pre2=norm_ffn_pre[None, :],
        w_up=ffn_w_up, fcw=ffn_conv_w, fcb=ffn_conv_b[None, :],
        w_down=ffn_w_down, n_post2=norm_ffn_post[None, :],
    )


def _row_tile(t, cap):
    tm = cap
    while t % tm:
        tm //= 2
    assert tm >= 64
    return tm


TM_STREAM = 1024
TM_OUT_PROJ = 512
TM_FFN_DOWN = 512
TM_PRE_NORM = 1024


def _layer_prompt(x, prm):
    nb, seq, _ = x.shape
    assert seq % CHUNK == 0
    nc = seq // CHUNK
    x2d = x.reshape(nb * seq, D_MODEL)
    tm = _row_tile(seq, TM_STREAM)
    h, dt = _pre_norm(x2d, prm["nw_pre"], prm["w_dt"], tm=_row_tile(seq, TM_PRE_NORM))
    proj, w_out, w_up, w_in_b = _in_proj(h, prm["w_in_t"], tm=tm, tn=1024, sub=min(tm, 256),
                                         cast=(prm["w_out"], prm["w_up"]))
    xbc, sc, w_xbc_b = _xbc_proj(h, prm["w_in_t"], PROJ_GATE, prm["conv_w"], prm["conv_b"], None,
                                 nseq=nb, seg=seq, tm=tm, tn=1024)
    mixin, cv, st = _mixer_prompt(proj, xbc, dt, prm, nb=nb, nc=nc)
    x1, h2 = _out_proj(mixin, x2d, w_out, prm["n_post"], prm["n_pre2"], tm=_row_tile(seq, TM_OUT_PROJ))
    act, fst_g, fst_u, w_down = _ffn_up(h2, w_up, prm["fcw"], prm["fcb"], None, nseq=nb, seg=seq, tm=tm, tn=512,
                                        cast=(prm["w_down"],))
    y = _ffn_down(act, x1, w_down, prm["n_post2"], tm=_row_tile(seq, TM_FFN_DOWN))
    return dict(w_in_t=w_in_b, w_xbc_t=w_xbc_b, w_out=w_out, w_up=w_up, w_down=w_down), (
            y.reshape(nb, seq, D_MODEL),
            st.reshape(nb, SSM_HEADS, SSM_HEAD_DIM, SSM_STATE),
            sc,
            jnp.concatenate([fst_g, fst_u], axis=-1),
            cv.reshape(nb, CHUNK, GATE_HEADS, GATE_HEAD_DIM))


def _layer_sample(x, state_ssm, state_sconv, state_fconv, prm, wb):
    nb, seg, _ = x.shape
    assert seg == SUBLANES and nb % SEQ_TILE == 0
    t = nb * seg
    x2d = x.reshape(t, D_MODEL)
    tm = _row_tile(t, TM_STREAM)
    h, dt = _pre_norm(x2d, prm["nw_pre"], prm["w_dt"], tm=_row_tile(t, TM_PRE_NORM))
    proj, = _in_proj(h, wb["w_in_t"], tm=tm, tn=1024, sub=min(tm, 256))
    xbc, sc = _xbc_proj(h, wb["w_xbc_t"], 0, prm["conv_w"], prm["conv_b"], state_sconv, nseq=nb, seg=seg, tm=tm, tn=1024)
    dec = _state_decay(dt, prm, seg=seg)[:, :SSM_HEADS].reshape(nb * SSM_HEADS)
    mixin, cv, st = _mixer_sample(dec, proj, xbc, dt, state_ssm.reshape(nb, SSM_WIDTH, SSM_STATE), prm, seg=seg)
    x1, h2 = _out_proj(mixin, x2d, wb["w_out"], prm["n_post"], prm["n_pre2"], tm=_row_tile(t, TM_OUT_PROJ))
    act, fst_g, fst_u = _ffn_up(h2, wb["w_up"], prm["fcw"], prm["fcb"], state_fconv, nseq=nb, seg=seg, tm=tm, tn=512)
    y = _ffn_down(act, x1, wb["w_down"], prm["n_post2"], tm=_row_tile(t, TM_FFN_DOWN))
    return (y.reshape(nb, seg, D_MODEL),
            st.reshape(nb, SSM_HEADS, SSM_HEAD_DIM, SSM_STATE),
            sc,
            jnp.concatenate([fst_g, fst_u], axis=-1),
            cv.reshape(nb, seg, GATE_HEADS, GATE_HEAD_DIM))


def kernel(x_prompt, x_sample, state_ssm, state_ssm_conv, state_ffn_conv, norm_mix_pre, w_in, gate_ln_g, gate_ln_b,
           gate_w_s, gate_b_s, ssm_conv_w, ssm_conv_b, ssm_dt_bias, ssm_a_log, ssm_d, ssm_norm_w, w_out,
           norm_mix_post, norm_ffn_pre, ffn_w_up, ffn_conv_w, ffn_conv_b, ffn_w_down, norm_ffn_post):
    depth = w_in.shape[0]
    yp, ys = x_prompt, x_sample
    outs_p, outs_s = [], []
    for l in range(depth):
        prm = _prep_params(norm_mix_pre[l], w_in[l], gate_ln_g[l], gate_ln_b[l], gate_w_s[l], gate_b_s[l],
                           ssm_conv_w[l], ssm_conv_b[l], ssm_dt_bias[l], ssm_a_log[l], ssm_d[l], ssm_norm_w[l],
                           w_out[l], norm_mix_post[l], norm_ffn_pre[l], ffn_w_up[l], ffn_conv_w[l], ffn_conv_b[l],
                           ffn_w_down[l], norm_ffn_post[l], x_sample.shape[1])
        wb, (yp, *rest_p) = _layer_prompt(yp, prm)
        ys, *rest_s = _layer_sample(ys, state_ssm[l], state_ssm_conv[l], state_ffn_conv[l], prm, wb)
        outs_p.append(rest_p)
        outs_s.append(rest_s)
    stack = lambda outs, k: jnp.stack([o[k] for o in outs])
    return (yp, ys,
            stack(outs_p, 0), stack(outs_p, 1), stack(outs_p, 2), stack(outs_p, 3),
            stack(outs_s, 0), stack(outs_s, 1), stack(outs_s, 2), stack(outs_s, 3))
```

```python
import functools

import jax
import jax.numpy as jnp
import numpy as np
from jax import lax
from jax.experimental import pallas as pl
from jax.experimental.pallas import tpu as pltpu

F32 = jnp.float32
BF16 = jnp.bfloat16

D_MODEL = 2048
GATE_WIDTH = 2048
GATE_HEADS = 16
GATE_HEAD_DIM = 128
CHUNK = 128
SSM_WIDTH = 2048
SSM_HEAD_DIM = 64
SSM_HEADS = 32
SSM_GROUPS = 4
SSM_STATE = 128
SSM_CONV = 4
SSM_CONV_DIM = SSM_WIDTH + 2 * SSM_GROUPS * SSM_STATE
PROJ_MAIN = 2 * GATE_WIDTH + SSM_WIDTH + SSM_CONV_DIM
D_FF = 5632
FFN_CONV = 3
EPS = 1e-6
HEADS_PER_GROUP = SSM_HEADS // SSM_GROUPS
GROUP_WIDTH = SSM_WIDTH // SSM_GROUPS

LANES = 128
SUBLANES = 8
VMEM_LIMIT = 56 * 1024 * 1024


def _cparams(sem):
    return pltpu.CompilerParams(dimension_semantics=sem, vmem_limit_bytes=VMEM_LIMIT)


def _rms(x, w):
    return x * lax.rsqrt(jnp.mean(x * x, axis=-1, keepdims=True) + EPS) * w


def _gelu_erf(x):
    return 0.5 * x * (1.0 + lax.erf(x * np.float32(0.7071067811865476)))


def _split_bf16(x, n):
    parts = []
    r = x
    for k in range(n):
        p = r.astype(BF16)
        parts.append(p)
        if k + 1 < n:
            r = r - p.astype(F32)
    return parts


def _dot(a, b):
    return jnp.dot(a, b, preferred_element_type=F32)


def _dot_nt(a, b):
    return lax.dot_general(a, b, (((1,), (1,)), ((), ())), preferred_element_type=F32)


def _mm_split(m_bf, x, n):
    acc = None
    for p in _split_bf16(x, n):
        t = _dot(m_bf, p)
        acc = t if acc is None else acc + t
    return acc


def _expand_heads(x, e_bf):
    acc = None
    for p in _split_bf16(x, 2):
        t = _dot(p, e_bf)
        acc = t if acc is None else acc + t
    return acc


def _seq_shift(x3, p_ref, sq, d, tpos):
    k1 = p_ref.shape[1]
    out = pltpu.roll(x3, d, axis=1)
    for t in range(d):
        out = jnp.where(tpos == t, jnp.broadcast_to(p_ref[sq, k1 + t - d:k1 + t - d + 1, :], x3.shape), out)
    return out


def _seg_masks(rows, seg):
    r = lax.broadcasted_iota(jnp.int32, (rows, rows), 0)
    c = lax.broadcasted_iota(jnp.int32, (rows, rows), 1)
    same = (r // seg) == (c // seg)
    return same & (c <= r), same


BF16_ROWS = 2 * SUBLANES


def _cast_rows(rows, nsteps):
    per = BF16_ROWS
    while rows % per or rows // per > nsteps:
        per += BF16_ROWS
        assert per <= rows
    return per


def _side_cast_specs(weights, nsteps, step_of):
    ins, outs, shapes = [], [], []
    for w in weights:
        rows, cols = w.shape
        per = _cast_rows(rows, nsteps)
        spec = pl.BlockSpec((per, cols), lambda *g, nb=rows // per: (jnp.minimum(step_of(*g), nb - 1), 0))
        ins.append(spec)
        outs.append(spec)
        shapes.append(jax.ShapeDtypeStruct((rows, cols), BF16))
    return ins, outs, shapes


def _side_cast(cast_in, cast_out):
    for w_ref, o_ref in zip(cast_in, cast_out):
        o_ref[...] = w_ref[...].astype(BF16)


def _pre_norm_body(x_ref, nw_ref, wdt_ref, h_ref, dt_ref):
    h = _rms(x_ref[...], nw_ref[...]).astype(BF16)
    h_ref[...] = h
    dt_ref[...] = _dot_nt(h, wdt_ref[...])


def _pre_norm(x2d, nw, w_dt, *, tm):
    t = x2d.shape[0]
    assert t % tm == 0
    return pl.pallas_call(
        _pre_norm_body,
        grid=(t // tm,),
        in_specs=[
            pl.BlockSpec((tm, D_MODEL), lambda i: (i, 0)),
            pl.BlockSpec((1, D_MODEL), lambda i: (0, 0)),
            pl.BlockSpec((LANES, D_MODEL), lambda i: (0, 0)),
        ],
        out_specs=[
            pl.BlockSpec((tm, D_MODEL), lambda i: (i, 0)),
            pl.BlockSpec((tm, LANES), lambda i: (i, 0)),
        ],
        out_shape=[
            jax.ShapeDtypeStruct((t, D_MODEL), BF16),
            jax.ShapeDtypeStruct((t, LANES), F32),
        ],
        compiler_params=_cparams(("arbitrary",)),
        name="pre_norm",
    )(x2d, nw, w_dt)


def _in_proj_body(h_ref, w_ref, *rest, n_gelu, n_silu, sub, n_cast, cast_w):
    cast_in, o_ref, cast_out, wb_s = rest[:n_cast], rest[n_cast], rest[n_cast + 1:2 * n_cast + 1], rest[-1]
    j = pl.program_id(0)
    i = pl.program_id(1)
    tm = h_ref.shape[0]
    _side_cast(cast_in, cast_out)

    @pl.when(i == 0)
    def _():
        wb_s[...] = w_ref[...].astype(BF16)
        if cast_w:
            rest[-2][...] = wb_s[...]

    def run(epilogue):
        for rs in _row_subs(tm, sub):
            o_ref[rs, :] = epilogue(_dot_nt(h_ref[rs, :], wb_s[...])).astype(o_ref.dtype)

    @pl.when(j < n_gelu)
    def _():
        run(_gelu_erf)

    @pl.when((j >= n_gelu) & (j < n_gelu + n_silu))
    def _():
        run(jax.nn.silu)

    @pl.when(j >= n_gelu + n_silu)
    def _():
        run(lambda a: a)


def _in_proj(h, w_in_t, *, tm, tn, sub, cast=()):
    t = h.shape[0]
    assert t % tm == 0 and PROJ_MAIN % tn == 0 and GATE_WIDTH % tn == 0 and SSM_WIDTH % tn == 0 and tm % sub == 0
    nj, ni = PROJ_MAIN // tn, t // tm
    c_in, c_out, c_shapes = _side_cast_specs(cast, nj * ni, lambda j, i: j * ni + i)
    w_spec = pl.BlockSpec((tn, D_MODEL), lambda j, i: (j, 0))
    cast_w = w_in_t.dtype != BF16
    return pl.pallas_call(
        functools.partial(_in_proj_body, n_gelu=2 * GATE_WIDTH // tn, n_silu=SSM_WIDTH // tn, sub=sub,
                          n_cast=len(cast), cast_w=cast_w),
        grid=(nj, ni),
        in_specs=[pl.BlockSpec((tm, D_MODEL), lambda j, i: (i, 0)), w_spec] + c_in,
        out_specs=[pl.BlockSpec((tm, tn), lambda j, i: (i, j))] + c_out + [w_spec] * cast_w,
        out_shape=([jax.ShapeDtypeStruct((t, PROJ_MAIN), BF16)] + c_shapes
                   + [jax.ShapeDtypeStruct((PROJ_MAIN, D_MODEL), BF16)] * cast_w),
        scratch_shapes=[pltpu.VMEM((tn, D_MODEL), BF16)],
        compiler_params=_cparams(("arbitrary", "arbitrary")),
        name="in_proj",
    )(h, w_in_t, *cast)


def _gate_tile(gu_ref, gv_ref, lng_ref, lnb_ref, wm_s, btT_ref, mix_ref):
    g = gv_ref[...].astype(F32)
    mu = jnp.mean(g, axis=-1, keepdims=True)
    xc = g - mu
    v = xc * lax.rsqrt(jnp.mean(xc * xc, axis=-1, keepdims=True) + EPS) * lng_ref[...] + lnb_ref[...]
    vb = v.astype(BF16)
    rows = v.shape[0]
    for h in range(GATE_HEADS):
        sl = slice(h * GATE_HEAD_DIM, (h + 1) * GATE_HEAD_DIM)
        s = _dot(wm_s[h], vb[:, sl]) + jnp.broadcast_to(btT_ref[:, h:h + 1], (rows, GATE_HEAD_DIM))
        mix_ref[:, sl] = gu_ref[:, sl] * s.astype(mix_ref.dtype)
    return v


def _ssd_token_level(act, dt_raw, dtb_ref, alog_ref, maskf, segf):
    xs = act[:, :SSM_WIDTH]
    bm = act[:, SSM_WIDTH:SSM_WIDTH + GROUP_WIDTH]
    cm = act[:, SSM_WIDTH + GROUP_WIDTH:]
    dt = jax.nn.softplus(dt_raw + dtb_ref[...])
    a = -jnp.exp(alog_ref[...])
    da = dt * a
    cs = _mm_split(maskf, da, 3)
    cl = _mm_split(segf, da, 3)
    return xs, bm, cm, dt, cs, cl


def _ssd_diag_pair(cb, cs, cs_t, dt_t, mask, h):
    seg = cs[:, h:h + 1] - cs_t[h:h + 1, :]
    decay = jnp.exp(jnp.where(mask, seg, -jnp.inf))
    return cb * decay * dt_t[h:h + 1, :]


def _ssd_finish(y, xs, z_ref, dexp_ref, nwm_ref, mix_ref):
    y = y + dexp_ref[...] * xs
    y = y * z_ref[...].astype(F32)
    for g in range(SSM_GROUPS):
        sl = slice(g * GROUP_WIDTH, (g + 1) * GROUP_WIDTH)
        yg = y[:, sl]
        yg = yg * lax.rsqrt(jnp.mean(yg * yg, axis=-1, keepdims=True) + EPS) * nwm_ref[:, sl]
        mix_ref[:, GATE_WIDTH + g * GROUP_WIDTH:GATE_WIDTH + (g + 1) * GROUP_WIDTH] = yg.astype(mix_ref.dtype)


def _mixer_prompt_body(gu_ref, gv_ref, z_ref, xbc_ref, dt_ref, lng_ref, lnb_ref, wt_ref, btT_ref, cw_ref, cb_ref,
                       dtb_ref, alog_ref, dexp_ref, nwm_ref, e_ref,
                       mix_ref, cv_ref, st_ref, sc_ref,
                       wm_s, shift_s, xx_s, st_s, y_s):
    b = pl.program_id(0)
    c = pl.program_id(1)
    last = c == pl.num_programs(1) - 1
    rows = CHUNK
    keep = 2 * SUBLANES
    mask, same = _seg_masks(rows, rows)

    @pl.when((b == 0) & (c == 0))
    def _():
        for h in range(GATE_HEADS):
            wm_s[h] = jnp.where(mask, wt_ref[h], 0.0).astype(BF16)
        r = lax.broadcasted_iota(jnp.int32, shift_s.shape, 0)
        col = lax.broadcasted_iota(jnp.int32, shift_s.shape, 1)
        shift_s[...] = jnp.where(col == rows + r % rows - (r // rows + 1), 1.0, 0.0).astype(BF16)
        xx_s[0:rows, :] = jnp.zeros((rows, SSM_CONV_DIM), BF16)

    @pl.when(c == 0)
    def _():
        xx_s[rows - keep:rows, :] = jnp.zeros((keep, SSM_CONV_DIM), BF16)
        st_s[...] = jnp.zeros(st_s.shape, F32)

    v = _gate_tile(gu_ref, gv_ref, lng_ref, lnb_ref, wm_s, btT_ref, mix_ref)

    @pl.when(last)
    def _():
        cv_ref[0] = v

    xx_s[rows:2 * rows, :] = xbc_ref[...]
    sh = _dot(shift_s[...], xx_s[...])
    conv = cb_ref[...] + cw_ref[SSM_CONV - 1:SSM_CONV, :] * xbc_ref[...].astype(F32)
    for d in range(1, SSM_CONV):
        conv = conv + cw_ref[SSM_CONV - 1 - d:SSM_CONV - d, :] * sh[(d - 1) * rows:d * rows, :]
    act = jax.nn.silu(conv)

    @pl.when(last)
    def _():
        sc_ref[0] = xbc_ref[rows - (SSM_CONV - 1):rows, :].astype(F32)

    xx_s[rows - keep:rows, :] = xx_s[2 * rows - keep:2 * rows, :]

    maskf = mask.astype(BF16)
    segf = same.astype(BF16)
    xs, bm, cm, dt, cs, cl = _ssd_token_level(act, dt_ref[...], dtb_ref, alog_ref, maskf, segf)
    cs_t = cs.T
    dt_t = dt.T
    ecs = jnp.exp(cs)
    e_bf = e_ref[...]
    coef_x = _expand_heads(dt * jnp.exp(cl - cs), e_bf)
    dlast_x = _expand_heads(jnp.exp(cl[0:SUBLANES, :]), e_bf)[0:1, :]
    lane = lax.broadcasted_iota(jnp.int32, (rows, LANES), 1)
    xs_b = xs.astype(BF16)
    for g in range(SSM_GROUPS):
        cg = cm[:, g * SSM_STATE:(g + 1) * SSM_STATE]
        bg = bm[:, g * SSM_STATE:(g + 1) * SSM_STATE]
        cb = lax.dot_general(cg.astype(BF16), bg.astype(BF16), (((1,), (1,)), ((), ())), preferred_element_type=F32)
        for p in range(HEADS_PER_GROUP // 2):
            h0 = g * HEADS_PER_GROUP + 2 * p
            sl = slice((h0 // 2) * LANES, (h0 // 2 + 1) * LANES)
            rhs = jnp.concatenate([xs_b[:, sl], st_s[:, sl].astype(BF16)], axis=0)
            ys = []
            for h in (h0, h0 + 1):
                m_h = _ssd_diag_pair(cb, cs, cs_t, dt_t, mask, h)
                c_h = cg * jnp.broadcast_to(ecs[:, h:h + 1], (rows, SSM_STATE))
                lhs = jnp.concatenate([m_h.astype(BF16), c_h.astype(BF16)], axis=1)
                ys.append(_dot(lhs, rhs))
            y_s[:, sl] = jnp.where(lane < SSM_HEAD_DIM, ys[0], ys[1])
    wc = (xs * coef_x).astype(BF16)
    for g in range(SSM_GROUPS):
        sl = slice(g * GROUP_WIDTH, (g + 1) * GROUP_WIDTH)
        bg = bm[:, g * SSM_STATE:(g + 1) * SSM_STATE].astype(BF16)
        upd = lax.dot_general(bg, wc[:, sl], (((0,), (0,)), ((), ())), preferred_element_type=F32)
        st_s[:, sl] = st_s[:, sl] * dlast_x[:, sl] + upd

    @pl.when(last)
    def _():
        st_ref[0] = st_s[...].T

    _ssd_finish(y_s[...], xs, z_ref, dexp_ref, nwm_ref, mix_ref)


def _mixer_prompt(proj, dt, prm, *, nb, nc):
    rows = CHUNK
    t = nb * nc * rows
    row = lambda b, c: b * nc + c
    full = lambda shape: pl.BlockSpec(shape, lambda b, c: (0,) * len(shape))
    return pl.pallas_call(
        _mixer_prompt_body,
        grid=(nb, nc),
        in_specs=[
            pl.BlockSpec((rows, GATE_WIDTH), lambda b, c: (row(b, c), 0)),
            pl.BlockSpec((rows, GATE_WIDTH), lambda b, c: (row(b, c), 1)),
            pl.BlockSpec((rows, SSM_WIDTH), lambda b, c: (row(b, c), 2)),
            pl.BlockSpec((rows, SSM_CONV_DIM), lambda b, c: (row(b, c), 2)),
            pl.BlockSpec((rows, LANES), lambda b, c: (row(b, c), 0)),
            full((1, GATE_WIDTH)), full((1, GATE_WIDTH)),
            full((GATE_HEADS, rows, rows)), full((rows, GATE_HEADS)),
            full((SSM_CONV, SSM_CONV_DIM)), full((1, SSM_CONV_DIM)),
            full((1, LANES)), full((1, LANES)), full((1, SSM_WIDTH)), full((1, SSM_WIDTH)),
            full((LANES, SSM_WIDTH)),
        ],
        out_specs=[
            pl.BlockSpec((rows, 2 * GATE_WIDTH), lambda b, c: (row(b, c), 0)),
            pl.BlockSpec((1, rows, GATE_WIDTH), lambda b, c: (b, 0, 0)),
            pl.BlockSpec((1, SSM_WIDTH, SSM_STATE), lambda b, c: (b, 0, 0)),
            pl.BlockSpec((1, SSM_CONV - 1, SSM_CONV_DIM), lambda b, c: (b, 0, 0)),
        ],
        out_shape=[
            jax.ShapeDtypeStruct((t, 2 * GATE_WIDTH), BF16),
            jax.ShapeDtypeStruct((nb, rows, GATE_WIDTH), F32),
            jax.ShapeDtypeStruct((nb, SSM_WIDTH, SSM_STATE), F32),
            jax.ShapeDtypeStruct((nb, SSM_CONV - 1, SSM_CONV_DIM), F32),
        ],
        scratch_shapes=[
            pltpu.VMEM((GATE_HEADS, rows, rows), BF16),
            pltpu.VMEM(((SSM_CONV - 1) * rows, 2 * rows), BF16),
            pltpu.VMEM((2 * rows, SSM_CONV_DIM), BF16),
            pltpu.VMEM((SSM_STATE, SSM_WIDTH), F32),
            pltpu.VMEM((rows, SSM_WIDTH), F32),
        ],
        compiler_params=_cparams(("arbitrary", "arbitrary")),
        name="mixer_prompt",
    )(proj, proj, proj, proj, dt, prm["ln_g"], prm["ln_b"], prm["wt_p"], prm["btT_p"], prm["conv_w"], prm["conv_b"],
      prm["dtb"], prm["alog"], prm["dexp"], prm["nwm"], prm["e"])


SEQ_TILE = 8
STATE_SLOTS = 3


def _state_decay_body(dt_ref, dtb_ref, alog_ref, o_ref, *, seg):
    nseq = o_ref.shape[0]
    a = -jnp.exp(alog_ref[...])
    tot = jnp.zeros(o_ref.shape, F32)
    for t in range(seg):
        d = jax.nn.softplus(dt_ref[pl.ds(t, nseq, stride=seg), :] + dtb_ref[...])
        tot = tot + d * a
    o_ref[...] = jnp.exp(tot)


def _state_decay(dt, prm, *, seg):
    nseq = dt.shape[0] // seg
    return pl.pallas_call(
        functools.partial(_state_decay_body, seg=seg),
        out_shape=jax.ShapeDtypeStruct((nseq, LANES), F32),
        name="state_decay",
    )(dt, prm["dtb"], prm["alog"])


def _mixer_sample_body(dec_ref, gu_ref, gv_ref, z_ref, xbc_ref, dt_ref, prev_ref, sin_ref,
                       lng_ref, lnb_ref, wt_ref, btT_ref, cw_ref, cb_ref,
                       dtb_ref, alog_ref, dexp_ref, nwm_ref, e_ref,
                       mix_ref, cv_ref, sout_ref, sc_ref,
                       wm_s, y_s, cm_s, bm_s, wct_s, ex_s, sbuf, sem, *, seg, nsteps):
    i = pl.program_id(0)
    rows = SEQ_TILE * seg
    mask, same = _seg_masks(rows, seg)

    def state_copy(blk):
        slot = blk % STATE_SLOTS
        return pltpu.make_async_copy(sin_ref.at[pl.ds(blk * SEQ_TILE, SEQ_TILE)], sbuf.at[slot], sem.at[slot])

    @pl.when(i == 0)
    def _():
        for k in range(min(STATE_SLOTS - 1, nsteps)):
            state_copy(k).start()

    @pl.when(i + STATE_SLOTS - 1 < nsteps)
    def _():
        state_copy(i + STATE_SLOTS - 1).start()

    @pl.when(i == 0)
    def _():
        for h in range(GATE_HEADS):
            wm_s[h] = jnp.where(mask, wt_ref[h], 0.0).astype(BF16)

    cv_ref[...] = _gate_tile(gu_ref, gv_ref, lng_ref, lnb_ref, wm_s, btT_ref, mix_ref)

    x3 = xbc_ref[...].astype(F32).reshape(SEQ_TILE, seg, SSM_CONV_DIM)
    tpos = lax.broadcasted_iota(jnp.int32, x3.shape, 1)
    conv = cb_ref[...] + cw_ref[SSM_CONV - 1:SSM_CONV, :] * x3
    for d in range(1, SSM_CONV):
        shifted = _seq_shift(x3, prev_ref, slice(0, SEQ_TILE), d, tpos)
        conv = conv + cw_ref[SSM_CONV - 1 - d:SSM_CONV - d, :] * shifted
    act = jax.nn.silu(conv).reshape(rows, SSM_CONV_DIM)
    sc_ref[...] = x3[:, seg - (SSM_CONV - 1):, :]

    maskf = mask.astype(BF16)
    segf = same.astype(BF16)
    xs, bm, cm, dt, cs, cl = _ssd_token_level(act, dt_ref[...], dtb_ref, alog_ref, maskf, segf)
    cs_t = cs.T
    dt_t = dt.T
    e_bf = e_ref[...]
    coef_x = _expand_heads(dt * jnp.exp(cl - cs), e_bf)
    ecs_x = _expand_heads(jnp.exp(cs), e_bf)
    lane = lax.broadcasted_iota(jnp.int32, (rows, LANES), 1)
    xs_b = xs.astype(BF16)
    for g in range(SSM_GROUPS):
        cg = cm[:, g * SSM_STATE:(g + 1) * SSM_STATE]
        bg = bm[:, g * SSM_STATE:(g + 1) * SSM_STATE]
        cb = lax.dot_general(cg.astype(BF16), bg.astype(BF16), (((1,), (1,)), ((), ())), preferred_element_type=F32)
        for p in range(HEADS_PER_GROUP // 2):
            h0 = g * HEADS_PER_GROUP + 2 * p
            sl = slice((h0 // 2) * LANES, (h0 // 2 + 1) * LANES)
            ys = [_dot(_ssd_diag_pair(cb, cs, cs_t, dt_t, mask, h).astype(BF16), xs_b[:, sl]) for h in (h0, h0 + 1)]
            y_s[:, sl] = jnp.where(lane < SSM_HEAD_DIM, ys[0], ys[1])
    cm_s[...] = cm
    bm_s[...] = bm
    wct_s[...] = (xs * coef_x).T
    ex_s[...] = ecs_x

    rowid = lax.broadcasted_iota(jnp.int32, (rows, SSM_STATE), 0)
    state_copy(i).wait()
    s_cur = sbuf.at[i % STATE_SLOTS]

    for s in range(SEQ_TILE):
        r8 = slice(s * seg, (s + 1) * seg)
        for g in range(SSM_GROUPS):
            gsl = slice(g * GROUP_WIDTH, (g + 1) * GROUP_WIDTH)
            nsl = slice(g * SSM_STATE, (g + 1) * SSM_STATE)
            st = s_cur[s, gsl, :]
            c8 = cm_s[r8, nsl].astype(BF16)
            yo = lax.dot_general(c8, st.astype(BF16), (((1,), (1,)), ((), ())), preferred_element_type=F32)
            y_s[r8, gsl] = y_s[r8, gsl] + yo * ex_s[r8, gsl]
            bmask = jnp.where(rowid // seg == s, bm_s[:, nsl], 0.0).astype(BF16)
            upd = _dot(wct_s[gsl, :].astype(BF16), bmask)
            for r in range(HEADS_PER_GROUP):
                d = dec_ref[(i * SEQ_TILE + s) * SSM_HEADS + g * HEADS_PER_GROUP + r]
                hsl = slice(r * SSM_HEAD_DIM, (r + 1) * SSM_HEAD_DIM)
                osl = slice(g * GROUP_WIDTH + r * SSM_HEAD_DIM, g * GROUP_WIDTH + (r + 1) * SSM_HEAD_DIM)
                sout_ref[s, osl, :] = st[hsl, :] * d + upd[hsl, :]

    _ssd_finish(y_s[...], xs, z_ref, dexp_ref, nwm_ref, mix_ref)


def _mixer_sample(dec, proj, dt, prev, state, prm, *, seg):
    rows = SEQ_TILE * seg
    t = proj.shape[0]
    nseq = t // seg
    assert t % rows == 0
    full = lambda shape: pl.BlockSpec(shape, lambda i: (0,) * len(shape))
    return pl.pallas_call(
        functools.partial(_mixer_sample_body, seg=seg, nsteps=t // rows),
        grid=(t // rows,),
        in_specs=[
            pl.BlockSpec(memory_space=pltpu.SMEM),
            pl.BlockSpec((rows, GATE_WIDTH), lambda i: (i, 0)),
            pl.BlockSpec((rows, GATE_WIDTH), lambda i: (i, 1)),
            pl.BlockSpec((rows, SSM_WIDTH), lambda i: (i, 2)),
            pl.BlockSpec((rows, SSM_CONV_DIM), lambda i: (i, 2)),
            pl.BlockSpec((rows, LANES), lambda i: (i, 0)),
            pl.BlockSpec((SEQ_TILE, SSM_CONV - 1, SSM_CONV_DIM), lambda i: (i, 0, 0)),
            pl.BlockSpec(memory_space=pl.ANY),
            full((1, GATE_WIDTH)), full((1, GATE_WIDTH)),
            full((GATE_HEADS, rows, rows)), full((rows, GATE_HEADS)),
            full((SSM_CONV, SSM_CONV_DIM)), full((1, SSM_CONV_DIM)),
            full((1, LANES)), full((1, LANES)), full((1, SSM_WIDTH)), full((1, SSM_WIDTH)),
            full((LANES, SSM_WIDTH)),
        ],
        out_specs=[
            pl.BlockSpec((rows, 2 * GATE_WIDTH), lambda i: (i, 0)),
            pl.BlockSpec((rows, GATE_WIDTH), lambda i: (i, 0)),
            pl.BlockSpec((SEQ_TILE, SSM_WIDTH, SSM_STATE), lambda i: (i, 0, 0)),
            pl.BlockSpec((SEQ_TILE, SSM_CONV - 1, SSM_CONV_DIM), lambda i: (i, 0, 0)),
        ],
        out_shape=[
            jax.ShapeDtypeStruct((t, 2 * GATE_WIDTH), BF16),
            jax.ShapeDtypeStruct((t, GATE_WIDTH), F32),
            jax.ShapeDtypeStruct((nseq, SSM_WIDTH, SSM_STATE), F32),
            jax.ShapeDtypeStruct((nseq, SSM_CONV - 1, SSM_CONV_DIM), F32),
        ],
        scratch_shapes=[
            pltpu.VMEM((GATE_HEADS, rows, rows), BF16),
            pltpu.VMEM((rows, SSM_WIDTH), F32),
            pltpu.VMEM((rows, GROUP_WIDTH), F32),
            pltpu.VMEM((rows, GROUP_WIDTH), F32),
            pltpu.VMEM((SSM_WIDTH, rows), F32),
            pltpu.VMEM((rows, SSM_WIDTH), F32),
            pltpu.VMEM((STATE_SLOTS, SEQ_TILE, SSM_WIDTH, SSM_STATE), F32),
            pltpu.SemaphoreType.DMA((STATE_SLOTS,)),
        ],
        compiler_params=_cparams(("arbitrary",)),
        name="mixer_sample",
    )(dec, proj, proj, proj, proj, dt, prev, state, prm["ln_g"], prm["ln_b"], prm["wt_s"], prm["btT_s"],
      prm["conv_w"], prm["conv_b"], prm["dtb"], prm["alog"], prm["dexp"], prm["nwm"], prm["e"])


ROW_SUB = 128


def _row_subs(tm, sub=ROW_SUB):
    sub = min(sub, tm)
    assert tm % sub == 0
    return [slice(r * sub, (r + 1) * sub) for r in range(tm // sub)]


def _out_proj_body(m_ref, x_ref, w_ref, npost_ref, npre_ref, x1_ref, h2_ref):
    for rs in _row_subs(m_ref.shape[0]):
        mix = _dot(m_ref[rs, :], w_ref[...])
        x1 = x_ref[rs, :] + _rms(mix, npost_ref[...])
        x1_ref[rs, :] = x1
        h2_ref[rs, :] = _rms(x1, npre_ref[...]).astype(h2_ref.dtype)


def _out_proj(mixin, x2d, w_out, npost, npre, *, tm):
    t, k = mixin.shape
    assert t % tm == 0
    return pl.pallas_call(
        _out_proj_body,
        grid=(t // tm,),
        in_specs=[
            pl.BlockSpec((tm, k), lambda i: (i, 0)),
            pl.BlockSpec((tm, D_MODEL), lambda i: (i, 0)),
            pl.BlockSpec((k, D_MODEL), lambda i: (0, 0), pipeline_mode=pl.Buffered(1)),
            pl.BlockSpec((1, D_MODEL), lambda i: (0, 0)),
            pl.BlockSpec((1, D_MODEL), lambda i: (0, 0)),
        ],
        out_specs=[
            pl.BlockSpec((tm, D_MODEL), lambda i: (i, 0)),
            pl.BlockSpec((tm, D_MODEL), lambda i: (i, 0)),
        ],
        out_shape=[
            jax.ShapeDtypeStruct((t, D_MODEL), F32),
            jax.ShapeDtypeStruct((t, D_MODEL), BF16),
        ],
        compiler_params=_cparams(("arbitrary",)),
        name="out_proj",
    )(mixin, x2d, w_out, npost, npre)


def _ffn_conv_taps(cur, shifted, cw_ref, cb_ref):
    out = cb_ref[...] + cw_ref[FFN_CONV - 1:FFN_CONV, :] * cur
    for d in range(1, FFN_CONV):
        out = out + cw_ref[FFN_CONV - 1 - d:FFN_CONV - d, :] * shifted[d - 1]
    return out


def _ffn_up_prompt_body(h_ref, wg_ref, wu_ref, cwg_ref, cwu_ref, cbg_ref, cbu_ref, *rest, tiles_per_seq, n_cast):
    cast_in, (a_ref, stg_ref, stu_ref) = rest[:n_cast], rest[n_cast:n_cast + 3]
    cast_out, (tg_s, tu_s) = rest[n_cast + 3:2 * n_cast + 3], rest[2 * n_cast + 3:]
    i = pl.program_id(1)
    tm = h_ref.shape[0]
    _side_cast(cast_in, cast_out)

    @pl.when((i % tiles_per_seq) == 0)
    def _():
        for t_s in (tg_s, tu_s):
            t_s[...] = jnp.zeros(t_s.shape, F32)

    tails = [tg_s[...], tu_s[...]]
    row8 = lax.broadcasted_iota(jnp.int32, tg_s.shape, 0)
    for rs in _row_subs(tm, 128):
        h = h_ref[rs, :]
        convs = []
        for k, (w_ref, cw_ref, cb_ref) in enumerate(((wg_ref, cwg_ref, cbg_ref), (wu_ref, cwu_ref, cbu_ref))):
            x = _dot(h, w_ref[...])
            shifted = []
            for d in range(1, FFN_CONV):
                r = pltpu.roll(x, d, axis=0)
                head = jnp.where(row8 >= d, r[0:SUBLANES], pltpu.roll(tails[k], d, axis=0))
                shifted.append(jnp.concatenate([head, r[SUBLANES:]], axis=0))
            convs.append(_ffn_conv_taps(x, shifted, cw_ref, cb_ref))
            tails[k] = x[x.shape[0] - SUBLANES:]
        a_ref[rs, :] = (jax.nn.gelu(convs[0], approximate=True) * convs[1]).astype(a_ref.dtype)
    for tail, t_s, st_ref in zip(tails, (tg_s, tu_s), (stg_ref, stu_ref)):
        t_s[...] = tail
        st_ref[0] = tail[SUBLANES - (FFN_CONV - 1):]


def _ffn_up_sample_body(h_ref, wg_ref, wu_ref, cwg_ref, cwu_ref, cbg_ref, cbu_ref, pg_ref, pu_ref,
                        a_ref, stg_ref, stu_ref, *, seg):
    tm = h_ref.shape[0]
    tn = wg_ref.shape[1]
    for rs in _row_subs(tm):
        n = rs.stop - rs.start
        nseq = n // seg
        sq = slice(rs.start // seg, rs.stop // seg)
        h = h_ref[rs, :]
        tpos = lax.broadcasted_iota(jnp.int32, (nseq, seg, tn), 1)
        convs = []
        for w_ref, p_ref, cw_ref, cb_ref, st_ref in ((wg_ref, pg_ref, cwg_ref, cbg_ref, stg_ref),
                                                     (wu_ref, pu_ref, cwu_ref, cbu_ref, stu_ref)):
            x3 = _dot(h, w_ref[...]).reshape(nseq, seg, tn)
            shifted = [_seq_shift(x3, p_ref, sq, d, tpos) for d in range(1, FFN_CONV)]
            convs.append(_ffn_conv_taps(x3, shifted, cw_ref, cb_ref))
            st_ref[sq] = x3[:, seg - (FFN_CONV - 1):, :]
        act = jax.nn.gelu(convs[0], approximate=True) * convs[1]
        a_ref[rs, :] = act.reshape(n, tn).astype(a_ref.dtype)


def _ffn_up(h2, w_up, cw, cb, prev, *, nseq, seg, tm, tn, cast=()):
    t = h2.shape[0]
    nj, ni = D_FF // tn, t // tm
    assert t % tm == 0 and D_FF % tn == 0 and t == nseq * seg
    common_in = [
        pl.BlockSpec((tm, D_MODEL), lambda j, i: (i, 0)),
        pl.BlockSpec((D_MODEL, tn), lambda j, i: (0, j)),
        pl.BlockSpec((D_MODEL, tn), lambda j, i: (0, j + nj)),
        pl.BlockSpec((FFN_CONV, tn), lambda j, i: (0, j)),
        pl.BlockSpec((FFN_CONV, tn), lambda j, i: (0, j + nj)),
        pl.BlockSpec((1, tn), lambda j, i: (0, j)),
        pl.BlockSpec((1, tn), lambda j, i: (0, j + nj)),
    ]
    a_spec = pl.BlockSpec((tm, tn), lambda j, i: (i, j))
    st_shape = jax.ShapeDtypeStruct((nseq, FFN_CONV - 1, D_FF), F32)
    out_shape = [jax.ShapeDtypeStruct((t, D_FF), BF16), st_shape, st_shape]
    if prev is None:
        assert seg % tm == 0
        tps = seg // tm
        c_in, c_out, c_shapes = _side_cast_specs(cast, nj * ni, lambda j, i: j * ni + i)
        return pl.pallas_call(
            functools.partial(_ffn_up_prompt_body, tiles_per_seq=tps, n_cast=len(cast)),
            grid=(nj, ni),
            in_specs=common_in + c_in,
            out_specs=[a_spec] + [pl.BlockSpec((1, FFN_CONV - 1, tn), lambda j, i: (i // tps, 0, j))] * 2 + c_out,
            out_shape=out_shape + c_shapes,
            scratch_shapes=[pltpu.VMEM((SUBLANES, tn), F32), pltpu.VMEM((SUBLANES, tn), F32)],
            compiler_params=_cparams(("arbitrary", "arbitrary")),
            name="ffn_up_prompt",
        )(h2, w_up, w_up, cw, cw, cb, cb, *cast)
    assert tm % seg == 0 and seg == SUBLANES and not cast
    return pl.pallas_call(
        functools.partial(_ffn_up_sample_body, seg=seg),
        grid=(nj, ni),
        in_specs=common_in + [
            pl.BlockSpec((tm // seg, FFN_CONV - 1, tn), lambda j, i: (i, 0, j)),
            pl.BlockSpec((tm // seg, FFN_CONV - 1, tn), lambda j, i: (i, 0, j + nj)),
        ],
        out_specs=[a_spec] + [pl.BlockSpec((tm // seg, FFN_CONV - 1, tn), lambda j, i: (i, 0, j))] * 2,
        out_shape=out_shape,
        compiler_params=_cparams(("arbitrary", "arbitrary")),
        name="ffn_up_sample",
    )(h2, w_up, w_up, cw, cw, cb, cb, prev, prev)


def _ffn_down_body(a_ref, x_ref, w_ref, nw_ref, y_ref):
    for rs in _row_subs(a_ref.shape[0], 128):
        f = _dot(a_ref[rs, :], w_ref[...])
        y_ref[rs, :] = x_ref[rs, :] + _rms(f, nw_ref[...])


def _ffn_down(act, x1, w_down, nw, *, tm):
    t, k = act.shape
    assert t % tm == 0
    return pl.pallas_call(
        _ffn_down_body,
        grid=(t // tm,),
        in_specs=[
            pl.BlockSpec((tm, k), lambda i: (i, 0)),
            pl.BlockSpec((tm, D_MODEL), lambda i: (i, 0)),
            pl.BlockSpec((k, D_MODEL), lambda i: (0, 0), pipeline_mode=pl.Buffered(1)),
            pl.BlockSpec((1, D_MODEL), lambda i: (0, 0)),
        ],
        out_specs=pl.BlockSpec((tm, D_MODEL), lambda i: (i, 0)),
        out_shape=jax.ShapeDtypeStruct((t, D_MODEL), F32),
        compiler_params=_cparams(("arbitrary",)),
        name="ffn_down",
    )(act, x1, w_down, nw)


def _head_expander():
    e = np.zeros((LANES, SSM_WIDTH), np.float32)
    for h in range(SSM_HEADS):
        e[h, h * SSM_HEAD_DIM:(h + 1) * SSM_HEAD_DIM] = 1.0
    return jnp.asarray(e, BF16)


def _pad_lanes(v):
    return jnp.pad(v, (0, LANES - v.shape[0]))[None, :]


def _prep_params(norm_mix_pre, w_in, gate_ln_g, gate_ln_b, gate_w_s, gate_b_s, ssm_conv_w, ssm_conv_b, ssm_dt_bias,
                 ssm_a_log, ssm_d, ssm_norm_w, w_out, norm_mix_post, norm_ffn_pre, ffn_w_up, ffn_conv_w, ffn_conv_b,
                 ffn_w_down, norm_ffn_post, seg_sample):
    w_in_t = jnp.swapaxes(w_in, 0, 1)
    ws_small = gate_w_s[:, :seg_sample, :seg_sample]
    return dict(
        nw_pre=norm_mix_pre[None, :],
        w_in_t=w_in_t,
        w_dt=jnp.pad(w_in_t[PROJ_MAIN:, :], ((0, LANES - SSM_HEADS), (0, 0))).astype(BF16),
        ln_g=gate_ln_g[None, :], ln_b=gate_ln_b[None, :],
        wt_p=gate_w_s, btT_p=gate_b_s.T,
        wt_s=jnp.tile(ws_small, (1, SEQ_TILE, SEQ_TILE)), btT_s=jnp.tile(gate_b_s[:, :seg_sample], (1, SEQ_TILE)).T,
        conv_w=ssm_conv_w, conv_b=ssm_conv_b[None, :],
        dtb=_pad_lanes(ssm_dt_bias), alog=_pad_lanes(ssm_a_log),
        dexp=jnp.repeat(ssm_d, SSM_HEAD_DIM)[None, :], nwm=ssm_norm_w[None, :],
        e=_head_expander(),
        w_out=w_out, n_post=norm_mix_post[None, :], n_pre2=norm_ffn_pre[None, :],
        w_up=ffn_w_up, fcw=ffn_conv_w, fcb=ffn_conv_b[None, :],
        w_down=ffn_w_down, n_post2=norm_ffn_post[None, :],
    )


def _row_tile(t, cap):
    tm = cap
    while t % tm:
        tm //= 2
    assert tm >= 64
    return tm


TM_STREAM = 1024
TM_OUT_PROJ = 512
TM_FFN_DOWN = 512
TM_PRE_NORM = 1024


def _layer_prompt(x, prm):
    nb, seq, _ = x.shape
    assert seq % CHUNK == 0
    nc = seq // CHUNK
    x2d = x.reshape(nb * seq, D_MODEL)
    tm = _row_tile(seq, TM_STREAM)
    h, dt = _pre_norm(x2d, prm["nw_pre"], prm["w_dt"], tm=_row_tile(seq, TM_PRE_NORM))
    proj, w_out, w_up, w_in_b = _in_proj(h, prm["w_in_t"], tm=tm, tn=1024, sub=min(tm, 256),
                                         cast=(prm["w_out"], prm["w_up"]))
    mixin, cv, st, sc = _mixer_prompt(proj, dt, prm, nb=nb, nc=nc)
    x1, h2 = _out_proj(mixin, x2d, w_out, prm["n_post"], prm["n_pre2"], tm=_row_tile(seq, TM_OUT_PROJ))
    act, fst_g, fst_u, w_down = _ffn_up(h2, w_up, prm["fcw"], prm["fcb"], None, nseq=nb, seg=seq, tm=tm, tn=512,
                                        cast=(prm["w_down"],))
    y = _ffn_down(act, x1, w_down, prm["n_post2"], tm=_row_tile(seq, TM_FFN_DOWN))
    return dict(w_in_t=w_in_b, w_out=w_out, w_up=w_up, w_down=w_down), (
            y.reshape(nb, seq, D_MODEL),
            st.reshape(nb, SSM_HEADS, SSM_HEAD_DIM, SSM_STATE),
            sc,
            jnp.concatenate([fst_g, fst_u], axis=-1),
            cv.reshape(nb, CHUNK, GATE_HEADS, GATE_HEAD_DIM))


def _layer_sample(x, state_ssm, state_sconv, state_fconv, prm, wb):
    nb, seg, _ = x.shape
    assert seg == SUBLANES and nb % SEQ_TILE == 0
    t = nb * seg
    x2d = x.reshape(t, D_MODEL)
    tm = _row_tile(t, TM_STREAM)
    h, dt = _pre_norm(x2d, prm["nw_pre"], prm["w_dt"], tm=_row_tile(t, TM_PRE_NORM))
    proj, = _in_proj(h, wb["w_in_t"], tm=tm, tn=1024, sub=min(tm, 256))
    dec = _state_decay(dt, prm, seg=seg)[:, :SSM_HEADS].reshape(nb * SSM_HEADS)
    mixin, cv, st, sc = _mixer_sample(dec, proj, dt, state_sconv,
                                      state_ssm.reshape(nb, SSM_WIDTH, SSM_STATE), prm, seg=seg)
    x1, h2 = _out_proj(mixin, x2d, wb["w_out"], prm["n_post"], prm["n_pre2"], tm=_row_tile(t, TM_OUT_PROJ))
    act, fst_g, fst_u = _ffn_up(h2, wb["w_up"], prm["fcw"], prm["fcb"], state_fconv, nseq=nb, seg=seg, tm=tm, tn=512)
    y = _ffn_down(act, x1, wb["w_down"], prm["n_post2"], tm=_row_tile(t, TM_FFN_DOWN))
    return (y.reshape(nb, seg, D_MODEL),
            st.reshape(nb, SSM_HEADS, SSM_HEAD_DIM, SSM_STATE),
            sc,
            jnp.concatenate([fst_g, fst_u], axis=-1),
            cv.reshape(nb, seg, GATE_HEADS, GATE_HEAD_DIM))


def kernel(x_prompt, x_sample, state_ssm, state_ssm_conv, state_ffn_conv, norm_mix_pre, w_in, gate_ln_g, gate_ln_b,
           gate_w_s, gate_b_s, ssm_conv_w, ssm_conv_b, ssm_dt_bias, ssm_a_log, ssm_d, ssm_norm_w, w_out,
           norm_mix_post, norm_ffn_pre, ffn_w_up, ffn_conv_w, ffn_conv_b, ffn_w_down, norm_ffn_post):
    depth = w_in.shape[0]
    yp, ys = x_prompt, x_sample
    outs_p, outs_s = [], []
    for l in range(depth):
        prm = _prep_params(norm_mix_pre[l], w_in[l], gate_ln_g[l], gate_ln_b[l], gate_w_s[l], gate_b_s[l],
                           ssm_conv_w[l], ssm_conv_b[l], ssm_dt_bias[l], ssm_a_log[l], ssm_d[l], ssm_norm_w[l],
                           w_out[l], norm_mix_post[l], norm_ffn_pre[l], ffn_w_up[l], ffn_conv_w[l], ffn_conv_b[l],
                           ffn_w_down[l], norm_ffn_post[l], x_sample.shape[1])
        wb, (yp, *rest_p) = _layer_prompt(yp, prm)
        ys, *rest_s = _layer_sample(ys, state_ssm[l], state_ssm_conv[l], state_ffn_conv[l], prm, wb)
        outs_p.append(rest_p)
        outs_s.append(rest_s)
    stack = lambda outs, k: jnp.stack([o[k] for o in outs])
    return (yp, ys,
            stack(outs_p, 0), stack(outs_p, 1), stack(outs_p, 2), stack(outs_p, 3),
            stack(outs_s, 0), stack(outs_s, 1), stack(outs_s, 2), stack(outs_s, 3))
```

```python
import functools

import jax
import jax.numpy as jnp
import numpy as np
from jax import lax
from jax.experimental import pallas as pl
from jax.experimental.pallas import tpu as pltpu

F32 = jnp.float32
BF16 = jnp.bfloat16

D_MODEL = 2048
GATE_WIDTH = 2048
GATE_HEADS = 16
GATE_HEAD_DIM = 128
CHUNK = 128
SSM_WIDTH = 2048
SSM_HEAD_DIM = 64
SSM_HEADS = 32
SSM_GROUPS = 4
SSM_STATE = 128
SSM_CONV = 4
SSM_CONV_DIM = SSM_WIDTH + 2 * SSM_GROUPS * SSM_STATE
PROJ_MAIN = 2 * GATE_WIDTH + SSM_WIDTH + SSM_CONV_DIM
D_FF = 5632
FFN_CONV = 3
EPS = 1e-6
HEADS_PER_GROUP = SSM_HEADS // SSM_GROUPS
GROUP_WIDTH = SSM_WIDTH // SSM_GROUPS

LANES = 128
SUBLANES = 8
VMEM_LIMIT = 56 * 1024 * 1024


def _cparams(sem):
    return pltpu.CompilerParams(dimension_semantics=sem, vmem_limit_bytes=VMEM_LIMIT)


def _rms(x, w):
    return x * lax.rsqrt(jnp.mean(x * x, axis=-1, keepdims=True) + EPS) * w


def _gelu_erf(x):
    return 0.5 * x * (1.0 + lax.erf(x * np.float32(0.7071067811865476)))


def _split_bf16(x, n):
    parts = []
    r = x
    for k in range(n):
        p = r.astype(BF16)
        parts.append(p)
        if k + 1 < n:
            r = r - p.astype(F32)
    return parts


def _dot(a, b):
    return jnp.dot(a, b, preferred_element_type=F32)


def _dot_nt(a, b):
    return lax.dot_general(a, b, (((1,), (1,)), ((), ())), preferred_element_type=F32)


def _mm_split(m_bf, x, n):
    acc = None
    for p in _split_bf16(x, n):
        t = _dot(m_bf, p)
        acc = t if acc is None else acc + t
    return acc


def _expand_heads(x, e_bf):
    acc = None
    for p in _split_bf16(x, 2):
        t = _dot(p, e_bf)
        acc = t if acc is None else acc + t
    return acc


def _seq_shift(x3, p_ref, sq, d, tpos):
    k1 = p_ref.shape[1]
    out = pltpu.roll(x3, d, axis=1)
    for t in range(d):
        out = jnp.where(tpos == t, jnp.broadcast_to(p_ref[sq, k1 + t - d:k1 + t - d + 1, :], x3.shape), out)
    return out


def _seg_masks(rows, seg):
    r = lax.broadcasted_iota(jnp.int32, (rows, rows), 0)
    c = lax.broadcasted_iota(jnp.int32, (rows, rows), 1)
    same = (r // seg) == (c // seg)
    return same & (c <= r), same


BF16_ROWS = 2 * SUBLANES


def _cast_rows(rows, nsteps):
    per = BF16_ROWS
    while rows % per or rows // per > nsteps:
        per += BF16_ROWS
        assert per <= rows
    return per


def _side_cast_specs(weights, nsteps, step_of):
    ins, outs, shapes = [], [], []
    for w in weights:
        rows, cols = w.shape
        per = _cast_rows(rows, nsteps)
        spec = pl.BlockSpec((per, cols), lambda *g, nb=rows // per: (jnp.minimum(step_of(*g), nb - 1), 0))
        ins.append(spec)
        outs.append(spec)
        shapes.append(jax.ShapeDtypeStruct((rows, cols), BF16))
    return ins, outs, shapes


def _side_cast(cast_in, cast_out):
    for w_ref, o_ref in zip(cast_in, cast_out):
        o_ref[...] = w_ref[...].astype(BF16)


def _pre_norm_body(xa_ref, xb_ref, nw_ref, wdt_ref, h_ref, dt_ref, *, na):
    i = pl.program_id(0)

    def emit(x_ref):
        h = _rms(x_ref[...], nw_ref[...]).astype(BF16)
        h_ref[...] = h
        dt_ref[...] = _dot_nt(h, wdt_ref[...])

    @pl.when(i < na)
    def _():
        emit(xa_ref)

    @pl.when(i >= na)
    def _():
        emit(xb_ref)


def _pre_norm(xa, xb, nw, w_dt, *, tm):
    ta, tb = xa.shape[0], xb.shape[0]
    assert ta % tm == 0 and tb % tm == 0
    na, t = ta // tm, ta + tb
    return pl.pallas_call(
        functools.partial(_pre_norm_body, na=na),
        grid=(t // tm,),
        in_specs=[
            pl.BlockSpec((tm, D_MODEL), lambda i: (jnp.minimum(i, na - 1), 0)),
            pl.BlockSpec((tm, D_MODEL), lambda i: (jnp.maximum(i - na, 0), 0)),
            pl.BlockSpec((1, D_MODEL), lambda i: (0, 0)),
            pl.BlockSpec((LANES, D_MODEL), lambda i: (0, 0)),
        ],
        out_specs=[
            pl.BlockSpec((tm, D_MODEL), lambda i: (i, 0)),
            pl.BlockSpec((tm, LANES), lambda i: (i, 0)),
        ],
        out_shape=[
            jax.ShapeDtypeStruct((t, D_MODEL), BF16),
            jax.ShapeDtypeStruct((t, LANES), F32),
        ],
        compiler_params=_cparams(("arbitrary",)),
        name="pre_norm",
    )(xa, xb, nw, w_dt)


def _in_proj_body(h_ref, w_ref, *rest, n_gelu, n_silu, sub, n_cast):
    cast_in, o_ref, cast_out, wb_s = rest[:n_cast], rest[n_cast], rest[n_cast + 1:2 * n_cast + 1], rest[-1]
    j = pl.program_id(0)
    i = pl.program_id(1)
    tm = h_ref.shape[0]
    _side_cast(cast_in, cast_out)

    @pl.when(i == 0)
    def _():
        wb_s[...] = w_ref[...].astype(BF16)

    def run(epilogue):
        for rs in _row_subs(tm, sub):
            o_ref[rs, :] = epilogue(_dot_nt(h_ref[rs, :], wb_s[...])).astype(o_ref.dtype)

    @pl.when(j < n_gelu)
    def _():
        run(_gelu_erf)

    @pl.when((j >= n_gelu) & (j < n_gelu + n_silu))
    def _():
        run(jax.nn.silu)

    @pl.when(j >= n_gelu + n_silu)
    def _():
        run(lambda a: a)


def _in_proj(h, w_in_t, *, tm, tn, sub, cast=()):
    t = h.shape[0]
    assert t % tm == 0 and PROJ_MAIN % tn == 0 and GATE_WIDTH % tn == 0 and SSM_WIDTH % tn == 0 and tm % sub == 0
    nj, ni = PROJ_MAIN // tn, t // tm
    c_in, c_out, c_shapes = _side_cast_specs(cast, nj * ni, lambda j, i: j * ni + i)
    w_spec = pl.BlockSpec((tn, D_MODEL), lambda j, i: (j, 0))
    return pl.pallas_call(
        functools.partial(_in_proj_body, n_gelu=2 * GATE_WIDTH // tn, n_silu=SSM_WIDTH // tn, sub=sub,
                          n_cast=len(cast)),
        grid=(nj, ni),
        in_specs=[pl.BlockSpec((tm, D_MODEL), lambda j, i: (i, 0)), w_spec] + c_in,
        out_specs=[pl.BlockSpec((tm, tn), lambda j, i: (i, j))] + c_out,
        out_shape=[jax.ShapeDtypeStruct((t, PROJ_MAIN), BF16)] + c_shapes,
        scratch_shapes=[pltpu.VMEM((tn, D_MODEL), BF16)],
        compiler_params=_cparams(("arbitrary", "arbitrary")),
        name="in_proj",
    )(h, w_in_t, *cast)


def _gate_tile(gu_ref, gv_ref, lng_ref, lnb_ref, wm_s, btT_ref, mix_ref):
    g = gv_ref[...].astype(F32)
    mu = jnp.mean(g, axis=-1, keepdims=True)
    xc = g - mu
    v = xc * lax.rsqrt(jnp.mean(xc * xc, axis=-1, keepdims=True) + EPS) * lng_ref[...] + lnb_ref[...]
    vb = v.astype(BF16)
    rows = v.shape[0]
    for h in range(GATE_HEADS):
        sl = slice(h * GATE_HEAD_DIM, (h + 1) * GATE_HEAD_DIM)
        s = _dot(wm_s[h], vb[:, sl]) + jnp.broadcast_to(btT_ref[:, h:h + 1], (rows, GATE_HEAD_DIM))
        mix_ref[:, sl] = gu_ref[:, sl] * s.astype(mix_ref.dtype)
    return v


def _ssd_token_level(act, dt_raw, dtb_ref, alog_ref, maskf, segf):
    xs = act[:, :SSM_WIDTH]
    bm = act[:, SSM_WIDTH:SSM_WIDTH + GROUP_WIDTH]
    cm = act[:, SSM_WIDTH + GROUP_WIDTH:]
    dt = jax.nn.softplus(dt_raw + dtb_ref[...])
    a = -jnp.exp(alog_ref[...])
    da = dt * a
    cs = _mm_split(maskf, da, 3)
    cl = _mm_split(segf, da, 3)
    return xs, bm, cm, dt, cs, cl


def _ssd_diag_pair(cb, cs, cs_t, dt_t, mask, h):
    seg = cs[:, h:h + 1] - cs_t[h:h + 1, :]
    decay = jnp.exp(jnp.where(mask, seg, -jnp.inf))
    return cb * decay * dt_t[h:h + 1, :]


def _ssd_finish(y, xs, z_ref, dexp_ref, nwm_ref, mix_ref):
    y = y + dexp_ref[...] * xs
    y = y * z_ref[...].astype(F32)
    for g in range(SSM_GROUPS):
        sl = slice(g * GROUP_WIDTH, (g + 1) * GROUP_WIDTH)
        yg = y[:, sl]
        yg = yg * lax.rsqrt(jnp.mean(yg * yg, axis=-1, keepdims=True) + EPS) * nwm_ref[:, sl]
        mix_ref[:, GATE_WIDTH + g * GROUP_WIDTH:GATE_WIDTH + (g + 1) * GROUP_WIDTH] = yg.astype(mix_ref.dtype)


def _mixer_prompt_body(gu_ref, gv_ref, z_ref, xbc_ref, dt_ref, lng_ref, lnb_ref, wt_ref, btT_ref, cw_ref, cb_ref,
                       dtb_ref, alog_ref, dexp_ref, nwm_ref, e_ref,
                       mix_ref, cv_ref, st_ref, sc_ref,
                       wm_s, shift_s, xx_s, st_s, y_s):
    b = pl.program_id(0)
    c = pl.program_id(1)
    last = c == pl.num_programs(1) - 1
    rows = CHUNK
    keep = 2 * SUBLANES
    mask, same = _seg_masks(rows, rows)

    @pl.when((b == 0) & (c == 0))
    def _():
        for h in range(GATE_HEADS):
            wm_s[h] = jnp.where(mask, wt_ref[h], 0.0).astype(BF16)
        r = lax.broadcasted_iota(jnp.int32, shift_s.shape, 0)
        col = lax.broadcasted_iota(jnp.int32, shift_s.shape, 1)
        shift_s[...] = jnp.where(col == rows + r % rows - (r // rows + 1), 1.0, 0.0).astype(BF16)
        xx_s[0:rows, :] = jnp.zeros((rows, SSM_CONV_DIM), BF16)

    @pl.when(c == 0)
    def _():
        xx_s[rows - keep:rows, :] = jnp.zeros((keep, SSM_CONV_DIM), BF16)
        st_s[...] = jnp.zeros(st_s.shape, F32)

    v = _gate_tile(gu_ref, gv_ref, lng_ref, lnb_ref, wm_s, btT_ref, mix_ref)

    @pl.when(last)
    def _():
        cv_ref[0] = v

    xx_s[rows:2 * rows, :] = xbc_ref[...]
    sh = _dot(shift_s[...], xx_s[...])
    conv = cb_ref[...] + cw_ref[SSM_CONV - 1:SSM_CONV, :] * xbc_ref[...].astype(F32)
    for d in range(1, SSM_CONV):
        conv = conv + cw_ref[SSM_CONV - 1 - d:SSM_CONV - d, :] * sh[(d - 1) * rows:d * rows, :]
    act = jax.nn.silu(conv)

    @pl.when(last)
    def _():
        sc_ref[0] = xbc_ref[rows - (SSM_CONV - 1):rows, :].astype(F32)

    xx_s[rows - keep:rows, :] = xx_s[2 * rows - keep:2 * rows, :]

    maskf = mask.astype(BF16)
    segf = same.astype(BF16)
    xs, bm, cm, dt, cs, cl = _ssd_token_level(act, dt_ref[...], dtb_ref, alog_ref, maskf, segf)
    cs_t = cs.T
    dt_t = dt.T
    ecs = jnp.exp(cs)
    e_bf = e_ref[...]
    coef_x = _expand_heads(dt * jnp.exp(cl - cs), e_bf)
    dlast_x = _expand_heads(jnp.exp(cl[0:SUBLANES, :]), e_bf)[0:1, :]
    lane = lax.broadcasted_iota(jnp.int32, (rows, LANES), 1)
    xs_b = xs.astype(BF16)
    for g in range(SSM_GROUPS):
        cg = cm[:, g * SSM_STATE:(g + 1) * SSM_STATE]
        bg = bm[:, g * SSM_STATE:(g + 1) * SSM_STATE]
        cb = lax.dot_general(cg.astype(BF16), bg.astype(BF16), (((1,), (1,)), ((), ())), preferred_element_type=F32)
        for p in range(HEADS_PER_GROUP // 2):
            h0 = g * HEADS_PER_GROUP + 2 * p
            sl = slice((h0 // 2) * LANES, (h0 // 2 + 1) * LANES)
            rhs = jnp.concatenate([xs_b[:, sl], st_s[:, sl].astype(BF16)], axis=0)
            ys = []
            for h in (h0, h0 + 1):
                m_h = _ssd_diag_pair(cb, cs, cs_t, dt_t, mask, h)
                c_h = cg * jnp.broadcast_to(ecs[:, h:h + 1], (rows, SSM_STATE))
                lhs = jnp.concatenate([m_h.astype(BF16), c_h.astype(BF16)], axis=1)
                ys.append(_dot(lhs, rhs))
            y_s[:, sl] = jnp.where(lane < SSM_HEAD_DIM, ys[0], ys[1])
    wc = (xs * coef_x).astype(BF16)
    for g in range(SSM_GROUPS):
        sl = slice(g * GROUP_WIDTH, (g + 1) * GROUP_WIDTH)
        bg = bm[:, g * SSM_STATE:(g + 1) * SSM_STATE].astype(BF16)
        upd = lax.dot_general(bg, wc[:, sl], (((0,), (0,)), ((), ())), preferred_element_type=F32)
        st_s[:, sl] = st_s[:, sl] * dlast_x[:, sl] + upd

    @pl.when(last)
    def _():
        st_ref[0] = st_s[...].T

    _ssd_finish(y_s[...], xs, z_ref, dexp_ref, nwm_ref, mix_ref)


def _mixer_prompt(proj, dt, prm, *, nb, nc):
    rows = CHUNK
    t = nb * nc * rows
    row = lambda b, c: b * nc + c
    full = lambda shape: pl.BlockSpec(shape, lambda b, c: (0,) * len(shape))
    return pl.pallas_call(
        _mixer_prompt_body,
        grid=(nb, nc),
        in_specs=[
            pl.BlockSpec((rows, GATE_WIDTH), lambda b, c: (row(b, c), 0)),
            pl.BlockSpec((rows, GATE_WIDTH), lambda b, c: (row(b, c), 1)),
            pl.BlockSpec((rows, SSM_WIDTH), lambda b, c: (row(b, c), 2)),
            pl.BlockSpec((rows, SSM_CONV_DIM), lambda b, c: (row(b, c), 2)),
            pl.BlockSpec((rows, LANES), lambda b, c: (row(b, c), 0)),
            full((1, GATE_WIDTH)), full((1, GATE_WIDTH)),
            full((GATE_HEADS, rows, rows)), full((rows, GATE_HEADS)),
            full((SSM_CONV, SSM_CONV_DIM)), full((1, SSM_CONV_DIM)),
            full((1, LANES)), full((1, LANES)), full((1, SSM_WIDTH)), full((1, SSM_WIDTH)),
            full((LANES, SSM_WIDTH)),
        ],
        out_specs=[
            pl.BlockSpec((rows, 2 * GATE_WIDTH), lambda b, c: (row(b, c), 0)),
            pl.BlockSpec((1, rows, GATE_WIDTH), lambda b, c: (b, 0, 0)),
            pl.BlockSpec((1, SSM_WIDTH, SSM_STATE), lambda b, c: (b, 0, 0)),
            pl.BlockSpec((1, SSM_CONV - 1, SSM_CONV_DIM), lambda b, c: (b, 0, 0)),
        ],
        out_shape=[
            jax.ShapeDtypeStruct((t, 2 * GATE_WIDTH), BF16),
            jax.ShapeDtypeStruct((nb, rows, GATE_WIDTH), F32),
            jax.ShapeDtypeStruct((nb, SSM_WIDTH, SSM_STATE), F32),
            jax.ShapeDtypeStruct((nb, SSM_CONV - 1, SSM_CONV_DIM), F32),
        ],
        scratch_shapes=[
            pltpu.VMEM((GATE_HEADS, rows, rows), BF16),
            pltpu.VMEM(((SSM_CONV - 1) * rows, 2 * rows), BF16),
            pltpu.VMEM((2 * rows, SSM_CONV_DIM), BF16),
            pltpu.VMEM((SSM_STATE, SSM_WIDTH), F32),
            pltpu.VMEM((rows, SSM_WIDTH), F32),
        ],
        compiler_params=_cparams(("arbitrary", "arbitrary")),
        name="mixer_prompt",
    )(proj, proj, proj, proj, dt, prm["ln_g"], prm["ln_b"], prm["wt_p"], prm["btT_p"], prm["conv_w"], prm["conv_b"],
      prm["dtb"], prm["alog"], prm["dexp"], prm["nwm"], prm["e"])


SEQ_TILE = 8


def _state_decay_body(dt_ref, dtb_ref, alog_ref, o_ref, *, seg):
    nseq = o_ref.shape[0]
    a = -jnp.exp(alog_ref[...])
    tot = jnp.zeros(o_ref.shape, F32)
    for t in range(seg):
        d = jax.nn.softplus(dt_ref[pl.ds(t, nseq, stride=seg), :] + dtb_ref[...])
        tot = tot + d * a
    o_ref[...] = jnp.exp(tot)


def _state_decay(dt, prm, *, seg):
    nseq = dt.shape[0] // seg
    return pl.pallas_call(
        functools.partial(_state_decay_body, seg=seg),
        out_shape=jax.ShapeDtypeStruct((nseq, LANES), F32),
        name="state_decay",
    )(dt, prm["dtb"], prm["alog"])


def _mixer_sample_body(dec_ref, gu_ref, gv_ref, z_ref, xbc_ref, dt_ref, prev_ref, sin_ref,
                       lng_ref, lnb_ref, wt_ref, btT_ref, cw_ref, cb_ref,
                       dtb_ref, alog_ref, dexp_ref, nwm_ref, e_ref,
                       mix_ref, cv_ref, sout_ref, sc_ref,
                       wm_s, y_s, cm_s, bm_s, wct_s, ex_s, *, seg):
    i = pl.program_id(0)
    rows = SEQ_TILE * seg
    mask, same = _seg_masks(rows, seg)

    @pl.when(i == 0)
    def _():
        for h in range(GATE_HEADS):
            wm_s[h] = jnp.where(mask, wt_ref[h], 0.0).astype(BF16)

    cv_ref[...] = _gate_tile(gu_ref, gv_ref, lng_ref, lnb_ref, wm_s, btT_ref, mix_ref)

    x3 = xbc_ref[...].astype(F32).reshape(SEQ_TILE, seg, SSM_CONV_DIM)
    tpos = lax.broadcasted_iota(jnp.int32, x3.shape, 1)
    conv = cb_ref[...] + cw_ref[SSM_CONV - 1:SSM_CONV, :] * x3
    for d in range(1, SSM_CONV):
        shifted = _seq_shift(x3, prev_ref, slice(0, SEQ_TILE), d, tpos)
        conv = conv + cw_ref[SSM_CONV - 1 - d:SSM_CONV - d, :] * shifted
    act = jax.nn.silu(conv).reshape(rows, SSM_CONV_DIM)
    sc_ref[...] = x3[:, seg - (SSM_CONV - 1):, :]

    maskf = mask.astype(BF16)
    segf = same.astype(BF16)
    xs, bm, cm, dt, cs, cl = _ssd_token_level(act, dt_ref[...], dtb_ref, alog_ref, maskf, segf)
    cs_t = cs.T
    dt_t = dt.T
    e_bf = e_ref[...]
    coef_x = _expand_heads(dt * jnp.exp(cl - cs), e_bf)
    ecs_x = _expand_heads(jnp.exp(cs), e_bf)
    lane = lax.broadcasted_iota(jnp.int32, (rows, LANES), 1)
    xs_b = xs.astype(BF16)
    for g in range(SSM_GROUPS):
        cg = cm[:, g * SSM_STATE:(g + 1) * SSM_STATE]
        bg = bm[:, g * SSM_STATE:(g + 1) * SSM_STATE]
        cb = lax.dot_general(cg.astype(BF16), bg.astype(BF16), (((1,), (1,)), ((), ())), preferred_element_type=F32)
        for p in range(HEADS_PER_GROUP // 2):
            h0 = g * HEADS_PER_GROUP + 2 * p
            sl = slice((h0 // 2) * LANES, (h0 // 2 + 1) * LANES)
            ys = [_dot(_ssd_diag_pair(cb, cs, cs_t, dt_t, mask, h).astype(BF16), xs_b[:, sl]) for h in (h0, h0 + 1)]
            y_s[:, sl] = jnp.where(lane < SSM_HEAD_DIM, ys[0], ys[1])
    cm_s[...] = cm
    bm_s[...] = bm
    wct_s[...] = (xs * coef_x).T
    ex_s[...] = ecs_x

    rowid = lax.broadcasted_iota(jnp.int32, (rows, SSM_STATE), 0)

    for s in range(SEQ_TILE):
        r8 = slice(s * seg, (s + 1) * seg)
        for g in range(SSM_GROUPS):
            gsl = slice(g * GROUP_WIDTH, (g + 1) * GROUP_WIDTH)
            nsl = slice(g * SSM_STATE, (g + 1) * SSM_STATE)
            st = sin_ref[s, gsl, :]
            c8 = cm_s[r8, nsl].astype(BF16)
            yo = lax.dot_general(c8, st.astype(BF16), (((1,), (1,)), ((), ())), preferred_element_type=F32)
            y_s[r8, gsl] = y_s[r8, gsl] + yo * ex_s[r8, gsl]
            bmask = jnp.where(rowid // seg == s, bm_s[:, nsl], 0.0).astype(BF16)
            upd = _dot(wct_s[gsl, :].astype(BF16), bmask)
            for r in range(HEADS_PER_GROUP):
                d = dec_ref[(i * SEQ_TILE + s) * SSM_HEADS + g * HEADS_PER_GROUP + r]
                hsl = slice(r * SSM_HEAD_DIM, (r + 1) * SSM_HEAD_DIM)
                osl = slice(g * GROUP_WIDTH + r * SSM_HEAD_DIM, g * GROUP_WIDTH + (r + 1) * SSM_HEAD_DIM)
                sout_ref[s, osl, :] = st[hsl, :] * d + upd[hsl, :]

    _ssd_finish(y_s[...], xs, z_ref, dexp_ref, nwm_ref, mix_ref)


def _mixer_sample(dec, proj, dt, row0, prev, state, prm, *, seg):
    rows = SEQ_TILE * seg
    nseq = state.shape[0]
    t = nseq * seg
    assert t % rows == 0 and row0 % rows == 0
    r0 = row0 // rows
    full = lambda shape: pl.BlockSpec(shape, lambda i: (0,) * len(shape))
    return pl.pallas_call(
        functools.partial(_mixer_sample_body, seg=seg),
        grid=(t // rows,),
        in_specs=[
            pl.BlockSpec(memory_space=pltpu.SMEM),
            pl.BlockSpec((rows, GATE_WIDTH), lambda i: (i + r0, 0)),
            pl.BlockSpec((rows, GATE_WIDTH), lambda i: (i + r0, 1)),
            pl.BlockSpec((rows, SSM_WIDTH), lambda i: (i + r0, 2)),
            pl.BlockSpec((rows, SSM_CONV_DIM), lambda i: (i + r0, 2)),
            pl.BlockSpec((rows, LANES), lambda i: (i + r0, 0)),
            pl.BlockSpec((SEQ_TILE, SSM_CONV - 1, SSM_CONV_DIM), lambda i: (i, 0, 0)),
            pl.BlockSpec((SEQ_TILE, SSM_WIDTH, SSM_STATE), lambda i: (i, 0, 0)),
            full((1, GATE_WIDTH)), full((1, GATE_WIDTH)),
            full((GATE_HEADS, rows, rows)), full((rows, GATE_HEADS)),
            full((SSM_CONV, SSM_CONV_DIM)), full((1, SSM_CONV_DIM)),
            full((1, LANES)), full((1, LANES)), full((1, SSM_WIDTH)), full((1, SSM_WIDTH)),
            full((LANES, SSM_WIDTH)),
        ],
        out_specs=[
            pl.BlockSpec((rows, 2 * GATE_WIDTH), lambda i: (i, 0)),
            pl.BlockSpec((rows, GATE_WIDTH), lambda i: (i, 0)),
            pl.BlockSpec((SEQ_TILE, SSM_WIDTH, SSM_STATE), lambda i: (i, 0, 0)),
            pl.BlockSpec((SEQ_TILE, SSM_CONV - 1, SSM_CONV_DIM), lambda i: (i, 0, 0)),
        ],
        out_shape=[
            jax.ShapeDtypeStruct((t, 2 * GATE_WIDTH), BF16),
            jax.ShapeDtypeStruct((t, GATE_WIDTH), F32),
            jax.ShapeDtypeStruct((nseq, SSM_WIDTH, SSM_STATE), F32),
            jax.ShapeDtypeStruct((nseq, SSM_CONV - 1, SSM_CONV_DIM), F32),
        ],
        scratch_shapes=[
            pltpu.VMEM((GATE_HEADS, rows, rows), BF16),
            pltpu.VMEM((rows, SSM_WIDTH), F32),
            pltpu.VMEM((rows, GROUP_WIDTH), F32),
            pltpu.VMEM((rows, GROUP_WIDTH), F32),
            pltpu.VMEM((SSM_WIDTH, rows), F32),
            pltpu.VMEM((rows, SSM_WIDTH), F32),
        ],
        compiler_params=_cparams(("arbitrary",)),
        name="mixer_sample",
    )(dec, proj, proj, proj, proj, dt, prev, state, prm["ln_g"], prm["ln_b"], prm["wt_s"], prm["btT_s"],
      prm["conv_w"], prm["conv_b"], prm["dtb"], prm["alog"], prm["dexp"], prm["nwm"], prm["e"])


ROW_SUB = 128


def _row_subs(tm, sub=ROW_SUB):
    sub = min(sub, tm)
    assert tm % sub == 0
    return [slice(r * sub, (r + 1) * sub) for r in range(tm // sub)]


def _out_proj_body(m_ref, x_ref, w_ref, npost_ref, npre_ref, x1_ref, h2_ref):
    for rs in _row_subs(m_ref.shape[0]):
        mix = _dot(m_ref[rs, :], w_ref[...])
        x1 = x_ref[rs, :] + _rms(mix, npost_ref[...])
        x1_ref[rs, :] = x1
        h2_ref[rs, :] = _rms(x1, npre_ref[...]).astype(h2_ref.dtype)


def _out_proj(mixin, x2d, w_out, npost, npre, *, tm):
    t, k = mixin.shape
    assert t % tm == 0
    return pl.pallas_call(
        _out_proj_body,
        grid=(t // tm,),
        in_specs=[
            pl.BlockSpec((tm, k), lambda i: (i, 0)),
            pl.BlockSpec((tm, D_MODEL), lambda i: (i, 0)),
            pl.BlockSpec((k, D_MODEL), lambda i: (0, 0), pipeline_mode=pl.Buffered(1)),
            pl.BlockSpec((1, D_MODEL), lambda i: (0, 0)),
            pl.BlockSpec((1, D_MODEL), lambda i: (0, 0)),
        ],
        out_specs=[
            pl.BlockSpec((tm, D_MODEL), lambda i: (i, 0)),
            pl.BlockSpec((tm, D_MODEL), lambda i: (i, 0)),
        ],
        out_shape=[
            jax.ShapeDtypeStruct((t, D_MODEL), F32),
            jax.ShapeDtypeStruct((t, D_MODEL), BF16),
        ],
        compiler_params=_cparams(("arbitrary",)),
        name="out_proj",
    )(mixin, x2d, w_out, npost, npre)


def _ffn_conv_taps(cur, shifted, cw_ref, cb_ref):
    out = cb_ref[...] + cw_ref[FFN_CONV - 1:FFN_CONV, :] * cur
    for d in range(1, FFN_CONV):
        out = out + cw_ref[FFN_CONV - 1 - d:FFN_CONV - d, :] * shifted[d - 1]
    return out


def _ffn_up_prompt_body(h_ref, wg_ref, wu_ref, cwg_ref, cwu_ref, cbg_ref, cbu_ref, *rest, tiles_per_seq, n_cast):
    cast_in, (a_ref, stg_ref, stu_ref) = rest[:n_cast], rest[n_cast:n_cast + 3]
    cast_out, (tg_s, tu_s) = rest[n_cast + 3:2 * n_cast + 3], rest[2 * n_cast + 3:]
    i = pl.program_id(1)
    tm = h_ref.shape[0]
    _side_cast(cast_in, cast_out)

    @pl.when((i % tiles_per_seq) == 0)
    def _():
        for t_s in (tg_s, tu_s):
            t_s[...] = jnp.zeros(t_s.shape, F32)

    tails = [tg_s[...], tu_s[...]]
    row8 = lax.broadcasted_iota(jnp.int32, tg_s.shape, 0)
    for rs in _row_subs(tm, 128):
        h = h_ref[rs, :]
        convs = []
        for k, (w_ref, cw_ref, cb_ref) in enumerate(((wg_ref, cwg_ref, cbg_ref), (wu_ref, cwu_ref, cbu_ref))):
            x = _dot(h, w_ref[...])
            shifted = []
            for d in range(1, FFN_CONV):
                r = pltpu.roll(x, d, axis=0)
                head = jnp.where(row8 >= d, r[0:SUBLANES], pltpu.roll(tails[k], d, axis=0))
                shifted.append(jnp.concatenate([head, r[SUBLANES:]], axis=0))
            convs.append(_ffn_conv_taps(x, shifted, cw_ref, cb_ref))
            tails[k] = x[x.shape[0] - SUBLANES:]
        a_ref[rs, :] = (jax.nn.gelu(convs[0], approximate=True) * convs[1]).astype(a_ref.dtype)
    for tail, t_s, st_ref in zip(tails, (tg_s, tu_s), (stg_ref, stu_ref)):
        t_s[...] = tail
        st_ref[0] = tail[SUBLANES - (FFN_CONV - 1):]


def _ffn_up_sample_body(h_ref, wg_ref, wu_ref, cwg_ref, cwu_ref, cbg_ref, cbu_ref, pg_ref, pu_ref,
                        a_ref, stg_ref, stu_ref, *, seg):
    tm = h_ref.shape[0]
    tn = wg_ref.shape[1]
    for rs in _row_subs(tm):
        n = rs.stop - rs.start
        nseq = n // seg
        sq = slice(rs.start // seg, rs.stop // seg)
        h = h_ref[rs, :]
        tpos = lax.broadcasted_iota(jnp.int32, (nseq, seg, tn), 1)
        convs = []
        for w_ref, p_ref, cw_ref, cb_ref, st_ref in ((wg_ref, pg_ref, cwg_ref, cbg_ref, stg_ref),
                                                     (wu_ref, pu_ref, cwu_ref, cbu_ref, stu_ref)):
            x3 = _dot(h, w_ref[...]).reshape(nseq, seg, tn)
            shifted = [_seq_shift(x3, p_ref, sq, d, tpos) for d in range(1, FFN_CONV)]
            convs.append(_ffn_conv_taps(x3, shifted, cw_ref, cb_ref))
            st_ref[sq] = x3[:, seg - (FFN_CONV - 1):, :]
        act = jax.nn.gelu(convs[0], approximate=True) * convs[1]
        a_ref[rs, :] = act.reshape(n, tn).astype(a_ref.dtype)


def _ffn_up(h2, w_up, cw, cb, prev, *, nseq, seg, tm, tn, cast=()):
    t = h2.shape[0]
    nj, ni = D_FF // tn, t // tm
    assert t % tm == 0 and D_FF % tn == 0 and t == nseq * seg
    common_in = [
        pl.BlockSpec((tm, D_MODEL), lambda j, i: (i, 0)),
        pl.BlockSpec((D_MODEL, tn), lambda j, i: (0, j)),
        pl.BlockSpec((D_MODEL, tn), lambda j, i: (0, j + nj)),
        pl.BlockSpec((FFN_CONV, tn), lambda j, i: (0, j)),
        pl.BlockSpec((FFN_CONV, tn), lambda j, i: (0, j + nj)),
        pl.BlockSpec((1, tn), lambda j, i: (0, j)),
        pl.BlockSpec((1, tn), lambda j, i: (0, j + nj)),
    ]
    a_spec = pl.BlockSpec((tm, tn), lambda j, i: (i, j))
    st_shape = jax.ShapeDtypeStruct((nseq, FFN_CONV - 1, D_FF), F32)
    out_shape = [jax.ShapeDtypeStruct((t, D_FF), BF16), st_shape, st_shape]
    if prev is None:
        assert seg % tm == 0
        tps = seg // tm
        c_in, c_out, c_shapes = _side_cast_specs(cast, nj * ni, lambda j, i: j * ni + i)
        return pl.pallas_call(
            functools.partial(_ffn_up_prompt_body, tiles_per_seq=tps, n_cast=len(cast)),
            grid=(nj, ni),
            in_specs=common_in + c_in,
            out_specs=[a_spec] + [pl.BlockSpec((1, FFN_CONV - 1, tn), lambda j, i: (i // tps, 0, j))] * 2 + c_out,
            out_shape=out_shape + c_shapes,
            scratch_shapes=[pltpu.VMEM((SUBLANES, tn), F32), pltpu.VMEM((SUBLANES, tn), F32)],
            compiler_params=_cparams(("arbitrary", "arbitrary")),
            name="ffn_up_prompt",
        )(h2, w_up, w_up, cw, cw, cb, cb, *cast)
    assert tm % seg == 0 and seg == SUBLANES and not cast
    return pl.pallas_call(
        functools.partial(_ffn_up_sample_body, seg=seg),
        grid=(nj, ni),
        in_specs=common_in + [
            pl.BlockSpec((tm // seg, FFN_CONV - 1, tn), lambda j, i: (i, 0, j)),
            pl.BlockSpec((tm // seg, FFN_CONV - 1, tn), lambda j, i: (i, 0, j + nj)),
        ],
        out_specs=[a_spec] + [pl.BlockSpec((tm // seg, FFN_CONV - 1, tn), lambda j, i: (i, 0, j))] * 2,
        out_shape=out_shape,
        compiler_params=_cparams(("arbitrary", "arbitrary")),
        name="ffn_up_sample",
    )(h2, w_up, w_up, cw, cw, cb, cb, prev, prev)


def _ffn_down_body(a_ref, x_ref, w_ref, nw_ref, y_ref):
    for rs in _row_subs(a_ref.shape[0], 128):
        f = _dot(a_ref[rs, :], w_ref[...])
        y_ref[rs, :] = x_ref[rs, :] + _rms(f, nw_ref[...])


def _ffn_down(act, x1, w_down, nw, *, tm):
    t, k = act.shape
    assert t % tm == 0
    return pl.pallas_call(
        _ffn_down_body,
        grid=(t // tm,),
        in_specs=[
            pl.BlockSpec((tm, k), lambda i: (i, 0)),
            pl.BlockSpec((tm, D_MODEL), lambda i: (i, 0)),
            pl.BlockSpec((k, D_MODEL), lambda i: (0, 0), pipeline_mode=pl.Buffered(1)),
            pl.BlockSpec((1, D_MODEL), lambda i: (0, 0)),
        ],
        out_specs=pl.BlockSpec((tm, D_MODEL), lambda i: (i, 0)),
        out_shape=jax.ShapeDtypeStruct((t, D_MODEL), F32),
        compiler_params=_cparams(("arbitrary",)),
        name="ffn_down",
    )(act, x1, w_down, nw)


def _head_expander():
    e = np.zeros((LANES, SSM_WIDTH), np.float32)
    for h in range(SSM_HEADS):
        e[h, h * SSM_HEAD_DIM:(h + 1) * SSM_HEAD_DIM] = 1.0
    return jnp.asarray(e, BF16)


def _pad_lanes(v):
    return jnp.pad(v, (0, LANES - v.shape[0]))[None, :]


def _prep_params(norm_mix_pre, w_in, gate_ln_g, gate_ln_b, gate_w_s, gate_b_s, ssm_conv_w, ssm_conv_b, ssm_dt_bias,
                 ssm_a_log, ssm_d, ssm_norm_w, w_out, norm_mix_post, norm_ffn_pre, ffn_w_up, ffn_conv_w, ffn_conv_b,
                 ffn_w_down, norm_ffn_post, seg_sample):
    w_in_t = jnp.swapaxes(w_in, 0, 1)
    ws_small = gate_w_s[:, :seg_sample, :seg_sample]
    return dict(
        nw_pre=norm_mix_pre[None, :],
        w_in_t=w_in_t,
        w_dt=jnp.pad(w_in_t[PROJ_MAIN:, :], ((0, LANES - SSM_HEADS), (0, 0))).astype(BF16),
        ln_g=gate_ln_g[None, :], ln_b=gate_ln_b[None, :],
        wt_p=gate_w_s, btT_p=gate_b_s.T,
        wt_s=jnp.tile(ws_small, (1, SEQ_TILE, SEQ_TILE)), btT_s=jnp.tile(gate_b_s[:, :seg_sample], (1, SEQ_TILE)).T,
        conv_w=ssm_conv_w, conv_b=ssm_conv_b[None, :],
        dtb=_pad_lanes(ssm_dt_bias), alog=_pad_lanes(ssm_a_log),
        dexp=jnp.repeat(ssm_d, SSM_HEAD_DIM)[None, :], nwm=ssm_norm_w[None, :],
        e=_head_expander(),
        w_out=w_out, n_post=norm_mix_post[None, :], n_pre2=norm_ffn_pre[None, :],
        w_up=ffn_w_up, fcw=ffn_conv_w, fcb=ffn_conv_b[None, :],
        w_down=ffn_w_down, n_post2=norm_ffn_post[None, :],
    )


def _row_tile(t, cap):
    tm = cap
    while t % tm:
        tm //= 2
    assert tm >= 64
    return tm


TM_STREAM = 1024
TM_OUT_PROJ = 512
TM_FFN_DOWN = 512
TM_PRE_NORM = 1024


def _layer(xp, xs, state_ssm, state_sconv, state_fconv, prm):
    nbp, seq, _ = xp.shape
    nbs, seg, _ = xs.shape
    assert seq % CHUNK == 0 and seg == SUBLANES and nbs % SEQ_TILE == 0
    nc = seq // CHUNK
    ta, tb = nbp * seq, nbs * seg
    xa, xb = xp.reshape(ta, D_MODEL), xs.reshape(tb, D_MODEL)
    common = np.gcd(seq, tb)
    tm = _row_tile(common, TM_STREAM)
    h, dt = _pre_norm(xa, xb, prm["nw_pre"], prm["w_dt"], tm=_row_tile(common, TM_PRE_NORM))
    proj, w_out, w_up = _in_proj(h, prm["w_in_t"], tm=tm, tn=1024, sub=min(tm, 256), cast=(prm["w_out"], prm["w_up"]))

    mixin, cv_p, st_p, sc_p = _mixer_prompt(proj, dt, prm, nb=nbp, nc=nc)
    x1, h2 = _out_proj(mixin, xa, w_out, prm["n_post"], prm["n_pre2"], tm=_row_tile(seq, TM_OUT_PROJ))
    act, fpg, fpu, w_down = _ffn_up(h2, w_up, prm["fcw"], prm["fcb"], None, nseq=nbp, seg=seq,
                                    tm=_row_tile(seq, TM_STREAM), tn=512, cast=(prm["w_down"],))
    y_p = _ffn_down(act, x1, w_down, prm["n_post2"], tm=_row_tile(seq, TM_FFN_DOWN))

    dec = _state_decay(dt[ta:], prm, seg=seg)[:, :SSM_HEADS].reshape(nbs * SSM_HEADS)
    mixin, cv_s, st_s, sc_s = _mixer_sample(dec, proj, dt, ta, state_sconv,
                                            state_ssm.reshape(nbs, SSM_WIDTH, SSM_STATE), prm, seg=seg)
    x1, h2 = _out_proj(mixin, xb, w_out, prm["n_post"], prm["n_pre2"], tm=_row_tile(tb, TM_OUT_PROJ))
    act, fsg, fsu = _ffn_up(h2, w_up, prm["fcw"], prm["fcb"], state_fconv, nseq=nbs, seg=seg,
                            tm=_row_tile(tb, TM_STREAM), tn=512)
    y_s = _ffn_down(act, x1, w_down, prm["n_post2"], tm=_row_tile(tb, TM_FFN_DOWN))
    shape4 = (SSM_HEADS, SSM_HEAD_DIM, SSM_STATE)
    return ((y_p.reshape(nbp, seq, D_MODEL), st_p.reshape(nbp, *shape4), sc_p, jnp.concatenate([fpg, fpu], axis=-1),
             cv_p.reshape(nbp, CHUNK, GATE_HEADS, GATE_HEAD_DIM)),
            (y_s.reshape(nbs, seg, D_MODEL), st_s.reshape(nbs, *shape4), sc_s, jnp.concatenate([fsg, fsu], axis=-1),
             cv_s.reshape(nbs, seg, GATE_HEADS, GATE_HEAD_DIM)))


def kernel(x_prompt, x_sample, state_ssm, state_ssm_conv, state_ffn_conv, norm_mix_pre, w_in, gate_ln_g, gate_ln_b,
           gate_w_s, gate_b_s, ssm_conv_w, ssm_conv_b, ssm_dt_bias, ssm_a_log, ssm_d, ssm_norm_w, w_out,
           norm_mix_post, norm_ffn_pre, ffn_w_up, ffn_conv_w, ffn_conv_b, ffn_w_down, norm_ffn_post):
    depth = w_in.shape[0]
    yp, ys = x_prompt, x_sample
    outs_p, outs_s = [], []
    for l in range(depth):
        prm = _prep_params(norm_mix_pre[l], w_in[l], gate_ln_g[l], gate_ln_b[l], gate_w_s[l], gate_b_s[l],
                           ssm_conv_w[l], ssm_conv_b[l], ssm_dt_bias[l], ssm_a_log[l], ssm_d[l], ssm_norm_w[l],
                           w_out[l], norm_mix_post[l], norm_ffn_pre[l], ffn_w_up[l], ffn_conv_w[l], ffn_conv_b[l],
                           ffn_w_down[l], norm_ffn_post[l], x_sample.shape[1])
        (yp, *rest_p), (ys, *rest_s) = _layer(yp, ys, state_ssm[l], state_ssm_conv[l], state_ffn_conv[l], prm)
        outs_p.append(rest_p)
        outs_s.append(rest_s)
    stack = lambda outs, k: jnp.stack([o[k] for o in outs])
    return (yp, ys,
            stack(outs_p, 0), stack(outs_p, 1), stack(outs_p, 2), stack(outs_p, 3),
            stack(outs_s, 0), stack(outs_s, 1), stack(outs_s, 2), stack(outs_s, 3))
```
